```python
import math
import jax, jax.numpy as jnp
from jax import lax
import numpy as np

D_MODEL = 1024
BATCH = 2
SEQ = 8192
DEPTH = 4
DEC_BATCH = 32
DEC_SEQ = 8
PAST_LEN = 8192
PAGE_SIZE = 128

F32 = jnp.float32
HEAD_DIM = 64
N_BRANCH = 4
BRANCH_W = D_MODEL // N_BRANCH
RET_HEADS = BRANCH_W // HEAD_DIM
RET_THETA = 10000.0
RET_CHUNK = 128
SSM_GROUP_CH = 16
SSM_GROUPS = BRANCH_W // SSM_GROUP_CH
SSM_STATE = 64
SB_HEADS = BRANCH_W // HEAD_DIM
NSA_HEADS = BRANCH_W // HEAD_DIM
CMP_BLOCK = 32
CMP_STRIDE = 16
SLC_BLOCK = 64
N_SEL = 16
WINDOW = 512
ROPE_THETA = 500000.0
ROPE_DIM = HEAD_DIM // 4
Q_BLOCK = 128
D_FF = ((8 * D_MODEL + 3 * 256 - 1) // (3 * 256)) * 256
DEEPNORM_ALPHA = (2.0 * DEPTH) ** 0.25
DEEPNORM_BETA = (8.0 * DEPTH) ** -0.25
LN_EPS = 1e-5
IN_WIDTHS = (
    BRANCH_W, BRANCH_W, BRANCH_W, BRANCH_W,
    BRANCH_W,
    BRANCH_W, BRANCH_W, BRANCH_W,
    BRANCH_W,
    HEAD_DIM, HEAD_DIM, HEAD_DIM, HEAD_DIM, HEAD_DIM, HEAD_DIM,
    3 * NSA_HEADS,
    N_BRANCH * D_MODEL,
)

kernel_name = 'hybrid_ret_s5_stickbreak_nsa_step'


def _split_in(z):
    idx = np.cumsum(np.array(IN_WIDTHS))[:-1].tolist()
    return jnp.split(z, idx, axis=-1)


def _ln(x, g=None, bias=None):
    xf = x.astype(F32)
    mu = jnp.mean(xf, -1, keepdims=True)
    var = jnp.mean(jnp.square(xf - mu), -1, keepdims=True)
    y = (xf - mu) * lax.rsqrt(var + LN_EPS)
    if g is not None:
        y = y * g + bias
    return y


def _rope(x, pos, rot_dim, theta):
    half = rot_dim // 2
    inv = theta ** (-jnp.arange(half, dtype=F32) / half)
    ang = pos.astype(F32)[:, None] * inv[None, :]
    cos = jnp.cos(ang)[None, :, None, :]
    sin = jnp.sin(ang)[None, :, None, :]
    x1 = x[..., :half].astype(F32)
    x2 = x[..., half:rot_dim].astype(F32)
    return jnp.concatenate([x1 * cos - x2 * sin, x1 * sin + x2 * cos, x[..., rot_dim:].astype(F32)], axis=-1)


def _to_blocks(t, nb):
    return t.reshape((t.shape[0], nb, Q_BLOCK) + t.shape[2:]).swapaxes(0, 1)


def _from_blocks(t):
    t = t.swapaxes(0, 1)
    return t.reshape((t.shape[0], t.shape[1] * t.shape[2]) + t.shape[3:])


def _retention(q, k, v, r0, chunk):
    b, l, h, _ = q.shape
    dv = v.shape[-1]
    n = l // chunk
    log_g = jnp.log1p(-jnp.exp2(-5.0 - jnp.arange(h, dtype=F32)))
    idx = jnp.arange(chunk, dtype=F32)
    diff = idx[:, None] - idx[None, :]
    intra = jnp.where(diff >= 0, jnp.exp(jnp.maximum(diff, 0.0)[None] * log_g[:, None, None]), 0.0)
    q_dec = jnp.exp((idx + 1.0)[None, :] * log_g[:, None])
    k_dec = jnp.exp((chunk - 1.0 - idx)[None, :] * log_g[:, None])
    c_dec = jnp.exp(chunk * log_g)

    def to_chunks(t):
        return t.astype(F32).reshape(b, n, chunk, h, -1).transpose(1, 0, 3, 2, 4)

    def step(r, inp):
        qc, kc, vc = inp
        s = jnp.einsum('bhcd,bhed->bhce', qc, kc) * intra
        o = jnp.einsum('bhce,bhev->bhcv', s, vc) + jnp.einsum('bhcd,bhdv->bhcv', qc * q_dec[None, :, :, None], r)
        r = r * c_dec[None, :, None, None] + jnp.einsum('bhcd,bhcv->bhdv', kc * k_dec[None, :, :, None], vc)
        return r, o

    r, o = lax.scan(step, r0.astype(F32), (to_chunks(q), to_chunks(k), to_chunks(v)))
    return o.transpose(1, 0, 3, 2, 4).reshape(b, l, h, dv), r


def _head_norm(x):
    xf = x.astype(F32)
    mu = jnp.mean(xf, -1, keepdims=True)
    var = jnp.mean(jnp.square(xf - mu), -1, keepdims=True)
    return (xf - mu) * lax.rsqrt(var + LN_EPS)


def _s5(u, x0, a_re, a_im, b_re, b_im, c_re, c_im, d_skip, log_dt):
    bsz, l, _ = u.shape
    ug = u.astype(F32).reshape(bsz, l, SSM_GROUPS, SSM_GROUP_CH)
    dt = jnp.exp(log_dt.astype(F32))[:, None]
    ar, ai = a_re.astype(F32), a_im.astype(F32)
    mag = jnp.exp(ar * dt)
    abr, abi = mag * jnp.cos(ai * dt), mag * jnp.sin(ai * dt)
    nr, ni = abr - 1.0, abi
    den = ar * ar + ai * ai
    fr = (nr * ar + ni * ai) / den
    fi = (ni * ar - nr * ai) / den
    br, bi = b_re.astype(F32), b_im.astype(F32)
    bbr = fr[..., None] * br - fi[..., None] * bi
    bbi = fr[..., None] * bi + fi[..., None] * br
    bu_r = jnp.einsum('blgc,gpc->blgp', ug, bbr)
    bu_i = jnp.einsum('blgc,gpc->blgp', ug, bbi)
    x0r, x0i = x0[..., 0].astype(F32), x0[..., 1].astype(F32)
    bu_r = bu_r.at[:, 0].add(abr * x0r - abi * x0i)
    bu_i = bu_i.at[:, 0].add(abr * x0i + abi * x0r)
    a_r_t = jnp.broadcast_to(abr, bu_r.shape)
    a_i_t = jnp.broadcast_to(abi, bu_r.shape)

    def combine(e1, e2):
        a1r, a1i, b1r, b1i = e1
        a2r, a2i, b2r, b2i = e2
        return (a2r * a1r - a2i * a1i, a2r * a1i + a2i * a1r,
                a2r * b1r - a2i * b1i + b2r, a2r * b1i + a2i * b1r + b2i)

    _, _, xr, xi = lax.associative_scan(combine, (a_r_t, a_i_t, bu_r, bu_i), axis=1)
    y = (jnp.einsum('blgp,gcp->blgc', xr, c_re.astype(F32)) - jnp.einsum('blgp,gcp->blgc', xi, c_im.astype(F32))
         + d_skip.astype(F32).reshape(SSM_GROUPS, SSM_GROUP_CH) * ug)
    return y.reshape(bsz, l, -1), jnp.stack([xr[:, -1], xi[:, -1]], axis=-1)


def _stick_breaking(q, q_pos, k, v, k_pos):
    z = jnp.einsum('bqhd,bkhd->bhqk', q, k).astype(F32) * (HEAD_DIM ** -0.5)
    causal = k_pos[None, :] < q_pos[:, None]
    log_1m = jnp.where(causal, jax.nn.log_sigmoid(-z), 0.0)
    suffix = lax.cumsum(log_1m, axis=3, reverse=True) - log_1m
    w = jnp.where(causal, jnp.exp(jax.nn.log_sigmoid(z) + suffix), 0.0)
    return jnp.einsum('bhqk,bkhd->bqhd', w, v.astype(F32))


def _sb_prompt(q, k, v, pos):
    nb = q.shape[1] // Q_BLOCK
    out = lax.map(lambda a: _stick_breaking(a[0], a[1], k, v, pos),
                  (_to_blocks(q, nb), pos.reshape(nb, Q_BLOCK)))
    return _from_blocks(out)


def _compress(rows, w):
    b, t, d = rows.shape
    chunks = rows.astype(F32).reshape(b, t // CMP_STRIDE, CMP_STRIDE, d)
    n_sub = CMP_BLOCK // CMP_STRIDE
    nc = t // CMP_STRIDE - n_sub + 1
    out = jnp.einsum('bnsd,sde->bne', chunks[:, 0:nc], w[0:CMP_STRIDE])
    for m in range(1, n_sub):
        out = out + jnp.einsum('bnsd,sde->bne', chunks[:, m:m + nc], w[m * CMP_STRIDE:(m + 1) * CMP_STRIDE])
    return out


def _cmp_to_slc(nc, ns):
    cs = jnp.arange(nc) * CMP_STRIDE
    ss = jnp.arange(ns) * SLC_BLOCK
    return ((cs[:, None] < ss[None, :] + SLC_BLOCK) & (ss[None, :] < cs[:, None] + CMP_BLOCK)).astype(F32)


def _nsa_cmp_slc(q_plain, q_rot, q_pos, kcmp, vcmp, ks_blk, vs_blk, cmp_to_slc):
    b, tq = q_plain.shape[:2]
    scale = HEAD_DIM ** -0.5
    nc = kcmp.shape[1]
    ns = ks_blk.shape[1]
    cmp_end = jnp.arange(nc) * CMP_STRIDE + (CMP_BLOCK - 1)
    cmask = cmp_end[None, :] <= q_pos[:, None]
    s = jnp.where(cmask, jnp.einsum('bqhd,bnd->bhqn', q_plain, kcmp).astype(F32) * scale, -jnp.inf)
    m = jnp.max(s, axis=-1, keepdims=True)
    m = jnp.where(jnp.isfinite(m), m, 0.0)
    e = jnp.where(cmask, jnp.exp(s - m), 0.0)
    p = e / jnp.maximum(jnp.sum(e, -1, keepdims=True), 1e-30)
    o_cmp = jnp.einsum('bhqn,bnd->bqhd', p, vcmp.astype(F32))
    imp = jnp.einsum('bhqn,ns->bqs', p, cmp_to_slc)
    blk = jnp.arange(ns)
    cur = q_pos // SLC_BLOCK
    causal = (blk[None, :] * SLC_BLOCK) <= q_pos[:, None]
    forced = (blk[None, :] == 0) | (blk[None, :] == cur[:, None]) | (blk[None, :] == cur[:, None] - 1)
    imp = jnp.where(causal[None], imp, -jnp.inf)
    imp = jnp.where(forced[None], jnp.inf, imp)
    n_top = min(N_SEL, ns)
    _, idx = lax.top_k(imp, n_top)
    take = jax.vmap(lambda blocks, ix: blocks[ix])
    kg = take(ks_blk, idx).reshape(b, tq, n_top * SLC_BLOCK, HEAD_DIM)
    vg = take(vs_blk, idx).reshape(b, tq, n_top * SLC_BLOCK, HEAD_DIM)
    tok_pos = (idx[..., None] * SLC_BLOCK + jnp.arange(SLC_BLOCK)).reshape(b, tq, n_top * SLC_BLOCK)
    smask = tok_pos <= q_pos[None, :, None]
    ss = jnp.einsum('bqhd,bqkd->bhqk', q_rot, kg).astype(F32) * scale
    ps = jax.nn.softmax(jnp.where(smask[:, None], ss, -jnp.inf), axis=-1)
    o_slc = jnp.einsum('bhqk,bqkd->bqhd', ps, vg.astype(F32))
    return o_cmp, o_slc


def _window_attend(q, q_pos, k, v, k_pos):
    s = jnp.einsum('bqhd,bkd->bhqk', q, k).astype(F32) * (HEAD_DIM ** -0.5)
    dist = q_pos[:, None] - k_pos[None, :]
    mask = (dist >= 0) & (dist < WINDOW) & (k_pos[None, :] >= 0)
    p = jax.nn.softmax(jnp.where(mask, s, -jnp.inf), axis=-1)
    return jnp.einsum('bhqk,bkd->bqhd', p, v.astype(F32))


def _token_mixers(h, pos, lw, past):
    b, l, _ = h.shape
    (rq, rk, rv, rg, su, sq, sk, sv, nq, nkc, nvc, nks, nvs, nkw, nvw, ng, mg) = _split_in(h @ lw['w_in'])

    def hd(t):
        return t.reshape(b, l, -1, HEAD_DIM)

    rq = _rope(hd(rq), pos, HEAD_DIM, RET_THETA)
    rk = _rope(hd(rk), pos, HEAD_DIM, RET_THETA) * (HEAD_DIM ** -0.5)
    r0 = jnp.zeros((b, RET_HEADS, HEAD_DIM, HEAD_DIM), F32) if past is None else past[3]
    chunk = RET_CHUNK if l % RET_CHUNK == 0 else l
    ro, r_new = _retention(rq, rk, hd(rv), r0, chunk)
    out_a = (_head_norm(ro) * jax.nn.silu(hd(rg).astype(F32))).reshape(b, l, BRANCH_W)

    s0 = jnp.zeros((b, SSM_GROUPS, SSM_STATE, 2), F32) if past is None else past[4]
    y_ssm, s_new = _s5(su, s0, lw['a_re'], lw['a_im'], lw['b_re'], lw['b_im'], lw['c_re'], lw['c_im'],
                       lw['d_skip'], lw['log_dt'])
    zb = jax.nn.gelu(y_ssm)
    out_b = zb * jax.nn.sigmoid(zb @ lw['w_glu'] + lw['b_glu'])

    sq, sk, sv = hd(sq), hd(sk), hd(sv)
    if past is None:
        out_c = _sb_prompt(sq, sk, sv, pos)
    else:
        kvp = past[0]
        t_all = kvp.shape[1] + l
        out_c = _stick_breaking(sq, pos, jnp.concatenate([kvp[:, :, 0], sk], 1),
                                jnp.concatenate([kvp[:, :, 1], sv], 1), jnp.arange(t_all))
    out_c = out_c.reshape(b, l, BRANCH_W)
    sb_rows = jnp.stack([sk, sv], axis=2)

    nq = hd(nq)
    nq_r = _rope(nq, pos, ROPE_DIM, ROPE_THETA)
    ks_r = _rope(nks[:, :, None], pos, ROPE_DIM, ROPE_THETA)[:, :, 0]
    kw_r = _rope(nkw[:, :, None], pos, ROPE_DIM, ROPE_THETA)[:, :, 0]
    nsa_rows = jnp.stack([nkc.astype(F32), nvc.astype(F32), ks_r, nvs.astype(F32)], axis=2)
    win_rows = jnp.stack([kw_r, nvw.astype(F32)], axis=2)
    rows = nsa_rows if past is None else jnp.concatenate([past[1], nsa_rows], 1)
    t_all = rows.shape[1]
    t_pad = -(-t_all // SLC_BLOCK) * SLC_BLOCK
    rows = jnp.pad(rows, ((0, 0), (0, t_pad - t_all), (0, 0), (0, 0)))
    kcmp = _compress(rows[:, :, 0], lw['w_cmp_k'])
    vcmp = _compress(rows[:, :, 1], lw['w_cmp_v'])
    n_slc = t_pad // SLC_BLOCK
    ks_blk = rows[:, :, 2].reshape(b, n_slc, SLC_BLOCK, HEAD_DIM)
    vs_blk = rows[:, :, 3].reshape(b, n_slc, SLC_BLOCK, HEAD_DIM)
    c2s = _cmp_to_slc(kcmp.shape[1], n_slc)
    if past is None:
        nb = l // Q_BLOCK
        n_prev = WINDOW // Q_BLOCK
        wpad = jnp.concatenate([jnp.zeros((b, WINDOW, 2, HEAD_DIM), win_rows.dtype), win_rows], 1)
        wblocks = wpad.reshape(b, nb + n_prev, Q_BLOCK, 2, HEAD_DIM)
        band_idx = jnp.arange(nb)[:, None] + jnp.arange(n_prev + 1)[None, :]
        wband = wblocks[:, band_idx].reshape(b, nb, (n_prev + 1) * Q_BLOCK, 2, HEAD_DIM).swapaxes(0, 1)
        band_pos = ((band_idx - n_prev)[:, :, None] * Q_BLOCK + jnp.arange(Q_BLOCK)[None, None, :]).reshape(nb, -1)

        def nsa_block(a):
            qp, qr, qpos, wb, wpos = a
            o_c, o_s = _nsa_cmp_slc(qp, qr, qpos, kcmp, vcmp, ks_blk, vs_blk, c2s)
            return o_c, o_s, _window_attend(qr, qpos, wb[:, :, 0], wb[:, :, 1], wpos)

        o_c, o_s, o_w = lax.map(nsa_block, (_to_blocks(nq, nb), _to_blocks(nq_r, nb),
                                            pos.reshape(nb, Q_BLOCK), wband, band_pos))
        o_c, o_s, o_w = _from_blocks(o_c), _from_blocks(o_s), _from_blocks(o_w)
        win_state = win_rows[:, l - min(WINDOW, l):]
    else:
        o_c, o_s = _nsa_cmp_slc(nq, nq_r, pos, kcmp, vcmp, ks_blk, vs_blk, c2s)
        wbuf = past[2]
        wb_len = wbuf.shape[1]
        wk = jnp.concatenate([wbuf, win_rows], 1)
        past_len = t_all - l
        w_pos = jnp.arange(past_len - wb_len, past_len + l)
        o_w = _window_attend(nq_r, pos, wk[:, :, 0], wk[:, :, 1], w_pos)
        win_state = wk[:, l:]
    g = jax.nn.sigmoid(ng.astype(F32)).reshape(b, l, 3, NSA_HEADS)[..., None]
    out_d = (g[:, :, 0] * o_c + g[:, :, 1] * o_s + g[:, :, 2] * o_w).reshape(b, l, BRANCH_W)

    branches = jnp.stack([out_a, out_b, out_c, out_d], axis=2)
    br = jnp.einsum('blnw,nwd->blnd', branches, lw['w_branch'])
    gate = jax.nn.sigmoid(mg.reshape(b, l, N_BRANCH, D_MODEL))
    y = jnp.sum(gate * br, axis=2) @ lw['w_out']
    return y, (sb_rows, nsa_rows, win_state, r_new, s_new)


def _layer(x, c, pos, lw, past):
    b = x.shape[0]
    mod = (jax.nn.silu(c) @ lw['w_ada'] + lw['b_ada']).reshape(b, 6, 1, D_MODEL)
    h = _ln(x) * (1.0 + mod[:, 1]) + mod[:, 0]
    y, state = _token_mixers(h, pos, lw, past)
    x = _ln(DEEPNORM_ALPHA * x + (1.0 + mod[:, 2]) * y, lw['ln1_g'], lw['ln1_b'])
    h = _ln(x) * (1.0 + mod[:, 4]) + mod[:, 3]
    a, u = jnp.split(h @ lw['w_ffn_up'], 2, axis=-1)
    f = (jax.nn.silu(a) * u) @ lw['w_ffn_down']
    x = _ln(DEEPNORM_ALPHA * x + (1.0 + mod[:, 5]) * f, lw['ln2_g'], lw['ln2_b'])
    return x, state


def setup_inputs(seed: int = 0) -> dict:
    key = jax.random.key(seed)
    ks = iter(jax.random.split(key, 48))

    def nrm(shape, scale):
        return jax.random.normal(next(ks), shape, F32) * scale

    n_pages = PAST_LEN // PAGE_SIZE
    n_used = DEC_BATCH * n_pages
    n_pool = n_used + (n_used + 3) // 4
    wb = min(WINDOW, PAST_LEN)
    in_cols = sum(IN_WIDTHS)
    page_table = jax.random.permutation(next(ks), n_pool)[:n_used].reshape(DEC_BATCH, n_pages).astype(jnp.int32)
    return {
        'x_prompt': nrm((BATCH, SEQ, D_MODEL), 1.0),
        'x_sample': nrm((DEC_BATCH, DEC_SEQ, D_MODEL), 1.0),
        'c_prompt': nrm((BATCH, D_MODEL), 1.0),
        'c_sample': nrm((DEC_BATCH, D_MODEL), 1.0),
        'cache_sb': nrm((DEPTH, n_pool, PAGE_SIZE, 2, SB_HEADS, HEAD_DIM), 1.0),
        'cache_nsa': nrm((DEPTH, n_pool, PAGE_SIZE, 4, HEAD_DIM), 1.0),
        'cache_win': nrm((DEPTH, DEC_BATCH, wb, 2, HEAD_DIM), 1.0),
        'state_ret': nrm((DEPTH, DEC_BATCH, RET_HEADS, HEAD_DIM, HEAD_DIM), 0.5),
        'state_ssm': nrm((DEPTH, DEC_BATCH, SSM_GROUPS, SSM_STATE, 2), 0.1),
        'page_table': page_table,
        'w_ada': nrm((DEPTH, D_MODEL, 6 * D_MODEL), 0.1 * D_MODEL ** -0.5),
        'b_ada': nrm((DEPTH, 6 * D_MODEL), 0.01),
        'w_in': nrm((DEPTH, D_MODEL, in_cols), D_MODEL ** -0.5),
        'ssm_a_re': -0.5 * jnp.exp(nrm((DEPTH, SSM_GROUPS, SSM_STATE), 0.01)),
        'ssm_a_im': jnp.broadcast_to(jnp.pi * jnp.arange(SSM_STATE, dtype=F32), (DEPTH, SSM_GROUPS, SSM_STATE)),
        'ssm_b_re': nrm((DEPTH, SSM_GROUPS, SSM_STATE, SSM_GROUP_CH), (2 * SSM_GROUP_CH) ** -0.5),
        'ssm_b_im': nrm((DEPTH, SSM_GROUPS, SSM_STATE, SSM_GROUP_CH), (2 * SSM_GROUP_CH) ** -0.5),
        'ssm_c_re': nrm((DEPTH, SSM_GROUPS, SSM_GROUP_CH, SSM_STATE), (2 * SSM_STATE) ** -0.5),
        'ssm_c_im': nrm((DEPTH, SSM_GROUPS, SSM_GROUP_CH, SSM_STATE), (2 * SSM_STATE) ** -0.5),
        'ssm_d': nrm((DEPTH, BRANCH_W), 1.0),
        'ssm_log_dt': jax.random.uniform(next(ks), (DEPTH, SSM_GROUPS), F32, math.log(1e-3), math.log(1e-1)),
        'w_glu': nrm((DEPTH, BRANCH_W, BRANCH_W), BRANCH_W ** -0.5),
        'b_glu': nrm((DEPTH, BRANCH_W), 0.01),
        'w_cmp_k': nrm((DEPTH, CMP_BLOCK, HEAD_DIM, HEAD_DIM), (CMP_BLOCK * HEAD_DIM) ** -0.5),
        'w_cmp_v': nrm((DEPTH, CMP_BLOCK, HEAD_DIM, HEAD_DIM), (CMP_BLOCK * HEAD_DIM) ** -0.5),
        'w_branch': nrm((DEPTH, N_BRANCH, BRANCH_W, D_MODEL), BRANCH_W ** -0.5),
        'w_out': nrm((DEPTH, D_MODEL, D_MODEL), DEEPNORM_BETA * D_MODEL ** -0.5),
        'ln1_g': 1.0 + nrm((DEPTH, D_MODEL), 0.01),
        'ln1_b': nrm((DEPTH, D_MODEL), 0.01),
        'w_ffn_up': nrm((DEPTH, D_MODEL, 2 * D_FF), D_MODEL ** -0.5),
        'w_ffn_down': nrm((DEPTH, D_FF, D_MODEL), DEEPNORM_BETA * D_FF ** -0.5),
        'ln2_g': 1.0 + nrm((DEPTH, D_MODEL), 0.01),
        'ln2_b': nrm((DEPTH, D_MODEL), 0.01),
    }


def reference(x_prompt, x_sample, c_prompt, c_sample, cache_sb, cache_nsa, cache_win, state_ret, state_ssm,
              page_table, w_ada, b_ada, w_in, ssm_a_re, ssm_a_im, ssm_b_re, ssm_b_im, ssm_c_re, ssm_c_im,
              ssm_d, ssm_log_dt, w_glu, b_glu, w_cmp_k, w_cmp_v, w_branch, w_out, ln1_g, ln1_b,
              w_ffn_up, w_ffn_down, ln2_g, ln2_b):
    dec_b = x_sample.shape[0]
    past_len = page_table.shape[1] * PAGE_SIZE
    pos_p = jnp.arange(x_prompt.shape[1], dtype=jnp.int32)
    pos_s = past_len + jnp.arange(x_sample.shape[1], dtype=jnp.int32)
    xp, xs = x_prompt, x_sample
    st_p = [[], [], [], [], []]
    st_s = [[], [], [], [], []]
    for i in range(DEPTH):
        lw = {'w_ada': w_ada[i], 'b_ada': b_ada[i], 'w_in': w_in[i],
              'a_re': ssm_a_re[i], 'a_im': ssm_a_im[i], 'b_re': ssm_b_re[i], 'b_im': ssm_b_im[i],
              'c_re': ssm_c_re[i], 'c_im': ssm_c_im[i], 'd_skip': ssm_d[i], 'log_dt': ssm_log_dt[i],
              'w_glu': w_glu[i], 'b_glu': b_glu[i], 'w_cmp_k': w_cmp_k[i], 'w_cmp_v': w_cmp_v[i],
              'w_branch': w_branch[i], 'w_out': w_out[i], 'ln1_g': ln1_g[i], 'ln1_b': ln1_b[i],
              'w_ffn_up': w_ffn_up[i], 'w_ffn_down': w_ffn_down[i], 'ln2_g': ln2_g[i], 'ln2_b': ln2_b[i]}
        xp, new_p = _layer(xp, c_prompt, pos_p, lw, None)
        past = (cache_sb[i][page_table].reshape(dec_b, past_len, 2, SB_HEADS, HEAD_DIM),
                cache_nsa[i][page_table].reshape(dec_b, past_len, 4, HEAD_DIM),
                cache_win[i], state_ret[i], state_ssm[i])
        xs, new_s = _layer(xs, c_sample, pos_s, lw, past)
        for j in range(5):
            st_p[j].append(new_p[j])
            st_s[j].append(new_s[j])
    sb_p, nsa_p, win_p, ret_p, ssm_p = [jnp.stack(t, axis=0) for t in st_p]
    sb_s, nsa_s, win_s, ret_s, ssm_s = [jnp.stack(t, axis=0) for t in st_s]
    return (xp, xs, sb_p, sb_s, nsa_p, nsa_s, win_p, win_s, ret_p, ret_s, ssm_p, ssm_s)
```

```python
import functools
import math

import jax
import jax.numpy as jnp
import numpy as np
from jax import lax
from jax.experimental import pallas as pl
from jax.experimental.pallas import tpu as pltpu

F32 = jnp.float32
BF16 = jnp.bfloat16

LANES = 128
SUBLANES = 8
VMEM_LIMIT_BYTES = 56 * 1024 * 1024

HEAD_DIM = 64
N_HEADS = 4
BRANCH_W = N_HEADS * HEAD_DIM
N_BRANCH = 4
RET_THETA = 10000.0
RET_CHUNK = 128
SSM_GROUP_CH = 16
SSM_STATE = 64
CMP_BLOCK = 32
CMP_STRIDE = 16
SLC_BLOCK = 64
N_SEL = 16
WINDOW = 512
ROPE_THETA = 500000.0
ROPE_DIM = HEAD_DIM // 4
LN_EPS = 1e-5
QK_SCALE = HEAD_DIM ** -0.5
NEG_BIG = -1e30


def _cparams(*sem):
    return pltpu.CompilerParams(dimension_semantics=tuple(sem), vmem_limit_bytes=VMEM_LIMIT_BYTES)


def _lane_iota(shape):
    return lax.broadcasted_iota(jnp.int32, shape, len(shape) - 1)


def _row_iota(shape):
    return lax.broadcasted_iota(jnp.int32, shape, len(shape) - 2)


def _dot(a, b):
    return jnp.dot(a.astype(BF16), b.astype(BF16), preferred_element_type=F32)


def _dot_nt(a, b):
    return lax.dot_general(a.astype(BF16), b.astype(BF16), (((1,), (1,)), ((), ())), preferred_element_type=F32)


def _dot_tn(a, b):
    return lax.dot_general(a.astype(BF16), b.astype(BF16), (((0,), (0,)), ((), ())), preferred_element_type=F32)


def _ln_rows(x):
    mu = jnp.mean(x, axis=-1, keepdims=True)
    xc = x - mu
    var = jnp.mean(xc * xc, axis=-1, keepdims=True)
    return xc * lax.rsqrt(var + LN_EPS)


def _sigmoid(x):
    return 1.0 / (1.0 + jnp.exp(-x))


def _silu(x):
    return x * _sigmoid(x)


def _head_mask(shape, h):
    lane = _lane_iota(shape)
    return (lane >= h * HEAD_DIM) & (lane < (h + 1) * HEAD_DIM)


def _ada_kernel(c_ref, w_ref, b_ref, o_ref):
    o_ref[0] = _dot(_silu(c_ref[...]), w_ref[0]) + b_ref[0]


def _ada_call(c_all, w_ada, b_ada):
    depth, d, n = w_ada.shape
    m = c_all.shape[0]
    tn = 1536
    return pl.pallas_call(
        _ada_kernel,
        grid=(depth, n // tn),
        in_specs=[pl.BlockSpec((m, d), lambda i, j: (0, 0)),
                  pl.BlockSpec((1, d, tn), lambda i, j: (i, 0, j)),
                  pl.BlockSpec((1, 1, tn), lambda i, j: (i, 0, j))],
        out_specs=pl.BlockSpec((1, m, tn), lambda i, j: (i, 0, j)),
        out_shape=jax.ShapeDtypeStruct((depth, m, n), F32),
        compiler_params=_cparams("parallel", "parallel"),
        name="ada_mod",
    )(c_all, w_ada, b_ada.reshape(depth, 1, n))


def _rope128(x, cos, sin_signed, half):
    first = (_lane_iota(x.shape) % HEAD_DIM) < half
    partner = jnp.where(first, pltpu.roll(x, LANES - half, 1), pltpu.roll(x, half, 1))
    return x * cos + partner * sin_signed


def _mod_rows(m_ref, row, per_token):
    if per_token:
        d = m_ref.shape[-1] // 8
        return m_ref[:, row * d:(row + 1) * d]
    return m_ref[0, row:row + 1, :]


def _in_kernel(per_token, x_ref, m_ref, wr_ref, wu_ref, wsb_ref, wn_ref, tab_ref,
               ret_ref, u_ref, sbq_ref, sbkv_ref, sbkv16_ref, nq_ref, nqr_ref, rows_ref, win_ref, ng_ref,
               rowsc_ref, rowss16_ref, win16_ref):
    x = x_ref[...]
    h = (_ln_rows(x) * (1.0 + _mod_rows(m_ref, 1, per_token)) + _mod_rows(m_ref, 0, per_token)).astype(BF16)
    tab = tab_ref[...]
    rc, rs = tab[:, 0:128], tab[:, 128:256]
    qc, qs = tab[:, 256:384], tab[:, 384:512]
    kc, ks = tab[:, 512:640], tab[:, 640:768]

    zr = jnp.dot(h, wr_ref[...], preferred_element_type=F32)
    half_r = HEAD_DIM // 2
    for c in range(2):
        ret_ref[:, c * 128:(c + 1) * 128] = _rope128(zr[:, c * 128:(c + 1) * 128], rc, rs, half_r)
    for c in range(2, 4):
        ret_ref[:, c * 128:(c + 1) * 128] = _rope128(zr[:, c * 128:(c + 1) * 128], rc, rs, half_r) * QK_SCALE
    ret_ref[:, 512:1024] = zr[:, 512:1024]

    u_ref[...] = jnp.dot(h, wu_ref[...], preferred_element_type=F32)

    zs = jnp.dot(h, wsb_ref[...], preferred_element_type=F32)
    sbq_ref[...] = (zs[:, 0:256] * QK_SCALE).astype(BF16)
    sbkv_ref[...] = zs[:, 256:768]
    sbkv16_ref[...] = zs[:, 256:768].astype(BF16)

    zn = jnp.dot(h, wn_ref[...], preferred_element_type=F32)
    half_n = ROPE_DIM // 2
    nq_ref[...] = (zn[:, 0:256] * QK_SCALE).astype(BF16)
    for c in range(2):
        nqr_ref[:, c * 128:(c + 1) * 128] = (
            _rope128(zn[:, c * 128:(c + 1) * 128], qc, qs, half_n) * QK_SCALE).astype(BF16)
    rows_slc = _rope128(zn[:, 384:512], kc, ks, half_n)
    rows_ref[:, 0:128] = zn[:, 256:384]
    rows_ref[:, 128:256] = rows_slc
    rowsc_ref[...] = zn[:, 256:384]
    rowss16_ref[...] = rows_slc.astype(BF16)
    win = _rope128(zn[:, 512:640], kc, ks, half_n)
    win_ref[...] = win
    win16_ref[...] = win.astype(BF16)
    ng_ref[...] = _sigmoid(zn[:, 640:768])


def _in_call(x2, mod, tabs, w, tiles_per_batch, tm):
    t, d = x2.shape
    per_token = tiles_per_batch is None
    nt = t // tm
    if per_token:
        mod_spec = pl.BlockSpec((tm, 8 * d), lambda i: (i, 0))
        tab_spec = pl.BlockSpec((tm, 768), lambda i: (i, 0))
    else:
        mod_spec = pl.BlockSpec((1, 8, d), lambda i: (i // tiles_per_batch, 0, 0))
        tab_spec = pl.BlockSpec((tm, 768), lambda i: (i % tiles_per_batch, 0))

    def wspec(a):
        return pl.BlockSpec(a.shape, lambda i: (0, 0))

    def ospec(n):
        return pl.BlockSpec((tm, n), lambda i: (i, 0))

    outs = [(1024, F32), (256, F32), (256, BF16), (512, F32), (512, BF16), (256, BF16), (256, BF16),
            (256, F32), (128, F32), (128, F32), (128, F32), (128, BF16), (128, BF16)]
    return pl.pallas_call(
        functools.partial(_in_kernel, per_token),
        grid=(nt,),
        in_specs=[pl.BlockSpec((tm, d), lambda i: (i, 0)), mod_spec,
                  wspec(w["w_ret"]), wspec(w["w_u"]), wspec(w["w_sb"]), wspec(w["w_nsa"]), tab_spec],
        out_specs=[ospec(n) for n, _ in outs],
        out_shape=[jax.ShapeDtypeStruct((t, n), dt) for n, dt in outs],
        compiler_params=_cparams("parallel"),
        name="in_proj",
    )(x2, mod, w["w_ret"], w["w_u"], w["w_sb"], w["w_nsa"], tabs)


def _merge_kernel(per_token, alpha, x_ref, m_ref, a_ref, b_ref, c_ref, d_ref, wmg_ref, wbr_ref, wout_ref,
                  g_ref, bias_ref, o_ref):
    x = x_ref[...]
    dm = x.shape[-1]
    h = (_ln_rows(x) * (1.0 + _mod_rows(m_ref, 1, per_token)) + _mod_rows(m_ref, 0, per_token)).astype(BF16)
    acc = jnp.zeros(x.shape, F32)
    for n, br_ref in enumerate((a_ref, b_ref, c_ref, d_ref)):
        mg = jnp.dot(h, wmg_ref[:, n * dm:(n + 1) * dm], preferred_element_type=F32)
        br = jnp.dot(br_ref[...].astype(BF16), wbr_ref[n], preferred_element_type=F32)
        acc = acc + _sigmoid(mg) * br
    y = jnp.dot(acc.astype(BF16), wout_ref[...], preferred_element_type=F32)
    r = alpha * x + (1.0 + _mod_rows(m_ref, 2, per_token)) * y
    o_ref[...] = _ln_rows(r) * g_ref[...] + bias_ref[...]


def _merge_call(x2, mod, branches, w, tiles_per_batch, tm, alpha):
    t, d = x2.shape
    per_token = tiles_per_batch is None
    if per_token:
        mod_spec = pl.BlockSpec((tm, 8 * d), lambda i: (i, 0))
    else:
        mod_spec = pl.BlockSpec((1, 8, d), lambda i: (i // tiles_per_batch, 0, 0))
    row = lambda n: pl.BlockSpec((tm, n), lambda i: (i, 0))
    return pl.pallas_call(
        functools.partial(_merge_kernel, per_token, alpha),
        grid=(t // tm,),
        in_specs=[row(d), mod_spec, row(BRANCH_W), row(BRANCH_W), row(BRANCH_W), row(BRANCH_W),
                  pl.BlockSpec(w["w_mg"].shape, lambda i: (0, 0)),
                  pl.BlockSpec(w["w_branch"].shape, lambda i: (0, 0, 0)),
                  pl.BlockSpec(w["w_out"].shape, lambda i: (0, 0)),
                  pl.BlockSpec((1, d), lambda i: (0, 0)), pl.BlockSpec((1, d), lambda i: (0, 0))],
        out_specs=row(d),
        out_shape=jax.ShapeDtypeStruct((t, d), F32),
        compiler_params=_cparams("parallel"),
        name="merge_out",
    )(x2, mod, *branches, w["w_mg"], w["w_branch"], w["w_out"], w["ln1_g"], w["ln1_b"])


def _ffn_kernel(per_token, alpha, n_chunk, x_ref, m_ref, wup_ref, wdn_ref, g_ref, bias_ref, o_ref):
    x = x_ref[...]
    dff = wdn_ref.shape[0]
    ck = dff // n_chunk
    h = (_ln_rows(x) * (1.0 + _mod_rows(m_ref, 4, per_token)) + _mod_rows(m_ref, 3, per_token)).astype(BF16)
    f = jnp.zeros(x.shape, F32)
    for c in range(n_chunk):
        a = jnp.dot(h, wup_ref[:, c * ck:(c + 1) * ck], preferred_element_type=F32)
        u = jnp.dot(h, wup_ref[:, dff + c * ck:dff + (c + 1) * ck], preferred_element_type=F32)
        f = f + jnp.dot((_silu(a) * u).astype(BF16), wdn_ref[c * ck:(c + 1) * ck, :], preferred_element_type=F32)
    r = alpha * x + (1.0 + _mod_rows(m_ref, 5, per_token)) * f
    o_ref[...] = _ln_rows(r) * g_ref[...] + bias_ref[...]


def _ffn_call(x2, mod, w, tiles_per_batch, tm, alpha):
    t, d = x2.shape
    per_token = tiles_per_batch is None
    if per_token:
        mod_spec = pl.BlockSpec((tm, 8 * d), lambda i: (i, 0))
    else:
        mod_spec = pl.BlockSpec((1, 8, d), lambda i: (i // tiles_per_batch, 0, 0))
    dff = w["w_dn"].shape[0]
    n_chunk = 2 if dff % 256 == 0 else 1
    return pl.pallas_call(
        functools.partial(_ffn_kernel, per_token, alpha, n_chunk),
        grid=(t // tm,),
        in_specs=[pl.BlockSpec((tm, d), lambda i: (i, 0)), mod_spec,
                  pl.BlockSpec(w["w_up"].shape, lambda i: (0, 0)),
                  pl.BlockSpec(w["w_dn"].shape, lambda i: (0, 0)),
                  pl.BlockSpec((1, d), lambda i: (0, 0)), pl.BlockSpec((1, d), lambda i: (0, 0))],
        out_specs=pl.BlockSpec((tm, d), lambda i: (i, 0)),
        out_shape=jax.ShapeDtypeStruct((t, d), F32),
        compiler_params=_cparams("parallel"),
        name="ffn",
    )(x2, mod, w["w_up"], w["w_dn"], w["ln2_g"], w["ln2_b"])


def _ret_kernel(chunk, q_ref, k_ref, v_ref, g_ref, r0_ref, intra_ref, qdec_ref, kdec_ref, cdec_ref,
                o_ref, rout_ref, r_s):
    j = pl.program_id(1)

    @pl.when(j == 0)
    def _():
        r_s[...] = r0_ref[0]

    q = q_ref[...]
    k = k_ref[...]
    v16 = v_ref[...].astype(BF16)
    k16 = k.astype(BF16)
    o = _dot(q * qdec_ref[...], r_s[...])
    for h in range(N_HEADS):
        hm = _head_mask(q.shape, h)
        s = _dot_nt(jnp.where(hm, q, 0.0), k16) * intra_ref[h]
        o = o + jnp.where(hm, _dot(s, v16), 0.0)
    upd = _dot_tn(k * kdec_ref[...], v16)
    rr = _row_iota(upd.shape) // HEAD_DIM
    cc = _lane_iota(upd.shape) // HEAD_DIM
    r_new = r_s[...] * cdec_ref[...] + jnp.where(rr == cc, upd, 0.0)
    r_s[...] = r_new

    @pl.when(j == pl.num_programs(1) - 1)
    def _():
        rout_ref[0] = r_new

    mu = jnp.zeros(o.shape, F32)
    for h in range(N_HEADS):
        hm = _head_mask(o.shape, h)
        mu = mu + jnp.where(hm, jnp.sum(jnp.where(hm, o, 0.0), axis=-1, keepdims=True), 0.0)
    oc = o - mu * (1.0 / HEAD_DIM)
    var = jnp.zeros(o.shape, F32)
    oc2 = oc * oc
    for h in range(N_HEADS):
        hm = _head_mask(o.shape, h)
        var = var + jnp.where(hm, jnp.sum(jnp.where(hm, oc2, 0.0), axis=-1, keepdims=True), 0.0)
    o_ref[...] = oc * lax.rsqrt(var * (1.0 / HEAD_DIM) + LN_EPS) * _silu(g_ref[...])


def _ret_tables(chunk):
    h = jnp.arange(N_HEADS, dtype=F32)
    log_g = jnp.log1p(-jnp.exp2(-5.0 - h))
    idx = jnp.arange(chunk, dtype=F32)
    diff = idx[:, None] - idx[None, :]
    intra = jnp.where(diff >= 0, jnp.exp(jnp.maximum(diff, 0.0)[None] * log_g[:, None, None]), 0.0)
    q_dec = jnp.exp((idx + 1.0)[None, :] * log_g[:, None])
    k_dec = jnp.exp((chunk - 1.0 - idx)[None, :] * log_g[:, None])
    c_dec = jnp.exp(chunk * log_g)
    lanes = lambda t: jnp.repeat(t.T, HEAD_DIM, axis=1)
    return intra, lanes(q_dec), lanes(k_dec), jnp.repeat(c_dec, HEAD_DIM)[None, :]


def _ret_call(ret, b, l, r0_bd):
    chunk = RET_CHUNK if l % RET_CHUNK == 0 else l
    n = l // chunk
    intra, q_dec, k_dec, c_dec = _ret_tables(chunk)
    col = lambda c: pl.BlockSpec((chunk, BRANCH_W), lambda i, j: (i * n + j, c))
    full = lambda a: pl.BlockSpec(a.shape, lambda i, j: (0,) * a.ndim)
    return pl.pallas_call(
        functools.partial(_ret_kernel, chunk),
        grid=(b, n),
        in_specs=[col(0), col(1), col(2), col(3),
                  pl.BlockSpec((1, BRANCH_W, BRANCH_W), lambda i, j: (i, 0, 0)),
                  full(intra), full(q_dec), full(k_dec), full(c_dec)],
        out_specs=[pl.BlockSpec((chunk, BRANCH_W), lambda i, j: (i * n + j, 0)),
                   pl.BlockSpec((1, BRANCH_W, BRANCH_W), lambda i, j: (i, 0, 0))],
        out_shape=[jax.ShapeDtypeStruct((b * l, BRANCH_W), F32),
                   jax.ShapeDtypeStruct((b, BRANCH_W, BRANCH_W), F32)],
        scratch_shapes=[pltpu.VMEM((BRANCH_W, BRANCH_W), F32)],
        compiler_params=_cparams("parallel", "arbitrary"),
        name="retention",
    )(ret, ret, ret, ret, r0_bd, intra, q_dec, k_dec, c_dec)


def _bd_from_heads(r):
    b = r.shape[0]
    eye = jnp.eye(N_HEADS, dtype=r.dtype)
    return jnp.einsum("bhij,hg->bhigj", r, eye).reshape(b, BRANCH_W, BRANCH_W)


def _heads_from_bd(r_bd):
    b = r_bd.shape[0]
    r5 = r_bd.reshape(b, N_HEADS, HEAD_DIM, N_HEADS, HEAD_DIM)
    return jnp.stack([r5[:, h, :, h, :] for h in range(N_HEADS)], axis=1)


def _cmul(ar, ai, br, bi):
    return ar * br - ai * bi, ar * bi + ai * br


def _gelu_tanh(x):
    return 0.5 * x * (1.0 + jnp.tanh(math.sqrt(2.0 / math.pi) * (x + 0.044715 * (x * x * x))))


def _s5_kernel(chain, u_ref, x0_ref, are_ref, aim_ref, ldt_ref, bre_ref, bim_ref, cre_ref, cim_ref, dsk_ref,
               wglu_ref, bglu_ref, o_ref, st_ref, xr_s, xi_s, cr_s, ci_s):
    rows = u_ref.shape[0]
    n_grp = rows // SUBLANES
    j = pl.program_id(1)

    ar, ai = are_ref[...], aim_ref[...]
    dt = jnp.exp(ldt_ref[...])
    mag = jnp.exp(ar * dt)
    abr, abi = mag * jnp.cos(ai * dt), mag * jnp.sin(ai * dt)
    nr, ni = abr - 1.0, abi
    den = ar * ar + ai * ai
    fr = (nr * ar + ni * ai) / den
    fi = (ni * ar - nr * ai) / den
    bbr = fr * bre_ref[...] - fi * bim_ref[...]
    bbi = fr * bim_ref[...] + fi * bre_ref[...]

    u = u_ref[...]
    u16 = u.astype(BF16)
    xr_s[...] = jnp.dot(u16, bbr.astype(BF16), preferred_element_type=F32)
    xi_s[...] = jnp.dot(u16, bbi.astype(BF16), preferred_element_type=F32)

    a2r, a2i = _cmul(abr, abi, abr, abi)
    a4r, a4i = _cmul(a2r, a2i, a2r, a2i)
    row = _row_iota((SUBLANES, abr.shape[-1]))
    pr, pi_ = jnp.broadcast_to(abr, row.shape), jnp.broadcast_to(abi, row.shape)
    qr, qi = abr, abi
    for i in range(1, SUBLANES):
        qr, qi = _cmul(qr, qi, abr, abi)
        pr = jnp.where(row == i, qr, pr)
        pi_ = jnp.where(row == i, qi, pi_)
    steps = ((1, abr, abi), (2, a2r, a2i), (4, a4r, a4i))

    if chain:
        @pl.when(j == 0)
        def _():
            cr_s[...] = x0_ref[0, 0:1, :]
            ci_s[...] = x0_ref[0, 1:2, :]

    def group(g, carry):
        r0 = pl.multiple_of(g * SUBLANES, SUBLANES)
        xr = xr_s[pl.ds(r0, SUBLANES), :]
        xi = xi_s[pl.ds(r0, SUBLANES), :]
        for d, er, ei in steps:
            sr = jnp.where(row >= d, pltpu.roll(xr, d, 0), 0.0)
            si = jnp.where(row >= d, pltpu.roll(xi, d, 0), 0.0)
            tr, ti = _cmul(er, ei, sr, si)
            xr, xi = xr + tr, xi + ti
        if chain:
            c_r, c_i = cr_s[...], ci_s[...]
        else:
            c_r, c_i = x0_ref[g, 0:1, :], x0_ref[g, 1:2, :]
        tr, ti = _cmul(pr, pi_, c_r, c_i)
        xr, xi = xr + tr, xi + ti
        xr_s[pl.ds(r0, SUBLANES), :] = xr
        xi_s[pl.ds(r0, SUBLANES), :] = xi
        if chain:
            cr_s[...] = xr[SUBLANES - 1:SUBLANES, :]
            ci_s[...] = xi[SUBLANES - 1:SUBLANES, :]
        else:
            st_ref[g, 0:1, :] = xr[SUBLANES - 1:SUBLANES, :]
            st_ref[g, 1:2, :] = xi[SUBLANES - 1:SUBLANES, :]
        return carry

    lax.fori_loop(0, n_grp, group, 0)

    if chain:
        @pl.when(j == pl.num_programs(1) - 1)
        def _():
            st_ref[0, 0:1, :] = cr_s[...]
            st_ref[0, 1:2, :] = ci_s[...]

    y = (jnp.dot(xr_s[...].astype(BF16), cre_ref[...], preferred_element_type=F32)
         - jnp.dot(xi_s[...].astype(BF16), cim_ref[...], preferred_element_type=F32) + dsk_ref[...] * u)
    zb = _gelu_tanh(y)
    o_ref[...] = zb * _sigmoid(jnp.dot(zb.astype(BF16), wglu_ref[...], preferred_element_type=F32) + bglu_ref[...])


def _s5_call(u2, b, l, x0, w):
    n_state = x0.shape[-1]
    chain = l % 128 == 0
    if chain:
        tl = min(l, 512)
        grid = (b, l // tl)
        rows = tl
        u_spec = pl.BlockSpec((tl, BRANCH_W), lambda i, j: (i * (l // tl) + j, 0))
        x0_spec = pl.BlockSpec((1, 2, n_state), lambda i, j: (i, 0, 0))
    else:
        assert l == SUBLANES
        grid = (1, 1)
        rows = b * l
        u_spec = pl.BlockSpec((rows, BRANCH_W), lambda i, j: (0, 0))
        x0_spec = pl.BlockSpec((b, 2, n_state), lambda i, j: (0, 0, 0))
    full = lambda a: pl.BlockSpec(a.shape, lambda i, j: (0,) * a.ndim)
    params = [w["a_re"], w["a_im"], w["log_dt"], w["b_re_bd"], w["b_im_bd"], w["c_re_bd"], w["c_im_bd"],
              w["d_skip"], w["w_glu"], w["b_glu"]]
    return pl.pallas_call(
        functools.partial(_s5_kernel, chain),
        grid=grid,
        in_specs=[u_spec, x0_spec] + [full(a) for a in params],
        out_specs=[u_spec, x0_spec],
        out_shape=[jax.ShapeDtypeStruct((b * l, BRANCH_W), F32), jax.ShapeDtypeStruct(x0.shape, F32)],
        scratch_shapes=[pltpu.VMEM((rows, n_state), F32), pltpu.VMEM((rows, n_state), F32),
                        pltpu.VMEM((1, n_state), F32), pltpu.VMEM((1, n_state), F32)],
        compiler_params=_cparams("parallel", "arbitrary"),
        name="s5",
    )(u2, x0, *params)


def _s5_params(a_re, a_im, b_re, b_im, c_re, c_im, d_skip, log_dt, w_glu, b_glu):
    g, p = a_re.shape
    cg = b_re.shape[-1]
    eye = jnp.eye(g, dtype=F32)
    b_bd = lambda t: jnp.einsum("gpc,gh->gchp", t, eye).reshape(g * cg, g * p)
    c_bd = lambda t: jnp.einsum("gcp,gh->gphc", t, eye).reshape(g * p, g * cg)
    return {
        "a_re": a_re.reshape(1, g * p), "a_im": a_im.reshape(1, g * p),
        "log_dt": jnp.repeat(log_dt, p).reshape(1, g * p),
        "b_re_bd": b_bd(b_re), "b_im_bd": b_bd(b_im),
        "c_re_bd": c_bd(c_re).astype(BF16), "c_im_bd": c_bd(c_im).astype(BF16),
        "d_skip": d_skip.reshape(1, g * cg), "w_glu": w_glu.astype(BF16), "b_glu": b_glu.reshape(1, -1),
    }


TK = 128


def _stack_heads_bd(q):
    qf = q.astype(F32)
    return jnp.concatenate([jnp.where(_head_mask(qf.shape, h), qf, 0.0) for h in range(N_HEADS)], axis=0).astype(BF16)


def _unstack_heads_bd(acc, tq):
    out = jnp.zeros((tq, acc.shape[-1]), F32)
    for h in range(N_HEADS):
        blk = acc[h * tq:(h + 1) * tq]
        out = out + jnp.where(_head_mask(blk.shape, h), blk, 0.0)
    return out


def _suffix_matrix():
    r = _row_iota((2 * TK, TK)) % TK
    c = _lane_iota((2 * TK, TK))
    return jnp.where(r > c, 1.0, 0.0).astype(BF16)


def _sb_tile(qbd, k16, v16, mask, u2, carry, acc):
    z = _dot_nt(qbd, k16)
    l1m = -(jnp.maximum(z, 0.0) + jnp.log(1.0 + jnp.exp(-jnp.abs(z))))
    if mask is not None:
        l1m = jnp.where(mask, l1m, 0.0)
    hi = l1m.astype(BF16)
    lo = (l1m - hi.astype(F32)).astype(BF16)
    suffix = jnp.dot(jnp.concatenate([hi, lo], axis=1), u2, preferred_element_type=F32)
    w = jnp.exp(z + l1m + suffix + carry)
    if mask is not None:
        w = jnp.where(mask, w, 0.0)
    acc = acc + jnp.dot(w.astype(BF16), v16, preferred_element_type=F32)
    carry = carry + jnp.sum(l1m, axis=-1, keepdims=True)
    return carry, acc


def _sb_prompt_kernel(q_ref, kv_ref, o_ref):
    j = pl.program_id(1)
    tq = q_ref.shape[0]
    qbd = _stack_heads_bd(q_ref[...])
    u2 = _suffix_matrix()
    rows = N_HEADS * tq
    qi = _row_iota((rows, TK)) % tq
    diag_mask = _lane_iota((rows, TK)) < qi

    def tile(kt, mask, carry, acc):
        r0 = pl.multiple_of(kt * TK, TK)
        return _sb_tile(qbd, kv_ref[pl.ds(r0, TK), 0:BRANCH_W], kv_ref[pl.ds(r0, TK), BRANCH_W:2 * BRANCH_W],
                        mask, u2, carry, acc)

    carry, acc = tile(j, diag_mask, jnp.zeros((rows, 1), F32), jnp.zeros((rows, BRANCH_W), F32))
    carry, acc = lax.fori_loop(0, j, lambda it, c: tile(j - 1 - it, None, c[0], c[1]), (carry, acc))
    o_ref[...] = _unstack_heads_bd(acc, tq)


def _sb_prompt_call(sbq, sbkv16, b, l):
    tq = TK
    n = l // tq
    return pl.pallas_call(
        _sb_prompt_kernel,
        grid=(b, n),
        in_specs=[pl.BlockSpec((tq, BRANCH_W), lambda i, j: (i * n + j, 0)),
                  pl.BlockSpec((l, 2 * BRANCH_W), lambda i, j: (i, 0))],
        out_specs=pl.BlockSpec((tq, BRANCH_W), lambda i, j: (i * n + j, 0)),
        out_shape=jax.ShapeDtypeStruct((b * l, BRANCH_W), F32),
        compiler_params=_cparams("parallel", "arbitrary"),
        name="sb_prompt",
    )(sbq, sbkv16)


PAGES_PER_STEP = 8


def _sb_decode_kernel(n_pg, pt_ref, q_ref, kvn_ref, *rest):
    page_refs = rest[:n_pg]
    o_ref, carry_s, acc_s = rest[n_pg:]
    s = pl.program_id(1)
    tq = q_ref.shape[1]
    rows = N_HEADS * tq
    qbd = _stack_heads_bd(q_ref[0])
    u2 = _suffix_matrix()

    @pl.when(s == 0)
    def _():
        mask = _lane_iota((rows, TK)) < (_row_iota((rows, TK)) % tq)
        kvn = kvn_ref[0]
        carry, acc = _sb_tile(qbd, kvn[:, 0:BRANCH_W], kvn[:, BRANCH_W:], mask, u2,
                              jnp.zeros((rows, 1), F32), jnp.zeros((rows, BRANCH_W), F32))
        carry_s[...] = carry
        acc_s[...] = acc

    carry, acc = carry_s[...], acc_s[...]
    for pg in page_refs:
        kv = pg[0, 0].astype(BF16)
        carry, acc = _sb_tile(qbd, kv[:, 0:BRANCH_W], kv[:, BRANCH_W:], None, u2, carry, acc)
    carry_s[...] = carry
    acc_s[...] = acc

    @pl.when(s == pl.num_programs(1) - 1)
    def _():
        o_ref[0] = _unstack_heads_bd(acc, tq)


def _sb_decode_call(sbq3, kvn_pad16, cache4, layer, page_table):
    b, tq, _ = sbq3.shape
    n_pages = page_table.shape[1]
    n_pg = math.gcd(PAGES_PER_STEP, n_pages)
    n_steps = n_pages // n_pg

    def page_spec(k):
        return pl.BlockSpec((1, 1, TK, 2 * BRANCH_W),
                            lambda i, s, pt: (layer, pt[i, n_pages - 1 - (s * n_pg + k)], 0, 0))

    grid_spec = pltpu.PrefetchScalarGridSpec(
        num_scalar_prefetch=1,
        grid=(b, n_steps),
        in_specs=[pl.BlockSpec((1, tq, BRANCH_W), lambda i, s, pt: (i, 0, 0)),
                  pl.BlockSpec((1, TK, 2 * BRANCH_W), lambda i, s, pt: (i, 0, 0))]
                 + [page_spec(k) for k in range(n_pg)],
        out_specs=pl.BlockSpec((1, tq, BRANCH_W), lambda i, s, pt: (i, 0, 0)),
        scratch_shapes=[pltpu.VMEM((N_HEADS * tq, 1), F32), pltpu.VMEM((N_HEADS * tq, BRANCH_W), F32)],
    )
    return pl.pallas_call(
        functools.partial(_sb_decode_kernel, n_pg),
        grid_spec=grid_spec,
        out_shape=jax.ShapeDtypeStruct((b, tq, BRANCH_W), F32),
        compiler_params=_cparams("parallel", "arbitrary"),
        name="sb_decode",
    )(page_table, sbq3, kvn_pad16, *([cache4] * n_pg))


def _round_up(x, m):
    return -(-x // m) * m


def _stack_heads_128(q):
    qf = q.astype(F32)
    tq = qf.shape[0]
    low = _lane_iota((tq, LANES)) < HEAD_DIM
    parts = []
    for h in range(N_HEADS):
        half = qf[:, LANES * (h // 2):LANES * (h // 2 + 1)]
        if h % 2 == 1:
            half = pltpu.roll(half, HEAD_DIM, 1)
        parts.append(jnp.where(low, half, 0.0))
    return jnp.concatenate(parts, axis=0)


def _compress(rows_ref, wc_ref, n):
    p0 = jnp.zeros((n, LANES), F32)
    p1 = jnp.zeros((n, LANES), F32)
    for s in range(CMP_STRIDE):
        x = rows_ref[pl.ds(s, n, stride=CMP_STRIDE), :].astype(BF16)
        p0 = p0 + jnp.dot(x, wc_ref[s], preferred_element_type=F32)
        p1 = p1 + jnp.dot(x, wc_ref[CMP_STRIDE + s], preferred_element_type=F32)
    return p0 + pltpu.roll(p1, n - 1, 0)


def _cmp_select(qp16, kvc, qpos_rows, qpos_q, tq, nsp, n_top):
    n = kvc.shape[0]
    kvc16 = kvc.astype(BF16)
    s = _dot_nt(qp16, kvc16)
    cmask = (_lane_iota(s.shape) * CMP_STRIDE + (CMP_BLOCK - 1)) <= qpos_rows
    s = jnp.where(cmask, s, NEG_BIG)
    m = jnp.max(s, axis=-1, keepdims=True)
    m = jnp.where(m > 0.5 * NEG_BIG, m, 0.0)
    e = jnp.where(cmask, jnp.exp(s - m), 0.0)
    p16 = (e / jnp.maximum(jnp.sum(e, axis=-1, keepdims=True), 1e-30)).astype(BF16)
    o_cmp = jnp.dot(p16, kvc16, preferred_element_type=F32)
    ci = _row_iota((n, nsp)) * CMP_STRIDE
    sj = _lane_iota((n, nsp)) * SLC_BLOCK
    c2s = jnp.where((ci < sj + SLC_BLOCK) & (sj < ci + CMP_BLOCK), 1.0, 0.0).astype(BF16)
    imp4 = jnp.dot(p16, c2s, preferred_element_type=F32)
    imp = imp4[0:tq]
    for h in range(1, N_HEADS):
        imp = imp + imp4[h * tq:(h + 1) * tq]

    blk = _lane_iota((tq, nsp))
    blkf = blk.astype(F32)
    cur = qpos_q // SLC_BLOCK
    forced = (blk == 0) | (blk == cur) | (blk == cur - 1)
    val = jnp.where(forced, jnp.inf, imp)
    avail = jnp.where(blk * SLC_BLOCK <= qpos_q, 1.0, 0.0)
    sel = jnp.zeros((tq, nsp), F32)
    for _ in range(n_top):
        vm = jnp.where(avail > 0.5, val, -jnp.inf)
        mx = jnp.max(vm, axis=-1, keepdims=True)
        cand = (avail > 0.5) & (vm == mx)
        first = jnp.min(jnp.where(cand, blkf, float(nsp)), axis=-1, keepdims=True)
        pick = blkf == first
        sel = jnp.where(pick, 1.0, sel)
        avail = jnp.where(pick, 0.0, avail)
    return o_cmp, sel.astype(BF16)


def _attn_tile(q16, kv16, mask, m, l, acc):
    s = jnp.where(mask, _dot_nt(q16, kv16), NEG_BIG)
    m_new = jnp.maximum(m, jnp.max(s, axis=-1, keepdims=True))
    p = jnp.where(mask, jnp.exp(s - m_new), 0.0)
    alpha = jnp.exp(m - m_new)
    l = alpha * l + jnp.sum(p, axis=-1, keepdims=True)
    acc = alpha * acc + jnp.dot(p.astype(BF16), kv16, preferred_element_type=F32)
    return m_new, l, acc


def _attn_init(rows, width):
    return jnp.full((rows, 1), NEG_BIG, F32), jnp.zeros((rows, 1), F32), jnp.zeros((rows, width), F32)


def _slc_tile(kt, carry, qr16, rows_ref, selb4, qpos_rows, nsp):
    r0 = pl.multiple_of(kt * TK, TK)
    kv16 = rows_ref[pl.ds(r0, TK), :].astype(BF16)
    er = _row_iota((nsp, TK))
    ec = _lane_iota((nsp, TK))
    expand = jnp.where(er == (TK // SLC_BLOCK) * kt + ec // SLC_BLOCK, 1.0, 0.0).astype(BF16)
    seltok = jnp.dot(selb4, expand, preferred_element_type=F32)
    tok = kt * TK + _lane_iota(seltok.shape)
    mask = (seltok > 0.5) & (tok <= qpos_rows)
    return _attn_tile(qr16, kv16, mask, *carry)


def _win_tile(kv16, kpos, carry, qw16, qpos_rows):
    dist = qpos_rows - kpos
    mask = (dist >= 0) & (dist < WINDOW)
    return _attn_tile(qw16, kv16, mask, *carry)


def _nsa_combine(g, o_c, o_s, o_w, tq):
    comb = []
    for h in range(N_HEADS):
        sl = slice(h * tq, (h + 1) * tq)
        comb.append(g[:, h:h + 1] * o_c[sl] + g[:, N_HEADS + h:N_HEADS + h + 1] * o_s[sl]
                    + g[:, 2 * N_HEADS + h:2 * N_HEADS + h + 1] * o_w[sl])
    low = _lane_iota((tq, LANES)) < HEAD_DIM
    out01 = jnp.where(low, pltpu.roll(comb[0], HEAD_DIM, 1), comb[1])
    out23 = jnp.where(low, pltpu.roll(comb[2], HEAD_DIM, 1), comb[3])
    return jnp.concatenate([out01, out23], axis=1)


def _nsa_prompt_kernel(n_top, nq_ref, nqr_ref, ng_ref, rowsc_ref, rows_ref, win_ref, wc_ref, o_ref, kvc_s):
    j = pl.program_id(1)
    tq = nq_ref.shape[0]
    l_rows = rows_ref.shape[0]
    nsp = _round_up(l_rows // SLC_BLOCK, LANES)

    @pl.when(j == 0)
    def _():
        kvc_s[...] = _compress(rowsc_ref, wc_ref, kvc_s.shape[0])

    rows4 = N_HEADS * tq
    qpos_rows = j * tq + lax.broadcasted_iota(jnp.int32, (rows4, 1), 0) % tq
    qpos_q = j * tq + lax.broadcasted_iota(jnp.int32, (tq, 1), 0)
    qp16 = _stack_heads_128(nq_ref[...]).astype(BF16)
    qw = _stack_heads_128(nqr_ref[...])
    qw16 = qw.astype(BF16)

    o_c, selb = _cmp_select(qp16, kvc_s[...], qpos_rows, qpos_q, tq, nsp, n_top)
    selb4 = jnp.concatenate([selb.astype(F32)] * N_HEADS, axis=0).astype(BF16)

    _, l_s, acc_s = lax.fori_loop(
        0, j + 1, lambda kt, c: _slc_tile(kt, c, qw16, rows_ref, selb4, qpos_rows, nsp),
        _attn_init(rows4, LANES))
    o_s = acc_s / l_s

    def win_body(kt, c):
        r0 = pl.multiple_of(kt * TK, TK)
        kpos = kt * TK + _lane_iota((rows4, TK))
        return _win_tile(win_ref[pl.ds(r0, TK), :].astype(BF16), kpos, c, qw16, qpos_rows)

    _, l_w, acc_w = lax.fori_loop(jnp.maximum(j - WINDOW // TK, 0), j + 1, win_body, _attn_init(rows4, LANES))
    o_ref[...] = _nsa_combine(ng_ref[...], o_c, o_s, acc_w / l_w, tq)


def _nsa_prompt_call(nq, nqr, ng, rowsc, rowss16, win16, wc, b, l):
    tq = TK
    n = l // tq
    n_top = min(N_SEL, l // SLC_BLOCK)
    qspec = lambda w: pl.BlockSpec((tq, w), lambda i, j: (i * n + j, 0))
    return pl.pallas_call(
        functools.partial(_nsa_prompt_kernel, n_top),
        grid=(b, n),
        in_specs=[qspec(BRANCH_W), qspec(BRANCH_W), qspec(LANES),
                  pl.BlockSpec((l, LANES), lambda i, j: (i, 0)),
                  pl.BlockSpec((l, LANES), lambda i, j: (i, 0)),
                  pl.BlockSpec((l, LANES), lambda i, j: (i, 0)),
                  pl.BlockSpec(wc.shape, lambda i, j: (0, 0, 0))],
        out_specs=qspec(BRANCH_W),
        out_shape=jax.ShapeDtypeStruct((b * l, BRANCH_W), F32),
        scratch_shapes=[pltpu.VMEM((l // CMP_STRIDE, LANES), F32)],
        compiler_params=_cparams("parallel", "arbitrary"),
        name="nsa_prompt",
    )(nq, nqr, ng, rowsc, rowss16, win16, wc)


def _nsa_decode_kernel(n_pg, past_len, n_cmp, n_top, pt_ref, nq_ref, nqr_ref, ng_ref, rown_ref, cwin_ref, winn_ref,
                       wc_ref, *rest):
    page_refs = rest[:n_pg]
    o_ref, wst_ref, rowsc_s, rows_s, win_s = rest[n_pg:]
    s = pl.program_id(1)
    tq = nq_ref.shape[1]
    for k, pg in enumerate(page_refs):
        r0 = pl.multiple_of((s * n_pg + k) * TK, TK)
        rowsc_s[pl.ds(r0, TK), :] = pg[0, 0, :, 0:LANES]
        rows_s[pl.ds(r0, TK), :] = pg[0, 0, :, LANES:2 * LANES]

    @pl.when(s == pl.num_programs(1) - 1)
    def _():
        t_rows = rows_s.shape[0]
        w_rows = win_s.shape[0]
        wb = cwin_ref.shape[2]
        nsp = _round_up(_round_up(past_len + tq, SLC_BLOCK) // SLC_BLOCK, LANES)
        rowsc_s[past_len:past_len + tq, :] = rown_ref[0, :, 0:LANES]
        rows_s[past_len:past_len + tq, :] = rown_ref[0, :, LANES:2 * LANES]
        rowsc_s[past_len + tq:t_rows, :] = jnp.zeros((t_rows - past_len - tq, LANES), F32)
        rows_s[past_len + tq:t_rows, :] = jnp.zeros((t_rows - past_len - tq, LANES), F32)
        win_s[0:wb, :] = cwin_ref[0, 0]
        win_s[wb:wb + tq, :] = winn_ref[0]
        win_s[wb + tq:w_rows, :] = jnp.zeros((w_rows - wb - tq, LANES), F32)
        wst_ref[0] = win_s[tq:wb + tq, :]

        rows4 = N_HEADS * tq
        qpos_rows = past_len + lax.broadcasted_iota(jnp.int32, (rows4, 1), 0) % tq
        qpos_q = past_len + lax.broadcasted_iota(jnp.int32, (tq, 1), 0)
        qp16 = _stack_heads_128(nq_ref[0]).astype(BF16)
        qw = _stack_heads_128(nqr_ref[0])
        qw16 = qw.astype(BF16)

        kvc = _compress(rowsc_s, wc_ref, n_cmp)
        o_c, selb = _cmp_select(qp16, kvc, qpos_rows, qpos_q, tq, nsp, n_top)
        selb4 = jnp.concatenate([selb.astype(F32)] * N_HEADS, axis=0).astype(BF16)

        _, l_s, acc_s = lax.fori_loop(
            0, t_rows // TK, lambda kt, c: _slc_tile(kt, c, qw16, rows_s, selb4, qpos_rows, nsp),
            _attn_init(rows4, LANES))
        o_s = acc_s / l_s

        carry = _attn_init(rows4, LANES)
        for kt in range(w_rows // TK):
            kpos = past_len - wb + kt * TK + _lane_iota((rows4, TK))
            carry = _win_tile(win_s[kt * TK:(kt + 1) * TK, :].astype(BF16), kpos, carry, qw16, qpos_rows)
        _, l_w, acc_w = carry
        o_ref[0] = _nsa_combine(ng_ref[0], o_c, o_s, acc_w / l_w, tq)


def _nsa_decode_call(nq3, nqr3, ng3, rown3, winn3, cache_nsa4, cache_win4, layer, page_table, wc):
    b, tq, _ = nq3.shape
    n_pages = page_table.shape[1]
    past_len = n_pages * TK
    wb = cache_win4.shape[2]
    n_pg = math.gcd(PAGES_PER_STEP, n_pages)
    n_steps = n_pages // n_pg
    t_pad = _round_up(past_len + tq, SLC_BLOCK)
    n_cmp = _round_up(t_pad // CMP_STRIDE, SUBLANES)
    t_rows = _round_up(n_cmp * CMP_STRIDE, TK)
    w_rows = _round_up(wb + tq, TK)
    n_top = min(N_SEL, t_pad // SLC_BLOCK)

    def page_spec(k):
        return pl.BlockSpec((1, 1, TK, BRANCH_W), lambda i, s, pt: (layer, pt[i, s * n_pg + k], 0, 0))

    row3 = lambda w: pl.BlockSpec((1, tq, w), lambda i, s, pt: (i, 0, 0))
    grid_spec = pltpu.PrefetchScalarGridSpec(
        num_scalar_prefetch=1,
        grid=(b, n_steps),
        in_specs=[row3(BRANCH_W), row3(BRANCH_W), row3(LANES), row3(BRANCH_W),
                  pl.BlockSpec((1, 1, wb, LANES), lambda i, s, pt: (layer, i, 0, 0)),
                  row3(LANES),
                  pl.BlockSpec(wc.shape, lambda i, s, pt: (0, 0, 0))]
                 + [page_spec(k) for k in range(n_pg)],
        out_specs=[row3(BRANCH_W), pl.BlockSpec((1, wb, LANES), lambda i, s, pt: (i, 0, 0))],
        scratch_shapes=[pltpu.VMEM((t_rows, LANES), F32), pltpu.VMEM((t_rows, LANES), F32),
                        pltpu.VMEM((w_rows, LANES), F32)],
    )
    return pl.pallas_call(
        functools.partial(_nsa_decode_kernel, n_pg, past_len, n_cmp, n_top),
        grid_spec=grid_spec,
        out_shape=[jax.ShapeDtypeStruct((b, tq, BRANCH_W), F32), jax.ShapeDtypeStruct((b, wb, LANES), F32)],
        compiler_params=_cparams("parallel", "arbitrary"),
        name="nsa_decode",
    )(page_table, nq3, nqr3, ng3, rown3, cache_win4, winn3, wc, *([cache_nsa4] * n_pg))


def _cmp_weights(w_cmp_k, w_cmp_v):
    z = jnp.zeros_like(w_cmp_k)
    top = jnp.concatenate([w_cmp_k, z], axis=2)
    bot = jnp.concatenate([z, w_cmp_v], axis=2)
    return jnp.concatenate([top, bot], axis=1).astype(BF16)


def _rope_tables(pos):
    posf = pos.astype(F32)[:, None]
    jj = np.arange(LANES) % HEAD_DIM

    def table(half, theta, lanes_on):
        inv = theta ** (-jnp.arange(half, dtype=F32) / half)
        ang = posf * inv[None, :]
        cos_l = jnp.cos(ang)[:, jj % half]
        sin_l = jnp.sin(ang)[:, jj % half] * jnp.where(jj < half, -1.0, 1.0).astype(F32)[None, :]
        on = jnp.asarray((jj < 2 * half) & lanes_on)[None, :]
        return jnp.where(on, cos_l, 1.0), jnp.where(on, sin_l, 0.0)

    every = np.ones(LANES, bool)
    first = np.arange(LANES) < HEAD_DIM
    parts = (*table(HEAD_DIM // 2, RET_THETA, every), *table(ROPE_DIM // 2, ROPE_THETA, every),
             *table(ROPE_DIM // 2, ROPE_THETA, first))
    return jnp.concatenate(parts, axis=1)


def _layer_weights(i, p):
    w_in = p["w_in"][i]
    d = w_in.shape[0]
    c_mg = w_in.shape[1] - N_BRANCH * d
    c_nsa = 8 * BRANCH_W
    w_nsa = w_in[:, c_nsa:c_mg]
    w_nsa = jnp.pad(w_nsa, ((0, 0), (0, 3 * BRANCH_W - w_nsa.shape[1])))
    w = {
        "w_ret": w_in[:, 0:4 * BRANCH_W].astype(BF16), "w_u": w_in[:, 4 * BRANCH_W:5 * BRANCH_W].astype(BF16),
        "w_sb": w_in[:, 5 * BRANCH_W:8 * BRANCH_W].astype(BF16), "w_nsa": w_nsa.astype(BF16),
        "w_mg": w_in[:, c_mg:].astype(BF16), "w_branch": p["w_branch"][i].astype(BF16),
        "w_out": p["w_out"][i].astype(BF16), "ln1_g": p["ln1_g"][i][None, :], "ln1_b": p["ln1_b"][i][None, :],
        "w_up": p["w_ffn_up"][i].astype(BF16), "w_dn": p["w_ffn_down"][i].astype(BF16),
        "ln2_g": p["ln2_g"][i][None, :], "ln2_b": p["ln2_b"][i][None, :],
        "wc": _cmp_weights(p["w_cmp_k"][i], p["w_cmp_v"][i]),
    }
    s5 = _s5_params(p["ssm_a_re"][i], p["ssm_a_im"][i], p["ssm_b_re"][i], p["ssm_b_im"][i], p["ssm_c_re"][i],
                    p["ssm_c_im"][i], p["ssm_d"][i], p["ssm_log_dt"][i], p["w_glu"][i], p["b_glu"][i])
    return w, s5


def _ssm_rows(s):
    b = s.shape[0]
    return jnp.moveaxis(s, -1, 1).reshape(b, 2, -1)


def _ssm_state(rows, g):
    b = rows.shape[0]
    return jnp.moveaxis(rows.reshape(b, 2, g, -1), 1, -1)


def _layer(x, mod, tabs, w, s5, alpha, past):
    b, l, d = x.shape
    t = b * l
    x2 = x.reshape(t, d)
    g = s5["a_re"].shape[-1] // SSM_STATE
    if past is None:
        tm = min(l, 512)
        tpb = l // tm
        mod_in, tab_in = mod, tabs
    else:
        tm, tpb = t, None
        mod_in = jnp.repeat(mod.reshape(b, 8 * d), l, axis=0)
        tab_in = jnp.tile(tabs, (b, 1))
    ret, u, sbq, sbkv, sbkv16, nq, nqr, rows, win, ng, rowsc, rowss16, win16 = _in_call(
        x2, mod_in, tab_in, w, tpb, tm)

    if past is None:
        r0 = jnp.zeros((b, BRANCH_W, BRANCH_W), F32)
        x0 = jnp.zeros((b, 2, g * SSM_STATE), F32)
    else:
        r0 = _bd_from_heads(past["ret"])
        x0 = _ssm_rows(past["ssm"])
    out_a, r_bd = _ret_call(ret, b, l, r0)
    out_b, s_rows = _s5_call(u, b, l, x0, s5)

    if past is None:
        out_c = _sb_prompt_call(sbq, sbkv16, b, l)
        out_d = _nsa_prompt_call(nq, nqr, ng, rowsc, rowss16, win16, w["wc"], b, l)
        win_state = win.reshape(b, l, 2, HEAD_DIM)[:, l - min(WINDOW, l):]
    else:
        kvn = jnp.pad(sbkv16.reshape(b, l, 2 * BRANCH_W), ((0, 0), (0, TK - l), (0, 0)))
        out_c = _sb_decode_call(sbq.reshape(b, l, BRANCH_W), kvn, past["sb"], past["layer"], past["page_table"])
        out_d, wst = _nsa_decode_call(nq.reshape(b, l, BRANCH_W), nqr.reshape(b, l, BRANCH_W),
                                      ng.reshape(b, l, LANES), rows.reshape(b, l, BRANCH_W),
                                      win.reshape(b, l, LANES), past["nsa"], past["win"], past["layer"],
                                      past["page_table"], w["wc"])
        out_c = out_c.reshape(t, BRANCH_W)
        out_d = out_d.reshape(t, BRANCH_W)
        win_state = wst.reshape(b, -1, 2, HEAD_DIM)

    tm2 = min(tm, 256)
    tpb2 = None if tpb is None else l // tm2
    x1 = _merge_call(x2, mod_in, [out_a, out_b, out_c, out_d], w, tpb2, tm2, alpha)
    xo = _ffn_call(x1, mod_in, w, tpb2, tm2, alpha)
    state = (sbkv.reshape(b, l, 2, N_HEADS, HEAD_DIM), rows.reshape(b, l, 4, HEAD_DIM), win_state,
             _heads_from_bd(r_bd), _ssm_state(s_rows, g))
    return xo.reshape(b, l, d), state


def kernel(x_prompt, x_sample, c_prompt, c_sample, cache_sb, cache_nsa, cache_win, state_ret, state_ssm, page_table, w_ada, b_ada, w_in, ssm_a_re, ssm_a_im, ssm_b_re, ssm_b_im, ssm_c_re, ssm_c_im, ssm_d, ssm_log_dt, w_glu, b_glu, w_cmp_k, w_cmp_v, w_branch, w_out, ln1_g, ln1_b, w_ffn_up, w_ffn_down, ln2_g, ln2_b):
    p = dict(w_in=w_in, ssm_a_re=ssm_a_re, ssm_a_im=ssm_a_im, ssm_b_re=ssm_b_re, ssm_b_im=ssm_b_im,
             ssm_c_re=ssm_c_re, ssm_c_im=ssm_c_im, ssm_d=ssm_d, ssm_log_dt=ssm_log_dt, w_glu=w_glu, b_glu=b_glu,
             w_cmp_k=w_cmp_k, w_cmp_v=w_cmp_v, w_branch=w_branch, w_out=w_out, ln1_g=ln1_g, ln1_b=ln1_b,
             w_ffn_up=w_ffn_up, w_ffn_down=w_ffn_down, ln2_g=ln2_g, ln2_b=ln2_b)
    depth, d = w_ada.shape[0], w_ada.shape[1]
    bp, lp, _ = x_prompt.shape
    bs, ls, _ = x_sample.shape
    n_pool, page = cache_sb.shape[1], cache_sb.shape[2]
    past_len = page_table.shape[1] * page
    alpha = (2.0 * depth) ** 0.25

    c_all = jnp.concatenate([c_prompt, c_sample], axis=0)
    m_rows = _round_up(bp + bs, SUBLANES)
    c_all = jnp.pad(c_all, ((0, m_rows - bp - bs), (0, 0)))
    mods = _ada_call(c_all, w_ada, b_ada)

    tabs_p = _rope_tables(jnp.arange(lp, dtype=jnp.int32))
    tabs_s = _rope_tables(past_len + jnp.arange(ls, dtype=jnp.int32))
    cache_sb4 = cache_sb.reshape(depth, n_pool, page, 2 * BRANCH_W)
    cache_nsa4 = cache_nsa.reshape(depth, n_pool, page, BRANCH_W)
    cache_win4 = cache_win.reshape(depth, bs, cache_win.shape[2], LANES)

    xp, xs = x_prompt, x_sample
    st_p = [[] for _ in range(5)]
    st_s = [[] for _ in range(5)]
    for i in range(depth):
        w, s5 = _layer_weights(i, p)
        mod_i = jnp.pad(mods[i].reshape(m_rows, 6, d), ((0, 0), (0, 2), (0, 0)))
        xp, new_p = _layer(xp, mod_i[:bp], tabs_p, w, s5, alpha, None)
        past = dict(sb=cache_sb4, nsa=cache_nsa4, win=cache_win4, ret=state_ret[i], ssm=state_ssm[i],
                    layer=i, page_table=page_table)
        xs, new_s = _layer(xs, mod_i[bp:bp + bs], tabs_s, w, s5, alpha, past)
        for k in range(5):
            st_p[k].append(new_p[k])
            st_s[k].append(new_s[k])
    sb_p, nsa_p, win_p, ret_p, ssm_p = [jnp.stack(s, axis=0) for s in st_p]
    sb_s, nsa_s, win_s, ret_s, ssm_s = [jnp.stack(s, axis=0) for s in st_s]
    return (xp, xs, sb_p, sb_s, nsa_p, nsa_s, win_p, win_s, ret_p, ret_s, ssm_p, ssm_s)
```

```python
import functools
import math

import jax
import jax.numpy as jnp
import numpy as np
from jax import lax
from jax.experimental import pallas as pl
from jax.experimental.pallas import tpu as pltpu

F32 = jnp.float32
BF16 = jnp.bfloat16

LANES = 128
SUBLANES = 8
VMEM_LIMIT_BYTES = 56 * 1024 * 1024

HEAD_DIM = 64
N_HEADS = 4
BRANCH_W = N_HEADS * HEAD_DIM
N_BRANCH = 4
RET_THETA = 10000.0
RET_CHUNK = 128
SSM_GROUP_CH = 16
SSM_STATE = 64
CMP_BLOCK = 32
CMP_STRIDE = 16
SLC_BLOCK = 64
N_SEL = 16
WINDOW = 512
ROPE_THETA = 500000.0
ROPE_DIM = HEAD_DIM // 4
LN_EPS = 1e-5
QK_SCALE = HEAD_DIM ** -0.5
NEG_BIG = -1e30


def _cparams(*sem):
    return pltpu.CompilerParams(dimension_semantics=tuple(sem), vmem_limit_bytes=VMEM_LIMIT_BYTES)


def _lane_iota(shape):
    return lax.broadcasted_iota(jnp.int32, shape, len(shape) - 1)


def _row_iota(shape):
    return lax.broadcasted_iota(jnp.int32, shape, len(shape) - 2)


def _dot(a, b):
    return jnp.dot(a.astype(BF16), b.astype(BF16), preferred_element_type=F32)


def _dot_nt(a, b):
    return lax.dot_general(a.astype(BF16), b.astype(BF16), (((1,), (1,)), ((), ())), preferred_element_type=F32)


def _dot_tn(a, b):
    return lax.dot_general(a.astype(BF16), b.astype(BF16), (((0,), (0,)), ((), ())), preferred_element_type=F32)


def _ln_rows(x):
    mu = jnp.mean(x, axis=-1, keepdims=True)
    xc = x - mu
    var = jnp.mean(xc * xc, axis=-1, keepdims=True)
    return xc * lax.rsqrt(var + LN_EPS)


def _sigmoid(x):
    return 1.0 / (1.0 + jnp.exp(-x))


def _silu(x):
    return x * _sigmoid(x)


def _head_mask(shape, h):
    lane = _lane_iota(shape)
    return (lane >= h * HEAD_DIM) & (lane < (h + 1) * HEAD_DIM)


def _ada_kernel(c_ref, w_ref, b_ref, o_ref):
    o_ref[0] = _dot(_silu(c_ref[...]), w_ref[0]) + b_ref[0]


def _ada_call(c_all, w_ada, b_ada):
    depth, d, n = w_ada.shape
    m = c_all.shape[0]
    tn = 1536
    return pl.pallas_call(
        _ada_kernel,
        grid=(depth, n // tn),
        in_specs=[pl.BlockSpec((m, d), lambda i, j: (0, 0)),
                  pl.BlockSpec((1, d, tn), lambda i, j: (i, 0, j)),
                  pl.BlockSpec((1, 1, tn), lambda i, j: (i, 0, j))],
        out_specs=pl.BlockSpec((1, m, tn), lambda i, j: (i, 0, j)),
        out_shape=jax.ShapeDtypeStruct((depth, m, n), F32),
        compiler_params=_cparams("parallel", "parallel"),
        name="ada_mod",
    )(c_all, w_ada, b_ada.reshape(depth, 1, n))


def _rope128(x, cos, sin_signed, half):
    first = (_lane_iota(x.shape) % HEAD_DIM) < half
    partner = jnp.where(first, pltpu.roll(x, LANES - half, 1), pltpu.roll(x, half, 1))
    return x * cos + partner * sin_signed


def _mod_rows(m_ref, row, per_token):
    if per_token:
        d = m_ref.shape[-1] // 8
        return m_ref[:, row * d:(row + 1) * d]
    return m_ref[0, row:row + 1, :]


def _in_kernel(per_token, x_ref, m_ref, wr_ref, wu_ref, wsb_ref, wn_ref, tab_ref,
               ret_ref, u_ref, sbq_ref, sbkv_ref, sbkv16_ref, nq_ref, nqr_ref, rows_ref, win_ref, ng_ref,
               rowsc_ref, rowss16_ref, win16_ref):
    x = x_ref[...]
    h = (_ln_rows(x) * (1.0 + _mod_rows(m_ref, 1, per_token)) + _mod_rows(m_ref, 0, per_token)).astype(BF16)
    tab = tab_ref[...]
    rc, rs = tab[:, 0:128], tab[:, 128:256]
    qc, qs = tab[:, 256:384], tab[:, 384:512]
    kc, ks = tab[:, 512:640], tab[:, 640:768]

    zr = jnp.dot(h, wr_ref[...], preferred_element_type=F32)
    half_r = HEAD_DIM // 2
    for c in range(2):
        ret_ref[:, c * 128:(c + 1) * 128] = _rope128(zr[:, c * 128:(c + 1) * 128], rc, rs, half_r)
    for c in range(2, 4):
        ret_ref[:, c * 128:(c + 1) * 128] = _rope128(zr[:, c * 128:(c + 1) * 128], rc, rs, half_r) * QK_SCALE
    ret_ref[:, 512:1024] = zr[:, 512:1024]

    u_ref[...] = jnp.dot(h, wu_ref[...], preferred_element_type=F32)

    zs = jnp.dot(h, wsb_ref[...], preferred_element_type=F32)
    sbq_ref[...] = (zs[:, 0:256] * QK_SCALE).astype(BF16)
    sbkv_ref[...] = zs[:, 256:768]
    sbkv16_ref[...] = zs[:, 256:768].astype(BF16)

    zn = jnp.dot(h, wn_ref[...], preferred_element_type=F32)
    half_n = ROPE_DIM // 2
    nq_ref[...] = (zn[:, 0:256] * QK_SCALE).astype(BF16)
    for c in range(2):
        nqr_ref[:, c * 128:(c + 1) * 128] = (
            _rope128(zn[:, c * 128:(c + 1) * 128], qc, qs, half_n) * QK_SCALE).astype(BF16)
    rows_slc = _rope128(zn[:, 384:512], kc, ks, half_n)
    rows_ref[:, 0:128] = zn[:, 256:384]
    rows_ref[:, 128:256] = rows_slc
    rowsc_ref[...] = zn[:, 256:384]
    rowss16_ref[...] = rows_slc.astype(BF16)
    win = _rope128(zn[:, 512:640], kc, ks, half_n)
    win_ref[...] = win
    win16_ref[...] = win.astype(BF16)
    ng_ref[...] = _sigmoid(zn[:, 640:768])


def _in_call(x2, mod, tabs, w, tiles_per_batch, tm):
    t, d = x2.shape
    per_token = tiles_per_batch is None
    nt = t // tm
    if per_token:
        mod_spec = pl.BlockSpec((tm, 8 * d), lambda i: (i, 0))
        tab_spec = pl.BlockSpec((tm, 768), lambda i: (i, 0))
    else:
        mod_spec = pl.BlockSpec((1, 8, d), lambda i: (i // tiles_per_batch, 0, 0))
        tab_spec = pl.BlockSpec((tm, 768), lambda i: (i % tiles_per_batch, 0))

    def wspec(a):
        return pl.BlockSpec(a.shape, lambda i: (0, 0))

    def ospec(n):
        return pl.BlockSpec((tm, n), lambda i: (i, 0))

    outs = [(1024, F32), (256, F32), (256, BF16), (512, F32), (512, BF16), (256, BF16), (256, BF16),
            (256, F32), (128, F32), (128, F32), (128, F32), (128, BF16), (128, BF16)]
    return pl.pallas_call(
        functools.partial(_in_kernel, per_token),
        grid=(nt,),
        in_specs=[pl.BlockSpec((tm, d), lambda i: (i, 0)), mod_spec,
                  wspec(w["w_ret"]), wspec(w["w_u"]), wspec(w["w_sb"]), wspec(w["w_nsa"]), tab_spec],
        out_specs=[ospec(n) for n, _ in outs],
        out_shape=[jax.ShapeDtypeStruct((t, n), dt) for n, dt in outs],
        compiler_params=_cparams("parallel"),
        name="in_proj",
    )(x2, mod, w["w_ret"], w["w_u"], w["w_sb"], w["w_nsa"], tabs)


def _merge_kernel(per_token, alpha, x_ref, m_ref, a_ref, b_ref, c_ref, d_ref, wmg_ref, wbr_ref, wout_ref,
                  g_ref, bias_ref, o_ref):
    x = x_ref[...]
    dm = x.shape[-1]
    h = (_ln_rows(x) * (1.0 + _mod_rows(m_ref, 1, per_token)) + _mod_rows(m_ref, 0, per_token)).astype(BF16)
    acc = jnp.zeros(x.shape, F32)
    for n, br_ref in enumerate((a_ref, b_ref, c_ref, d_ref)):
        mg = jnp.dot(h, wmg_ref[:, n * dm:(n + 1) * dm], preferred_element_type=F32)
        br = jnp.dot(br_ref[...].astype(BF16), wbr_ref[n], preferred_element_type=F32)
        acc = acc + _sigmoid(mg) * br
    y = jnp.dot(acc.astype(BF16), wout_ref[...], preferred_element_type=F32)
    r = alpha * x + (1.0 + _mod_rows(m_ref, 2, per_token)) * y
    o_ref[...] = _ln_rows(r) * g_ref[...] + bias_ref[...]


def _merge_call(x2, mod, branches, w, tiles_per_batch, tm, alpha):
    t, d = x2.shape
    per_token = tiles_per_batch is None
    if per_token:
        mod_spec = pl.BlockSpec((tm, 8 * d), lambda i: (i, 0))
    else:
        mod_spec = pl.BlockSpec((1, 8, d), lambda i: (i // tiles_per_batch, 0, 0))
    row = lambda n: pl.BlockSpec((tm, n), lambda i: (i, 0))
    return pl.pallas_call(
        functools.partial(_merge_kernel, per_token, alpha),
        grid=(t // tm,),
        in_specs=[row(d), mod_spec, row(BRANCH_W), row(BRANCH_W), row(BRANCH_W), row(BRANCH_W),
                  pl.BlockSpec(w["w_mg"].shape, lambda i: (0, 0)),
                  pl.BlockSpec(w["w_branch"].shape, lambda i: (0, 0, 0)),
                  pl.BlockSpec(w["w_out"].shape, lambda i: (0, 0)),
                  pl.BlockSpec((1, d), lambda i: (0, 0)), pl.BlockSpec((1, d), lambda i: (0, 0))],
        out_specs=row(d),
        out_shape=jax.ShapeDtypeStruct((t, d), F32),
        compiler_params=_cparams("parallel"),
        name="merge_out",
    )(x2, mod, *branches, w["w_mg"], w["w_branch"], w["w_out"], w["ln1_g"], w["ln1_b"])


def _ffn_kernel(per_token, alpha, n_chunk, x_ref, m_ref, wup_ref, wdn_ref, g_ref, bias_ref, o_ref):
    x = x_ref[...]
    dff = wdn_ref.shape[0]
    ck = dff // n_chunk
    h = (_ln_rows(x) * (1.0 + _mod_rows(m_ref, 4, per_token)) + _mod_rows(m_ref, 3, per_token)).astype(BF16)
    f = jnp.zeros(x.shape, F32)
    for c in range(n_chunk):
        a = jnp.dot(h, wup_ref[:, c * ck:(c + 1) * ck], preferred_element_type=F32)
        u = jnp.dot(h, wup_ref[:, dff + c * ck:dff + (c + 1) * ck], preferred_element_type=F32)
        f = f + jnp.dot((_silu(a) * u).astype(BF16), wdn_ref[c * ck:(c + 1) * ck, :], preferred_element_type=F32)
    r = alpha * x + (1.0 + _mod_rows(m_ref, 5, per_token)) * f
    o_ref[...] = _ln_rows(r) * g_ref[...] + bias_ref[...]


def _ffn_call(x2, mod, w, tiles_per_batch, tm, alpha):
    t, d = x2.shape
    per_token = tiles_per_batch is None
    if per_token:
        mod_spec = pl.BlockSpec((tm, 8 * d), lambda i: (i, 0))
    else:
        mod_spec = pl.BlockSpec((1, 8, d), lambda i: (i // tiles_per_batch, 0, 0))
    dff = w["w_dn"].shape[0]
    n_chunk = 2 if dff % 256 == 0 else 1
    return pl.pallas_call(
        functools.partial(_ffn_kernel, per_token, alpha, n_chunk),
        grid=(t // tm,),
        in_specs=[pl.BlockSpec((tm, d), lambda i: (i, 0)), mod_spec,
                  pl.BlockSpec(w["w_up"].shape, lambda i: (0, 0)),
                  pl.BlockSpec(w["w_dn"].shape, lambda i: (0, 0)),
                  pl.BlockSpec((1, d), lambda i: (0, 0)), pl.BlockSpec((1, d), lambda i: (0, 0))],
        out_specs=pl.BlockSpec((tm, d), lambda i: (i, 0)),
        out_shape=jax.ShapeDtypeStruct((t, d), F32),
        compiler_params=_cparams("parallel"),
        name="ffn",
    )(x2, mod, w["w_up"], w["w_dn"], w["ln2_g"], w["ln2_b"])


def _ret_kernel(chunk, q_ref, k_ref, v_ref, g_ref, r0_ref, intra_ref, qdec_ref, kdec_ref, cdec_ref,
                o_ref, rout_ref, r_s):
    j = pl.program_id(1)

    @pl.when(j == 0)
    def _():
        r_s[...] = r0_ref[0]

    q = q_ref[...]
    k = k_ref[...]
    v16 = v_ref[...].astype(BF16)
    k16 = k.astype(BF16)
    o = _dot(q * qdec_ref[...], r_s[...])
    for h in range(N_HEADS):
        hm = _head_mask(q.shape, h)
        s = _dot_nt(jnp.where(hm, q, 0.0), k16) * intra_ref[h]
        o = o + jnp.where(hm, _dot(s, v16), 0.0)
    upd = _dot_tn(k * kdec_ref[...], v16)
    rr = _row_iota(upd.shape) // HEAD_DIM
    cc = _lane_iota(upd.shape) // HEAD_DIM
    r_new = r_s[...] * cdec_ref[...] + jnp.where(rr == cc, upd, 0.0)
    r_s[...] = r_new

    @pl.when(j == pl.num_programs(1) - 1)
    def _():
        rout_ref[0] = r_new

    mu = jnp.zeros(o.shape, F32)
    for h in range(N_HEADS):
        hm = _head_mask(o.shape, h)
        mu = mu + jnp.where(hm, jnp.sum(jnp.where(hm, o, 0.0), axis=-1, keepdims=True), 0.0)
    oc = o - mu * (1.0 / HEAD_DIM)
    var = jnp.zeros(o.shape, F32)
    oc2 = oc * oc
    for h in range(N_HEADS):
        hm = _head_mask(o.shape, h)
        var = var + jnp.where(hm, jnp.sum(jnp.where(hm, oc2, 0.0), axis=-1, keepdims=True), 0.0)
    o_ref[...] = oc * lax.rsqrt(var * (1.0 / HEAD_DIM) + LN_EPS) * _silu(g_ref[...])


def _ret_tables(chunk):
    h = jnp.arange(N_HEADS, dtype=F32)
    log_g = jnp.log1p(-jnp.exp2(-5.0 - h))
    idx = jnp.arange(chunk, dtype=F32)
    diff = idx[:, None] - idx[None, :]
    intra = jnp.where(diff >= 0, jnp.exp(jnp.maximum(diff, 0.0)[None] * log_g[:, None, None]), 0.0)
    q_dec = jnp.exp((idx + 1.0)[None, :] * log_g[:, None])
    k_dec = jnp.exp((chunk - 1.0 - idx)[None, :] * log_g[:, None])
    c_dec = jnp.exp(chunk * log_g)
    lanes = lambda t: jnp.repeat(t.T, HEAD_DIM, axis=1)
    return intra, lanes(q_dec), lanes(k_dec), jnp.repeat(c_dec, HEAD_DIM)[None, :]


def _ret_call(ret, b, l, r0_bd):
    chunk = RET_CHUNK if l % RET_CHUNK == 0 else l
    n = l // chunk
    intra, q_dec, k_dec, c_dec = _ret_tables(chunk)
    col = lambda c: pl.BlockSpec((chunk, BRANCH_W), lambda i, j: (i * n + j, c))
    full = lambda a: pl.BlockSpec(a.shape, lambda i, j: (0,) * a.ndim)
    return pl.pallas_call(
        functools.partial(_ret_kernel, chunk),
        grid=(b, n),
        in_specs=[col(0), col(1), col(2), col(3),
                  pl.BlockSpec((1, BRANCH_W, BRANCH_W), lambda i, j: (i, 0, 0)),
                  full(intra), full(q_dec), full(k_dec), full(c_dec)],
        out_specs=[pl.BlockSpec((chunk, BRANCH_W), lambda i, j: (i * n + j, 0)),
                   pl.BlockSpec((1, BRANCH_W, BRANCH_W), lambda i, j: (i, 0, 0))],
        out_shape=[jax.ShapeDtypeStruct((b * l, BRANCH_W), F32),
                   jax.ShapeDtypeStruct((b, BRANCH_W, BRANCH_W), F32)],
        scratch_shapes=[pltpu.VMEM((BRANCH_W, BRANCH_W), F32)],
        compiler_params=_cparams("parallel", "arbitrary"),
        name="retention",
    )(ret, ret, ret, ret, r0_bd, intra, q_dec, k_dec, c_dec)


def _bd_from_heads(r):
    b = r.shape[0]
    eye = jnp.eye(N_HEADS, dtype=r.dtype)
    return jnp.einsum("bhij,hg->bhigj", r, eye).reshape(b, BRANCH_W, BRANCH_W)


def _heads_from_bd(r_bd):
    b = r_bd.shape[0]
    r5 = r_bd.reshape(b, N_HEADS, HEAD_DIM, N_HEADS, HEAD_DIM)
    return jnp.stack([r5[:, h, :, h, :] for h in range(N_HEADS)], axis=1)


def _cmul(ar, ai, br, bi):
    return ar * br - ai * bi, ar * bi + ai * br


def _gelu_tanh(x):
    return 0.5 * x * (1.0 + jnp.tanh(math.sqrt(2.0 / math.pi) * (x + 0.044715 * (x * x * x))))


def _s5_kernel(chain, u_ref, x0_ref, are_ref, aim_ref, ldt_ref, bre_ref, bim_ref, cre_ref, cim_ref, dsk_ref,
               wglu_ref, bglu_ref, o_ref, st_ref, xr_s, xi_s, cr_s, ci_s):
    rows = u_ref.shape[0]
    n_grp = rows // SUBLANES
    j = pl.program_id(1)

    ar, ai = are_ref[...], aim_ref[...]
    dt = jnp.exp(ldt_ref[...])
    mag = jnp.exp(ar * dt)
    abr, abi = mag * jnp.cos(ai * dt), mag * jnp.sin(ai * dt)
    nr, ni = abr - 1.0, abi
    den = ar * ar + ai * ai
    fr = (nr * ar + ni * ai) / den
    fi = (ni * ar - nr * ai) / den
    bbr = fr * bre_ref[...] - fi * bim_ref[...]
    bbi = fr * bim_ref[...] + fi * bre_ref[...]

    u = u_ref[...]
    u16 = u.astype(BF16)
    xr_s[...] = jnp.dot(u16, bbr.astype(BF16), preferred_element_type=F32)
    xi_s[...] = jnp.dot(u16, bbi.astype(BF16), preferred_element_type=F32)

    a2r, a2i = _cmul(abr, abi, abr, abi)
    a4r, a4i = _cmul(a2r, a2i, a2r, a2i)
    row = _row_iota((SUBLANES, abr.shape[-1]))
    pr, pi_ = jnp.broadcast_to(abr, row.shape), jnp.broadcast_to(abi, row.shape)
    qr, qi = abr, abi
    for i in range(1, SUBLANES):
        qr, qi = _cmul(qr, qi, abr, abi)
        pr = jnp.where(row == i, qr, pr)
        pi_ = jnp.where(row == i, qi, pi_)
    steps = ((1, abr, abi), (2, a2r, a2i), (4, a4r, a4i))

    if chain:
        @pl.when(j == 0)
        def _():
            cr_s[...] = x0_ref[0, 0:1, :]
            ci_s[...] = x0_ref[0, 1:2, :]

    def group(g, carry):
        r0 = pl.multiple_of(g * SUBLANES, SUBLANES)
        xr = xr_s[pl.ds(r0, SUBLANES), :]
        xi = xi_s[pl.ds(r0, SUBLANES), :]
        for d, er, ei in steps:
            sr = jnp.where(row >= d, pltpu.roll(xr, d, 0), 0.0)
            si = jnp.where(row >= d, pltpu.roll(xi, d, 0), 0.0)
            tr, ti = _cmul(er, ei, sr, si)
            xr, xi = xr + tr, xi + ti
        if chain:
            c_r, c_i = cr_s[...], ci_s[...]
        else:
            c_r, c_i = x0_ref[g, 0:1, :], x0_ref[g, 1:2, :]
        tr, ti = _cmul(pr, pi_, c_r, c_i)
        xr, xi = xr + tr, xi + ti
        xr_s[pl.ds(r0, SUBLANES), :] = xr
        xi_s[pl.ds(r0, SUBLANES), :] = xi
        if chain:
            cr_s[...] = xr[SUBLANES - 1:SUBLANES, :]
            ci_s[...] = xi[SUBLANES - 1:SUBLANES, :]
        else:
            st_ref[g, 0:1, :] = xr[SUBLANES - 1:SUBLANES, :]
            st_ref[g, 1:2, :] = xi[SUBLANES - 1:SUBLANES, :]
        return carry

    lax.fori_loop(0, n_grp, group, 0)

    if chain:
        @pl.when(j == pl.num_programs(1) - 1)
        def _():
            st_ref[0, 0:1, :] = cr_s[...]
            st_ref[0, 1:2, :] = ci_s[...]

    y = (jnp.dot(xr_s[...].astype(BF16), cre_ref[...], preferred_element_type=F32)
         - jnp.dot(xi_s[...].astype(BF16), cim_ref[...], preferred_element_type=F32) + dsk_ref[...] * u)
    zb = _gelu_tanh(y)
    o_ref[...] = zb * _sigmoid(jnp.dot(zb.astype(BF16), wglu_ref[...], preferred_element_type=F32) + bglu_ref[...])


def _s5_call(u2, b, l, x0, w):
    n_state = x0.shape[-1]
    chain = l % 128 == 0
    if chain:
        tl = min(l, 512)
        grid = (b, l // tl)
        rows = tl
        u_spec = pl.BlockSpec((tl, BRANCH_W), lambda i, j: (i * (l // tl) + j, 0))
        x0_spec = pl.BlockSpec((1, 2, n_state), lambda i, j: (i, 0, 0))
    else:
        assert l == SUBLANES
        grid = (1, 1)
        rows = b * l
        u_spec = pl.BlockSpec((rows, BRANCH_W), lambda i, j: (0, 0))
        x0_spec = pl.BlockSpec((b, 2, n_state), lambda i, j: (0, 0, 0))
    full = lambda a: pl.BlockSpec(a.shape, lambda i, j: (0,) * a.ndim)
    params = [w["a_re"], w["a_im"], w["log_dt"], w["b_re_bd"], w["b_im_bd"], w["c_re_bd"], w["c_im_bd"],
              w["d_skip"], w["w_glu"], w["b_glu"]]
    return pl.pallas_call(
        functools.partial(_s5_kernel, chain),
        grid=grid,
        in_specs=[u_spec, x0_spec] + [full(a) for a in params],
        out_specs=[u_spec, x0_spec],
        out_shape=[jax.ShapeDtypeStruct((b * l, BRANCH_W), F32), jax.ShapeDtypeStruct(x0.shape, F32)],
        scratch_shapes=[pltpu.VMEM((rows, n_state), F32), pltpu.VMEM((rows, n_state), F32),
                        pltpu.VMEM((1, n_state), F32), pltpu.VMEM((1, n_state), F32)],
        compiler_params=_cparams("parallel", "arbitrary"),
        name="s5",
    )(u2, x0, *params)


def _s5_params(a_re, a_im, b_re, b_im, c_re, c_im, d_skip, log_dt, w_glu, b_glu):
    g, p = a_re.shape
    cg = b_re.shape[-1]
    eye = jnp.eye(g, dtype=F32)
    b_bd = lambda t: jnp.einsum("gpc,gh->gchp", t, eye).reshape(g * cg, g * p)
    c_bd = lambda t: jnp.einsum("gcp,gh->gphc", t, eye).reshape(g * p, g * cg)
    return {
        "a_re": a_re.reshape(1, g * p), "a_im": a_im.reshape(1, g * p),
        "log_dt": jnp.repeat(log_dt, p).reshape(1, g * p),
        "b_re_bd": b_bd(b_re), "b_im_bd": b_bd(b_im),
        "c_re_bd": c_bd(c_re).astype(BF16), "c_im_bd": c_bd(c_im).astype(BF16),
        "d_skip": d_skip.reshape(1, g * cg), "w_glu": w_glu.astype(BF16), "b_glu": b_glu.reshape(1, -1),
    }


TK = 128


def _stack_heads_bd(q):
    qf = q.astype(F32)
    return jnp.concatenate([jnp.where(_head_mask(qf.shape, h), qf, 0.0) for h in range(N_HEADS)], axis=0).astype(BF16)


def _unstack_heads_bd(acc, tq):
    out = jnp.zeros((tq, acc.shape[-1]), F32)
    for h in range(N_HEADS):
        blk = acc[h * tq:(h + 1) * tq]
        out = out + jnp.where(_head_mask(blk.shape, h), blk, 0.0)
    return out


def _suffix_matrix(tk):
    r = _row_iota((2 * tk, tk)) % tk
    c = _lane_iota((2 * tk, tk))
    return jnp.where(r > c, 1.0, 0.0).astype(BF16)


def _sb_tile(qbd, k16, v16, mask, u2, carry, acc, transposed):
    z = jnp.dot(qbd, k16, preferred_element_type=F32) if transposed else _dot_nt(qbd, k16)
    l1m = -(jnp.maximum(z, 0.0) + jnp.log(1.0 + jnp.exp(-jnp.abs(z))))
    if mask is not None:
        l1m = jnp.where(mask, l1m, 0.0)
    hi = l1m.astype(BF16)
    lo = (l1m - hi.astype(F32)).astype(BF16)
    suffix = jnp.dot(jnp.concatenate([hi, lo], axis=1), u2, preferred_element_type=F32)
    w = jnp.exp(z + l1m + suffix + carry)
    if mask is not None:
        w = jnp.where(mask, w, 0.0)
    w16 = w.astype(BF16)
    acc = acc + (_dot_nt(w16, v16) if transposed else jnp.dot(w16, v16, preferred_element_type=F32))
    carry = carry + (suffix[:, 0:1] + l1m[:, 0:1])
    return carry, acc


TKP = 256


def _sb_prompt_kernel(q_ref, kv_ref, o_ref):
    j = pl.program_id(1)
    tq = q_ref.shape[0]
    qbd = _stack_heads_bd(q_ref[...])
    u2 = _suffix_matrix(TKP)
    rows = N_HEADS * tq
    n_full = (j * tq) // TKP
    qpos = j * tq + _row_iota((rows, TKP)) % tq
    diag_mask = (n_full * TKP + _lane_iota((rows, TKP))) < qpos

    def tile(kt, mask, carry, acc):
        r0 = pl.multiple_of(kt * TKP, TKP)
        return _sb_tile(qbd, kv_ref[pl.ds(r0, TKP), 0:BRANCH_W], kv_ref[pl.ds(r0, TKP), BRANCH_W:2 * BRANCH_W],
                        mask, u2, carry, acc, False)

    carry, acc = tile(n_full, diag_mask, jnp.zeros((rows, 1), F32), jnp.zeros((rows, BRANCH_W), F32))
    carry, acc = lax.fori_loop(0, n_full, lambda it, c: tile(n_full - 1 - it, None, c[0], c[1]), (carry, acc))
    o_ref[...] = _unstack_heads_bd(acc, tq)


def _sb_prompt_call(sbq, sbkv16, b, l):
    tq = TK
    n = l // tq
    assert l % TKP == 0
    return pl.pallas_call(
        _sb_prompt_kernel,
        grid=(b, n),
        in_specs=[pl.BlockSpec((tq, BRANCH_W), lambda i, j: (i * n + j, 0)),
                  pl.BlockSpec((l, 2 * BRANCH_W), lambda i, j: (i, 0))],
        out_specs=pl.BlockSpec((tq, BRANCH_W), lambda i, j: (i * n + j, 0)),
        out_shape=jax.ShapeDtypeStruct((b * l, BRANCH_W), F32),
        compiler_params=_cparams("parallel", "arbitrary"),
        name="sb_prompt",
    )(sbq, sbkv16)


PAGES_PER_STEP = 8


def _sb_decode_kernel(n_pg, pt_ref, q_ref, kvn_ref, *rest):
    page_refs = rest[:n_pg]
    o_ref, carry_s, acc_s = rest[n_pg:]
    s = pl.program_id(1)
    tq = q_ref.shape[1]
    rows = N_HEADS * tq
    qbd = _stack_heads_bd(q_ref[0])
    u2 = _suffix_matrix(TK)

    @pl.when(s == 0)
    def _():
        mask = _lane_iota((rows, TK)) < (_row_iota((rows, TK)) % tq)
        kvn = kvn_ref[0]
        carry, acc = _sb_tile(qbd, kvn[0:BRANCH_W, :], kvn[BRANCH_W:, :], mask, u2,
                              jnp.zeros((rows, 1), F32), jnp.zeros((rows, BRANCH_W), F32), True)
        carry_s[...] = carry
        acc_s[...] = acc

    carry, acc = carry_s[...], acc_s[...]
    for pg in page_refs:
        kv = pg[0, 0].astype(BF16)
        carry, acc = _sb_tile(qbd, kv[0:BRANCH_W, :], kv[BRANCH_W:, :], None, u2, carry, acc, True)
    carry_s[...] = carry
    acc_s[...] = acc

    @pl.when(s == pl.num_programs(1) - 1)
    def _():
        o_ref[0] = _unstack_heads_bd(acc, tq)


def _sb_decode_call(sbq3, kvn_t16, cache_t, layer, page_table):
    b, tq, _ = sbq3.shape
    n_pages = page_table.shape[1]
    n_pg = math.gcd(PAGES_PER_STEP, n_pages)
    n_steps = n_pages // n_pg

    def page_spec(k):
        return pl.BlockSpec((1, 1, 2 * BRANCH_W, TK),
                            lambda i, s, pt: (layer, pt[i, n_pages - 1 - (s * n_pg + k)], 0, 0))

    grid_spec = pltpu.PrefetchScalarGridSpec(
        num_scalar_prefetch=1,
        grid=(b, n_steps),
        in_specs=[pl.BlockSpec((1, tq, BRANCH_W), lambda i, s, pt: (i, 0, 0)),
                  pl.BlockSpec((1, 2 * BRANCH_W, TK), lambda i, s, pt: (i, 0, 0))]
                 + [page_spec(k) for k in range(n_pg)],
        out_specs=pl.BlockSpec((1, tq, BRANCH_W), lambda i, s, pt: (i, 0, 0)),
        scratch_shapes=[pltpu.VMEM((N_HEADS * tq, 1), F32), pltpu.VMEM((N_HEADS * tq, BRANCH_W), F32)],
    )
    return pl.pallas_call(
        functools.partial(_sb_decode_kernel, n_pg),
        grid_spec=grid_spec,
        out_shape=jax.ShapeDtypeStruct((b, tq, BRANCH_W), F32),
        compiler_params=_cparams("parallel", "arbitrary"),
        name="sb_decode",
    )(page_table, sbq3, kvn_t16, *([cache_t] * n_pg))


def _round_up(x, m):
    return -(-x // m) * m


def _stack_heads_128(q):
    qf = q.astype(F32)
    tq = qf.shape[0]
    low = _lane_iota((tq, LANES)) < HEAD_DIM
    parts = []
    for h in range(N_HEADS):
        half = qf[:, LANES * (h // 2):LANES * (h // 2 + 1)]
        if h % 2 == 1:
            half = pltpu.roll(half, HEAD_DIM, 1)
        parts.append(jnp.where(low, half, 0.0))
    return jnp.concatenate(parts, axis=0)


def _compress(rows_ref, wc_ref, n):
    p0 = jnp.zeros((n, LANES), F32)
    p1 = jnp.zeros((n, LANES), F32)
    for s in range(CMP_STRIDE):
        x = rows_ref[pl.ds(s, n, stride=CMP_STRIDE), :].astype(BF16)
        p0 = p0 + jnp.dot(x, wc_ref[s], preferred_element_type=F32)
        p1 = p1 + jnp.dot(x, wc_ref[CMP_STRIDE + s], preferred_element_type=F32)
    return p0 + pltpu.roll(p1, n - 1, 0)


def _cmp_select(qp16, kvc, qpos_rows, qpos_q, tq, nsp, n_top):
    n = kvc.shape[0]
    kvc16 = kvc.astype(BF16)
    s = _dot_nt(qp16, kvc16)
    cmask = (_lane_iota(s.shape) * CMP_STRIDE + (CMP_BLOCK - 1)) <= qpos_rows
    s = jnp.where(cmask, s, NEG_BIG)
    m = jnp.max(s, axis=-1, keepdims=True)
    m = jnp.where(m > 0.5 * NEG_BIG, m, 0.0)
    e = jnp.where(cmask, jnp.exp(s - m), 0.0)
    p16 = (e / jnp.maximum(jnp.sum(e, axis=-1, keepdims=True), 1e-30)).astype(BF16)
    o_cmp = jnp.dot(p16, kvc16, preferred_element_type=F32)
    ci = _row_iota((n, nsp)) * CMP_STRIDE
    sj = _lane_iota((n, nsp)) * SLC_BLOCK
    c2s = jnp.where((ci < sj + SLC_BLOCK) & (sj < ci + CMP_BLOCK), 1.0, 0.0).astype(BF16)
    imp4 = jnp.dot(p16, c2s, preferred_element_type=F32)
    imp = imp4[0:tq]
    for h in range(1, N_HEADS):
        imp = imp + imp4[h * tq:(h + 1) * tq]

    blk = _lane_iota((tq, nsp))
    blkf = blk.astype(F32)
    cur = qpos_q // SLC_BLOCK
    forced = (blk == 0) | (blk == cur) | (blk == cur - 1)
    val = jnp.where(forced, jnp.inf, imp)
    avail = jnp.where(blk * SLC_BLOCK <= qpos_q, 1.0, 0.0)
    sel = jnp.zeros((tq, nsp), F32)
    for _ in range(n_top):
        vm = jnp.where(avail > 0.5, val, -jnp.inf)
        mx = jnp.max(vm, axis=-1, keepdims=True)
        cand = (avail > 0.5) & (vm == mx)
        first = jnp.min(jnp.where(cand, blkf, float(nsp)), axis=-1, keepdims=True)
        pick = blkf == first
        sel = jnp.where(pick, 1.0, sel)
        avail = jnp.where(pick, 0.0, avail)
    return o_cmp, sel.astype(BF16)


def _attn_tile(q16, kv16, mask, m, l, acc, transposed=False):
    s = jnp.dot(q16, kv16, preferred_element_type=F32) if transposed else _dot_nt(q16, kv16)
    s = jnp.where(mask, s, NEG_BIG)
    m_new = jnp.maximum(m, jnp.max(s, axis=-1, keepdims=True))
    p = jnp.where(mask, jnp.exp(s - m_new), 0.0)
    alpha = jnp.exp(m - m_new)
    l = alpha * l + jnp.sum(p, axis=-1, keepdims=True)
    p16 = p.astype(BF16)
    acc = alpha * acc + (_dot_nt(p16, kv16) if transposed else jnp.dot(p16, kv16, preferred_element_type=F32))
    return m_new, l, acc


def _attn_init(rows, width):
    return jnp.full((rows, 1), NEG_BIG, F32), jnp.zeros((rows, 1), F32), jnp.zeros((rows, width), F32)


def _slc_tile(kt, carry, qr16, kv16, selb, qpos_rows, nsp, tk, transposed):
    er = _row_iota((nsp, tk))
    ec = _lane_iota((nsp, tk))
    expand = jnp.where(er == (tk // SLC_BLOCK) * kt + ec // SLC_BLOCK, 1.0, 0.0).astype(BF16)
    seltok = jnp.dot(selb, expand, preferred_element_type=F32)
    seltok4 = jnp.concatenate([seltok] * N_HEADS, axis=0)
    tok = kt * tk + _lane_iota(seltok4.shape)
    mask = (seltok4 > 0.5) & (tok <= qpos_rows)
    return _attn_tile(qr16, kv16, mask, *carry, transposed=transposed)


def _win_tile(kv16, kpos, carry, qw16, qpos_rows):
    dist = qpos_rows - kpos
    mask = (dist >= 0) & (dist < WINDOW)
    return _attn_tile(qw16, kv16, mask, *carry)


def _nsa_combine(g, o_c, o_s, o_w, tq):
    comb = []
    for h in range(N_HEADS):
        sl = slice(h * tq, (h + 1) * tq)
        comb.append(g[:, h:h + 1] * o_c[sl] + g[:, N_HEADS + h:N_HEADS + h + 1] * o_s[sl]
                    + g[:, 2 * N_HEADS + h:2 * N_HEADS + h + 1] * o_w[sl])
    low = _lane_iota((tq, LANES)) < HEAD_DIM
    out01 = jnp.where(low, pltpu.roll(comb[0], HEAD_DIM, 1), comb[1])
    out23 = jnp.where(low, pltpu.roll(comb[2], HEAD_DIM, 1), comb[3])
    return jnp.concatenate([out01, out23], axis=1)


def _nsa_prompt_kernel(n_top, nq_ref, nqr_ref, ng_ref, rowsc_ref, rows_ref, win_ref, wc_ref, o_ref, kvc_s):
    j = pl.program_id(1)
    tq = nq_ref.shape[0]
    l_rows = rows_ref.shape[0]
    nsp = _round_up(l_rows // SLC_BLOCK, LANES)

    @pl.when(j == 0)
    def _():
        kvc_s[...] = _compress(rowsc_ref, wc_ref, kvc_s.shape[0])

    rows4 = N_HEADS * tq
    qpos_rows = j * tq + lax.broadcasted_iota(jnp.int32, (rows4, 1), 0) % tq
    qpos_q = j * tq + lax.broadcasted_iota(jnp.int32, (tq, 1), 0)
    qp16 = _stack_heads_128(nq_ref[...]).astype(BF16)
    qw = _stack_heads_128(nqr_ref[...])
    qw16 = qw.astype(BF16)

    o_c, selb = _cmp_select(qp16, kvc_s[...], qpos_rows, qpos_q, tq, nsp, n_top)
    last = (j * tq) // TKP

    def slc_body(kt, c):
        r0 = pl.multiple_of(kt * TKP, TKP)
        return _slc_tile(kt, c, qw16, rows_ref[pl.ds(r0, TKP), :], selb, qpos_rows, nsp, TKP, False)

    _, l_s, acc_s = lax.fori_loop(0, last + 1, slc_body, _attn_init(rows4, LANES))
    o_s = acc_s / l_s

    def win_body(kt, c):
        r0 = pl.multiple_of(kt * TKP, TKP)
        kpos = kt * TKP + _lane_iota((rows4, TKP))
        return _win_tile(win_ref[pl.ds(r0, TKP), :], kpos, c, qw16, qpos_rows)

    first = jnp.maximum(j * tq - WINDOW, 0) // TKP
    _, l_w, acc_w = lax.fori_loop(first, last + 1, win_body, _attn_init(rows4, LANES))
    o_ref[...] = _nsa_combine(ng_ref[...], o_c, o_s, acc_w / l_w, tq)


def _nsa_prompt_call(nq, nqr, ng, rowsc, rowss16, win16, wc, b, l):
    tq = TK
    n = l // tq
    n_top = min(N_SEL, l // SLC_BLOCK)
    qspec = lambda w: pl.BlockSpec((tq, w), lambda i, j: (i * n + j, 0))
    return pl.pallas_call(
        functools.partial(_nsa_prompt_kernel, n_top),
        grid=(b, n),
        in_specs=[qspec(BRANCH_W), qspec(BRANCH_W), qspec(LANES),
                  pl.BlockSpec((l, LANES), lambda i, j: (i, 0)),
                  pl.BlockSpec((l, LANES), lambda i, j: (i, 0)),
                  pl.BlockSpec((l, LANES), lambda i, j: (i, 0)),
                  pl.BlockSpec(wc.shape, lambda i, j: (0, 0, 0))],
        out_specs=qspec(BRANCH_W),
        out_shape=jax.ShapeDtypeStruct((b * l, BRANCH_W), F32),
        scratch_shapes=[pltpu.VMEM((l // CMP_STRIDE, LANES), F32)],
        compiler_params=_cparams("parallel", "arbitrary"),
        name="nsa_prompt",
    )(nq, nqr, ng, rowsc, rowss16, win16, wc)


def _nsa_decode_kernel(n_pg, past_len, n_cmp, n_top, pt_ref, nq_ref, nqr_ref, ng_ref, rown_ref, rownt_ref, cwin_ref,
                       winn_ref, wc_ref, *rest):
    page_refs = rest[:n_pg]
    o_ref, wst_ref, rowsc_s, rowst_s, win_s = rest[n_pg:]
    s = pl.program_id(1)
    tq = nq_ref.shape[1]
    n_steps = rowst_s.shape[0]
    slab = n_pg * TK
    for k, pg in enumerate(page_refs):
        r0 = pl.multiple_of((s * n_pg + k) * TK, TK)
        rowsc_s[pl.ds(r0, TK), :] = pg[0, 0, 0:LANES, :].T
        rowst_s[s, :, k * TK:(k + 1) * TK] = pg[0, 0, LANES:2 * LANES, :]

    @pl.when(s == pl.num_programs(1) - 1)
    def _():
        t_rows = rowsc_s.shape[0]
        w_rows = win_s.shape[0]
        wb = cwin_ref.shape[2]
        nsp = _round_up(_round_up(past_len + tq, SLC_BLOCK) // SLC_BLOCK, LANES)
        rowsc_s[past_len:past_len + tq, :] = rown_ref[0, :, 0:LANES]
        rowsc_s[past_len + tq:t_rows, :] = jnp.zeros((t_rows - past_len - tq, LANES), F32)
        win_s[0:wb, :] = cwin_ref[0, 0]
        win_s[wb:wb + tq, :] = winn_ref[0]
        win_s[wb + tq:w_rows, :] = jnp.zeros((w_rows - wb - tq, LANES), F32)
        wst_ref[0] = win_s[tq:wb + tq, :]

        rows4 = N_HEADS * tq
        qpos_rows = past_len + lax.broadcasted_iota(jnp.int32, (rows4, 1), 0) % tq
        qpos_q = past_len + lax.broadcasted_iota(jnp.int32, (tq, 1), 0)
        qp16 = _stack_heads_128(nq_ref[0]).astype(BF16)
        qw16 = _stack_heads_128(nqr_ref[0]).astype(BF16)

        kvc = _compress(rowsc_s, wc_ref, n_cmp)
        o_c, selb = _cmp_select(qp16, kvc, qpos_rows, qpos_q, tq, nsp, n_top)

        carry = lax.fori_loop(
            0, n_steps,
            lambda kt, c: _slc_tile(kt, c, qw16, rowst_s[kt].astype(BF16), selb, qpos_rows, nsp, slab, True),
            _attn_init(rows4, LANES))
        _, l_s, acc_s = _slc_tile(past_len // TK, carry, qw16, rownt_ref[0].astype(BF16), selb, qpos_rows, nsp, TK,
                                  True)
        o_s = acc_s / l_s

        carry = _attn_init(rows4, LANES)
        for kt in range(w_rows // TK):
            kpos = past_len - wb + kt * TK + _lane_iota((rows4, TK))
            carry = _win_tile(win_s[kt * TK:(kt + 1) * TK, :].astype(BF16), kpos, carry, qw16, qpos_rows)
        _, l_w, acc_w = carry
        o_ref[0] = _nsa_combine(ng_ref[0], o_c, o_s, acc_w / l_w, tq)


def _nsa_decode_call(nq3, nqr3, ng3, rown3, rownt3, winn3, cache_nsa_t, cache_win4, layer, page_table, wc):
    b, tq, _ = nq3.shape
    n_pages = page_table.shape[1]
    past_len = n_pages * TK
    wb = cache_win4.shape[2]
    n_pg = math.gcd(PAGES_PER_STEP, n_pages)
    n_steps = n_pages // n_pg
    t_pad = _round_up(past_len + tq, SLC_BLOCK)
    n_cmp = _round_up(t_pad // CMP_STRIDE, SUBLANES)
    t_rows = _round_up(n_cmp * CMP_STRIDE, TK)
    assert t_rows == past_len + TK
    w_rows = _round_up(wb + tq, TK)
    n_top = min(N_SEL, t_pad // SLC_BLOCK)

    def page_spec(k):
        return pl.BlockSpec((1, 1, BRANCH_W, TK), lambda i, s, pt: (layer, pt[i, s * n_pg + k], 0, 0))

    row3 = lambda r, w: pl.BlockSpec((1, r, w), lambda i, s, pt: (i, 0, 0))
    grid_spec = pltpu.PrefetchScalarGridSpec(
        num_scalar_prefetch=1,
        grid=(b, n_steps),
        in_specs=[row3(tq, BRANCH_W), row3(tq, BRANCH_W), row3(tq, LANES), row3(tq, BRANCH_W), row3(LANES, TK),
                  pl.BlockSpec((1, 1, wb, LANES), lambda i, s, pt: (layer, i, 0, 0)),
                  row3(tq, LANES),
                  pl.BlockSpec(wc.shape, lambda i, s, pt: (0, 0, 0))]
                 + [page_spec(k) for k in range(n_pg)],
        out_specs=[row3(tq, BRANCH_W), row3(wb, LANES)],
        scratch_shapes=[pltpu.VMEM((t_rows, LANES), F32), pltpu.VMEM((n_steps, LANES, n_pg * TK), F32),
                        pltpu.VMEM((w_rows, LANES), F32)],
    )
    return pl.pallas_call(
        functools.partial(_nsa_decode_kernel, n_pg, past_len, n_cmp, n_top),
        grid_spec=grid_spec,
        out_shape=[jax.ShapeDtypeStruct((b, tq, BRANCH_W), F32), jax.ShapeDtypeStruct((b, wb, LANES), F32)],
        compiler_params=_cparams("parallel", "arbitrary"),
        name="nsa_decode",
    )(page_table, nq3, nqr3, ng3, rown3, rownt3, cache_win4, winn3, wc, *([cache_nsa_t] * n_pg))


def _cmp_weights(w_cmp_k, w_cmp_v):
    z = jnp.zeros_like(w_cmp_k)
    top = jnp.concatenate([w_cmp_k, z], axis=2)
    bot = jnp.concatenate([z, w_cmp_v], axis=2)
    return jnp.concatenate([top, bot], axis=1).astype(BF16)


def _rope_tables(pos):
    posf = pos.astype(F32)[:, None]
    jj = np.arange(LANES) % HEAD_DIM

    def table(half, theta, lanes_on):
        inv = theta ** (-jnp.arange(half, dtype=F32) / half)
        ang = posf * inv[None, :]
        cos_l = jnp.cos(ang)[:, jj % half]
        sin_l = jnp.sin(ang)[:, jj % half] * jnp.where(jj < half, -1.0, 1.0).astype(F32)[None, :]
        on = jnp.asarray((jj < 2 * half) & lanes_on)[None, :]
        return jnp.where(on, cos_l, 1.0), jnp.where(on, sin_l, 0.0)

    every = np.ones(LANES, bool)
    first = np.arange(LANES) < HEAD_DIM
    parts = (*table(HEAD_DIM // 2, RET_THETA, every), *table(ROPE_DIM // 2, ROPE_THETA, every),
             *table(ROPE_DIM // 2, ROPE_THETA, first))
    return jnp.concatenate(parts, axis=1)


def _layer_weights(i, p):
    w_in = p["w_in"][i]
    d = w_in.shape[0]
    c_mg = w_in.shape[1] - N_BRANCH * d
    c_nsa = 8 * BRANCH_W
    w_nsa = w_in[:, c_nsa:c_mg]
    w_nsa = jnp.pad(w_nsa, ((0, 0), (0, 3 * BRANCH_W - w_nsa.shape[1])))
    w = {
        "w_ret": w_in[:, 0:4 * BRANCH_W].astype(BF16), "w_u": w_in[:, 4 * BRANCH_W:5 * BRANCH_W].astype(BF16),
        "w_sb": w_in[:, 5 * BRANCH_W:8 * BRANCH_W].astype(BF16), "w_nsa": w_nsa.astype(BF16),
        "w_mg": w_in[:, c_mg:].astype(BF16), "w_branch": p["w_branch"][i].astype(BF16),
        "w_out": p["w_out"][i].astype(BF16), "ln1_g": p["ln1_g"][i][None, :], "ln1_b": p["ln1_b"][i][None, :],
        "w_up": p["w_ffn_up"][i].astype(BF16), "w_dn": p["w_ffn_down"][i].astype(BF16),
        "ln2_g": p["ln2_g"][i][None, :], "ln2_b": p["ln2_b"][i][None, :],
        "wc": _cmp_weights(p["w_cmp_k"][i], p["w_cmp_v"][i]),
    }
    s5 = _s5_params(p["ssm_a_re"][i], p["ssm_a_im"][i], p["ssm_b_re"][i], p["ssm_b_im"][i], p["ssm_c_re"][i],
                    p["ssm_c_im"][i], p["ssm_d"][i], p["ssm_log_dt"][i], p["w_glu"][i], p["b_glu"][i])
    return w, s5


def _ssm_rows(s):
    b = s.shape[0]
    return jnp.moveaxis(s, -1, 1).reshape(b, 2, -1)


def _ssm_state(rows, g):
    b = rows.shape[0]
    return jnp.moveaxis(rows.reshape(b, 2, g, -1), 1, -1)


def _layer(x, mod, tabs, w, s5, alpha, past):
    b, l, d = x.shape
    t = b * l
    x2 = x.reshape(t, d)
    g = s5["a_re"].shape[-1] // SSM_STATE
    if past is None:
        tm = min(l, 512)
        tpb = l // tm
        mod_in, tab_in = mod, tabs
    else:
        tm, tpb = t, None
        mod_in = jnp.repeat(mod.reshape(b, 8 * d), l, axis=0)
        tab_in = jnp.tile(tabs, (b, 1))
    ret, u, sbq, sbkv, sbkv16, nq, nqr, rows, win, ng, rowsc, rowss16, win16 = _in_call(
        x2, mod_in, tab_in, w, tpb, tm)

    if past is None:
        r0 = jnp.zeros((b, BRANCH_W, BRANCH_W), F32)
        x0 = jnp.zeros((b, 2, g * SSM_STATE), F32)
    else:
        r0 = _bd_from_heads(past["ret"])
        x0 = _ssm_rows(past["ssm"])
    out_a, r_bd = _ret_call(ret, b, l, r0)
    out_b, s_rows = _s5_call(u, b, l, x0, s5)

    if past is None:
        out_c = _sb_prompt_call(sbq, sbkv16, b, l)
        out_d = _nsa_prompt_call(nq, nqr, ng, rowsc, rowss16, win16, w["wc"], b, l)
        win_state = win.reshape(b, l, 2, HEAD_DIM)[:, l - min(WINDOW, l):]
    else:
        kvn_t = jnp.swapaxes(jnp.pad(sbkv16.reshape(b, l, 2 * BRANCH_W), ((0, 0), (0, TK - l), (0, 0))), 1, 2)
        out_c = _sb_decode_call(sbq.reshape(b, l, BRANCH_W), kvn_t, past["sb"], past["layer"], past["page_table"])
        rows3 = rows.reshape(b, l, BRANCH_W)
        rown_t = jnp.swapaxes(jnp.pad(rows3[:, :, LANES:], ((0, 0), (0, TK - l), (0, 0))), 1, 2)
        out_d, wst = _nsa_decode_call(nq.reshape(b, l, BRANCH_W), nqr.reshape(b, l, BRANCH_W),
                                      ng.reshape(b, l, LANES), rows3, rown_t,
                                      win.reshape(b, l, LANES), past["nsa"], past["win"], past["layer"],
                                      past["page_table"], w["wc"])
        out_c = out_c.reshape(t, BRANCH_W)
        out_d = out_d.reshape(t, BRANCH_W)
        win_state = wst.reshape(b, -1, 2, HEAD_DIM)

    tm2 = min(tm, 256)
    tpb2 = None if tpb is None else l // tm2
    x1 = _merge_call(x2, mod_in, [out_a, out_b, out_c, out_d], w, tpb2, tm2, alpha)
    xo = _ffn_call(x1, mod_in, w, tpb2, tm2, alpha)
    state = (sbkv.reshape(b, l, 2, N_HEADS, HEAD_DIM), rows.reshape(b, l, 4, HEAD_DIM), win_state,
             _heads_from_bd(r_bd), _ssm_state(s_rows, g))
    return xo.reshape(b, l, d), state


def kernel(x_prompt, x_sample, c_prompt, c_sample, cache_sb, cache_nsa, cache_win, state_ret, state_ssm, page_table, w_ada, b_ada, w_in, ssm_a_re, ssm_a_im, ssm_b_re, ssm_b_im, ssm_c_re, ssm_c_im, ssm_d, ssm_log_dt, w_glu, b_glu, w_cmp_k, w_cmp_v, w_branch, w_out, ln1_g, ln1_b, w_ffn_up, w_ffn_down, ln2_g, ln2_b):
    p = dict(w_in=w_in, ssm_a_re=ssm_a_re, ssm_a_im=ssm_a_im, ssm_b_re=ssm_b_re, ssm_b_im=ssm_b_im,
             ssm_c_re=ssm_c_re, ssm_c_im=ssm_c_im, ssm_d=ssm_d, ssm_log_dt=ssm_log_dt, w_glu=w_glu, b_glu=b_glu,
             w_cmp_k=w_cmp_k, w_cmp_v=w_cmp_v, w_branch=w_branch, w_out=w_out, ln1_g=ln1_g, ln1_b=ln1_b,
             w_ffn_up=w_ffn_up, w_ffn_down=w_ffn_down, ln2_g=ln2_g, ln2_b=ln2_b)
    depth, d = w_ada.shape[0], w_ada.shape[1]
    bp, lp, _ = x_prompt.shape
    bs, ls, _ = x_sample.shape
    n_pool, page = cache_sb.shape[1], cache_sb.shape[2]
    past_len = page_table.shape[1] * page
    alpha = (2.0 * depth) ** 0.25

    c_all = jnp.concatenate([c_prompt, c_sample], axis=0)
    m_rows = _round_up(bp + bs, SUBLANES)
    c_all = jnp.pad(c_all, ((0, m_rows - bp - bs), (0, 0)))
    mods = _ada_call(c_all, w_ada, b_ada)

    tabs_p = _rope_tables(jnp.arange(lp, dtype=jnp.int32))
    tabs_s = _rope_tables(past_len + jnp.arange(ls, dtype=jnp.int32))
    cache_sb4 = jnp.transpose(cache_sb, (0, 1, 3, 4, 5, 2)).reshape(depth, n_pool, 2 * BRANCH_W, page)
    cache_nsa4 = jnp.transpose(cache_nsa, (0, 1, 3, 4, 2)).reshape(depth, n_pool, BRANCH_W, page)
    cache_win4 = cache_win.reshape(depth, bs, cache_win.shape[2], LANES)

    xp, xs = x_prompt, x_sample
    st_p = [[] for _ in range(5)]
    st_s = [[] for _ in range(5)]
    for i in range(depth):
        w, s5 = _layer_weights(i, p)
        mod_i = jnp.pad(mods[i].reshape(m_rows, 6, d), ((0, 0), (0, 2), (0, 0)))
        xp, new_p = _layer(xp, mod_i[:bp], tabs_p, w, s5, alpha, None)
        past = dict(sb=cache_sb4, nsa=cache_nsa4, win=cache_win4, ret=state_ret[i], ssm=state_ssm[i],
                    layer=i, page_table=page_table)
        xs, new_s = _layer(xs, mod_i[bp:bp + bs], tabs_s, w, s5, alpha, past)
        for k in range(5):
            st_p[k].append(new_p[k])
            st_s[k].append(new_s[k])
    sb_p, nsa_p, win_p, ret_p, ssm_p = [jnp.stack(s, axis=0) for s in st_p]
    sb_s, nsa_s, win_s, ret_s, ssm_s = [jnp.stack(s, axis=0) for s in st_s]
    return (xp, xs, sb_p, sb_s, nsa_p, nsa_s, win_p, win_s, ret_p, ret_s, ssm_p, ssm_s)
```

```python
import functools
import math

import jax
import jax.numpy as jnp
import numpy as np
from jax import lax
from jax.experimental import pallas as pl
from jax.experimental.pallas import tpu as pltpu

F32 = jnp.float32
BF16 = jnp.bfloat16

LANES = 128
SUBLANES = 8
VMEM_LIMIT_BYTES = 56 * 1024 * 1024

HEAD_DIM = 64
N_HEADS = 4
BRANCH_W = N_HEADS * HEAD_DIM
N_BRANCH = 4
RET_THETA = 10000.0
RET_CHUNK = 128
SSM_GROUP_CH = 16
SSM_STATE = 64
CMP_BLOCK = 32
CMP_STRIDE = 16
SLC_BLOCK = 64
N_SEL = 16
WINDOW = 512
ROPE_THETA = 500000.0
ROPE_DIM = HEAD_DIM // 4
LN_EPS = 1e-5
QK_SCALE = HEAD_DIM ** -0.5
NEG_BIG = -1e30


def _cparams(*sem, flags=None):
    return pltpu.CompilerParams(dimension_semantics=tuple(sem), vmem_limit_bytes=VMEM_LIMIT_BYTES, flags=flags)


def _lane_iota(shape):
    return lax.broadcasted_iota(jnp.int32, shape, len(shape) - 1)


def _row_iota(shape):
    return lax.broadcasted_iota(jnp.int32, shape, len(shape) - 2)


def _dot(a, b):
    return jnp.dot(a.astype(BF16), b.astype(BF16), preferred_element_type=F32)


def _dot_nt(a, b):
    return lax.dot_general(a.astype(BF16), b.astype(BF16), (((1,), (1,)), ((), ())), preferred_element_type=F32)


def _dot_tn(a, b):
    return lax.dot_general(a.astype(BF16), b.astype(BF16), (((0,), (0,)), ((), ())), preferred_element_type=F32)


def _ln_rows(x):
    mu = jnp.mean(x, axis=-1, keepdims=True)
    xc = x - mu
    var = jnp.mean(xc * xc, axis=-1, keepdims=True)
    return xc * lax.rsqrt(var + LN_EPS)


def _sigmoid(x):
    return 1.0 / (1.0 + jnp.exp(-x))


def _silu(x):
    return x * _sigmoid(x)


def _head_mask(shape, h):
    lane = _lane_iota(shape)
    return (lane >= h * HEAD_DIM) & (lane < (h + 1) * HEAD_DIM)


def _ada_kernel(c_ref, w_ref, b_ref, o_ref):
    o_ref[0] = _dot(_silu(c_ref[...]), w_ref[0]) + b_ref[0]


def _ada_call(c_all, w_ada, b_ada):
    depth, d, n = w_ada.shape
    m = c_all.shape[0]
    tn = 1536
    return pl.pallas_call(
        _ada_kernel,
        grid=(depth, n // tn),
        in_specs=[pl.BlockSpec((m, d), lambda i, j: (0, 0)),
                  pl.BlockSpec((1, d, tn), lambda i, j: (i, 0, j)),
                  pl.BlockSpec((1, 1, tn), lambda i, j: (i, 0, j))],
        out_specs=pl.BlockSpec((1, m, tn), lambda i, j: (i, 0, j)),
        out_shape=jax.ShapeDtypeStruct((depth, m, n), F32),
        compiler_params=_cparams("parallel", "parallel"),
        name="ada_mod",
    )(c_all, w_ada, b_ada.reshape(depth, 1, n))


def _rope128(x, cos, sin_signed, half):
    first = (_lane_iota(x.shape) % HEAD_DIM) < half
    partner = jnp.where(first, pltpu.roll(x, LANES - half, 1), pltpu.roll(x, half, 1))
    return x * cos + partner * sin_signed


def _mod_rows(m_ref, row, per_token):
    if per_token:
        d = m_ref.shape[-1] // 8
        return m_ref[:, row * d:(row + 1) * d]
    return m_ref[0, row:row + 1, :]


def _in_kernel(per_token, x_ref, m_ref, wr_ref, wu_ref, wsb_ref, wn_ref, tab_ref,
               ret_ref, u_ref, sbq_ref, sbkv_ref, sbkv16_ref, nq_ref, nqr_ref, rows_ref, win_ref, ng_ref,
               rowsc_ref, rowss16_ref, win16_ref):
    x = x_ref[...]
    h = (_ln_rows(x) * (1.0 + _mod_rows(m_ref, 1, per_token)) + _mod_rows(m_ref, 0, per_token)).astype(BF16)
    tab = tab_ref[...]
    rc, rs = tab[:, 0:128], tab[:, 128:256]
    qc, qs = tab[:, 256:384], tab[:, 384:512]
    kc, ks = tab[:, 512:640], tab[:, 640:768]

    zr = jnp.dot(h, wr_ref[...], preferred_element_type=F32)
    half_r = HEAD_DIM // 2
    for c in range(2):
        ret_ref[:, c * 128:(c + 1) * 128] = _rope128(zr[:, c * 128:(c + 1) * 128], rc, rs, half_r)
    for c in range(2, 4):
        ret_ref[:, c * 128:(c + 1) * 128] = _rope128(zr[:, c * 128:(c + 1) * 128], rc, rs, half_r) * QK_SCALE
    ret_ref[:, 512:1024] = zr[:, 512:1024]

    u_ref[...] = jnp.dot(h, wu_ref[...], preferred_element_type=F32)

    zs = jnp.dot(h, wsb_ref[...], preferred_element_type=F32)
    sbq_ref[...] = (zs[:, 0:256] * QK_SCALE).astype(BF16)
    sbkv_ref[...] = zs[:, 256:768]
    sbkv16_ref[...] = zs[:, 256:768].astype(BF16)

    zn = jnp.dot(h, wn_ref[...], preferred_element_type=F32)
    half_n = ROPE_DIM // 2
    nq_ref[...] = (zn[:, 0:256] * QK_SCALE).astype(BF16)
    for c in range(2):
        nqr_ref[:, c * 128:(c + 1) * 128] = (
            _rope128(zn[:, c * 128:(c + 1) * 128], qc, qs, half_n) * QK_SCALE).astype(BF16)
    rows_slc = _rope128(zn[:, 384:512], kc, ks, half_n)
    rows_ref[:, 0:128] = zn[:, 256:384]
    rows_ref[:, 128:256] = rows_slc
    rowsc_ref[...] = zn[:, 256:384]
    rowss16_ref[...] = rows_slc.astype(BF16)
    win = _rope128(zn[:, 512:640], kc, ks, half_n)
    win_ref[...] = win
    win16_ref[...] = win.astype(BF16)
    ng_ref[...] = _sigmoid(zn[:, 640:768])


def _in_call(x2, mod, tabs, w, tiles_per_batch, tm):
    t, d = x2.shape
    per_token = tiles_per_batch is None
    nt = t // tm
    if per_token:
        mod_spec = pl.BlockSpec((tm, 8 * d), lambda i: (i, 0))
        tab_spec = pl.BlockSpec((tm, 768), lambda i: (i, 0))
    else:
        mod_spec = pl.BlockSpec((1, 8, d), lambda i: (i // tiles_per_batch, 0, 0))
        tab_spec = pl.BlockSpec((tm, 768), lambda i: (i % tiles_per_batch, 0))

    def wspec(a):
        return pl.BlockSpec(a.shape, lambda i: (0, 0))

    def ospec(n):
        return pl.BlockSpec((tm, n), lambda i: (i, 0))

    outs = [(1024, F32), (256, F32), (256, BF16), (512, F32), (512, BF16), (256, BF16), (256, BF16),
            (256, F32), (128, F32), (128, F32), (128, F32), (128, BF16), (128, BF16)]
    return pl.pallas_call(
        functools.partial(_in_kernel, per_token),
        grid=(nt,),
        in_specs=[pl.BlockSpec((tm, d), lambda i: (i, 0)), mod_spec,
                  wspec(w["w_ret"]), wspec(w["w_u"]), wspec(w["w_sb"]), wspec(w["w_nsa"]), tab_spec],
        out_specs=[ospec(n) for n, _ in outs],
        out_shape=[jax.ShapeDtypeStruct((t, n), dt) for n, dt in outs],
        compiler_params=_cparams("parallel"),
        name="in_proj",
    )(x2, mod, w["w_ret"], w["w_u"], w["w_sb"], w["w_nsa"], tabs)


def _merge_kernel(per_token, alpha, x_ref, m_ref, a_ref, b_ref, c_ref, d_ref, wmg_ref, wbr_ref, wout_ref,
                  g_ref, bias_ref, o_ref):
    x = x_ref[...]
    dm = x.shape[-1]
    h = (_ln_rows(x) * (1.0 + _mod_rows(m_ref, 1, per_token)) + _mod_rows(m_ref, 0, per_token)).astype(BF16)
    acc = jnp.zeros(x.shape, F32)
    for n, br_ref in enumerate((a_ref, b_ref, c_ref, d_ref)):
        mg = jnp.dot(h, wmg_ref[:, n * dm:(n + 1) * dm], preferred_element_type=F32)
        br = jnp.dot(br_ref[...].astype(BF16), wbr_ref[n], preferred_element_type=F32)
        acc = acc + _sigmoid(mg) * br
    y = jnp.dot(acc.astype(BF16), wout_ref[...], preferred_element_type=F32)
    r = alpha * x + (1.0 + _mod_rows(m_ref, 2, per_token)) * y
    o_ref[...] = _ln_rows(r) * g_ref[...] + bias_ref[...]


def _merge_call(x2, mod, branches, w, tiles_per_batch, tm, alpha):
    t, d = x2.shape
    per_token = tiles_per_batch is None
    if per_token:
        mod_spec = pl.BlockSpec((tm, 8 * d), lambda i: (i, 0))
    else:
        mod_spec = pl.BlockSpec((1, 8, d), lambda i: (i // tiles_per_batch, 0, 0))
    row = lambda n: pl.BlockSpec((tm, n), lambda i: (i, 0))
    return pl.pallas_call(
        functools.partial(_merge_kernel, per_token, alpha),
        grid=(t // tm,),
        in_specs=[row(d), mod_spec, row(BRANCH_W), row(BRANCH_W), row(BRANCH_W), row(BRANCH_W),
                  pl.BlockSpec(w["w_mg"].shape, lambda i: (0, 0)),
                  pl.BlockSpec(w["w_branch"].shape, lambda i: (0, 0, 0)),
                  pl.BlockSpec(w["w_out"].shape, lambda i: (0, 0)),
                  pl.BlockSpec((1, d), lambda i: (0, 0)), pl.BlockSpec((1, d), lambda i: (0, 0))],
        out_specs=row(d),
        out_shape=jax.ShapeDtypeStruct((t, d), F32),
        compiler_params=_cparams("parallel"),
        name="merge_out",
    )(x2, mod, *branches, w["w_mg"], w["w_branch"], w["w_out"], w["ln1_g"], w["ln1_b"])


def _ffn_kernel(per_token, alpha, n_chunk, x_ref, m_ref, wup_ref, wdn_ref, g_ref, bias_ref, o_ref):
    x = x_ref[...]
    dff = wdn_ref.shape[0]
    ck = dff // n_chunk
    h = (_ln_rows(x) * (1.0 + _mod_rows(m_ref, 4, per_token)) + _mod_rows(m_ref, 3, per_token)).astype(BF16)
    f = jnp.zeros(x.shape, F32)
    for c in range(n_chunk):
        a = jnp.dot(h, wup_ref[:, c * ck:(c + 1) * ck], preferred_element_type=F32)
        u = jnp.dot(h, wup_ref[:, dff + c * ck:dff + (c + 1) * ck], preferred_element_type=F32)
        f = f + jnp.dot((_silu(a) * u).astype(BF16), wdn_ref[c * ck:(c + 1) * ck, :], preferred_element_type=F32)
    r = alpha * x + (1.0 + _mod_rows(m_ref, 5, per_token)) * f
    o_ref[...] = _ln_rows(r) * g_ref[...] + bias_ref[...]


def _ffn_call(x2, mod, w, tiles_per_batch, tm, alpha):
    t, d = x2.shape
    per_token = tiles_per_batch is None
    if per_token:
        mod_spec = pl.BlockSpec((tm, 8 * d), lambda i: (i, 0))
    else:
        mod_spec = pl.BlockSpec((1, 8, d), lambda i: (i // tiles_per_batch, 0, 0))
    dff = w["w_dn"].shape[0]
    n_chunk = 2 if dff % 256 == 0 else 1
    return pl.pallas_call(
        functools.partial(_ffn_kernel, per_token, alpha, n_chunk),
        grid=(t // tm,),
        in_specs=[pl.BlockSpec((tm, d), lambda i: (i, 0)), mod_spec,
                  pl.BlockSpec(w["w_up"].shape, lambda i: (0, 0)),
                  pl.BlockSpec(w["w_dn"].shape, lambda i: (0, 0)),
                  pl.BlockSpec((1, d), lambda i: (0, 0)), pl.BlockSpec((1, d), lambda i: (0, 0))],
        out_specs=pl.BlockSpec((tm, d), lambda i: (i, 0)),
        out_shape=jax.ShapeDtypeStruct((t, d), F32),
        compiler_params=_cparams("parallel"),
        name="ffn",
    )(x2, mod, w["w_up"], w["w_dn"], w["ln2_g"], w["ln2_b"])


def _ret_kernel(chunk, q_ref, k_ref, v_ref, g_ref, r0_ref, intra_ref, qdec_ref, kdec_ref, cdec_ref,
                o_ref, rout_ref, r_s):
    j = pl.program_id(1)

    @pl.when(j == 0)
    def _():
        r_s[...] = r0_ref[0]

    q = q_ref[...]
    k = k_ref[...]
    v16 = v_ref[...].astype(BF16)
    k16 = k.astype(BF16)
    o = _dot(q * qdec_ref[...], r_s[...])
    for h in range(N_HEADS):
        hm = _head_mask(q.shape, h)
        s = _dot_nt(jnp.where(hm, q, 0.0), k16) * intra_ref[h]
        o = o + jnp.where(hm, _dot(s, v16), 0.0)
    upd = _dot_tn(k * kdec_ref[...], v16)
    rr = _row_iota(upd.shape) // HEAD_DIM
    cc = _lane_iota(upd.shape) // HEAD_DIM
    r_new = r_s[...] * cdec_ref[...] + jnp.where(rr == cc, upd, 0.0)
    r_s[...] = r_new

    @pl.when(j == pl.num_programs(1) - 1)
    def _():
        rout_ref[0] = r_new

    mu = jnp.zeros(o.shape, F32)
    for h in range(N_HEADS):
        hm = _head_mask(o.shape, h)
        mu = mu + jnp.where(hm, jnp.sum(jnp.where(hm, o, 0.0), axis=-1, keepdims=True), 0.0)
    oc = o - mu * (1.0 / HEAD_DIM)
    var = jnp.zeros(o.shape, F32)
    oc2 = oc * oc
    for h in range(N_HEADS):
        hm = _head_mask(o.shape, h)
        var = var + jnp.where(hm, jnp.sum(jnp.where(hm, oc2, 0.0), axis=-1, keepdims=True), 0.0)
    o_ref[...] = oc * lax.rsqrt(var * (1.0 / HEAD_DIM) + LN_EPS) * _silu(g_ref[...])


def _ret_tables(chunk):
    h = jnp.arange(N_HEADS, dtype=F32)
    log_g = jnp.log1p(-jnp.exp2(-5.0 - h))
    idx = jnp.arange(chunk, dtype=F32)
    diff = idx[:, None] - idx[None, :]
    intra = jnp.where(diff >= 0, jnp.exp(jnp.maximum(diff, 0.0)[None] * log_g[:, None, None]), 0.0)
    q_dec = jnp.exp((idx + 1.0)[None, :] * log_g[:, None])
    k_dec = jnp.exp((chunk - 1.0 - idx)[None, :] * log_g[:, None])
    c_dec = jnp.exp(chunk * log_g)
    lanes = lambda t: jnp.repeat(t.T, HEAD_DIM, axis=1)
    return intra, lanes(q_dec), lanes(k_dec), jnp.repeat(c_dec, HEAD_DIM)[None, :]


def _ret_call(ret, b, l, r0_bd):
    chunk = RET_CHUNK if l % RET_CHUNK == 0 else l
    n = l // chunk
    intra, q_dec, k_dec, c_dec = _ret_tables(chunk)
    col = lambda c: pl.BlockSpec((chunk, BRANCH_W), lambda i, j: (i * n + j, c))
    full = lambda a: pl.BlockSpec(a.shape, lambda i, j: (0,) * a.ndim)
    return pl.pallas_call(
        functools.partial(_ret_kernel, chunk),
        grid=(b, n),
        in_specs=[col(0), col(1), col(2), col(3),
                  pl.BlockSpec((1, BRANCH_W, BRANCH_W), lambda i, j: (i, 0, 0)),
                  full(intra), full(q_dec), full(k_dec), full(c_dec)],
        out_specs=[pl.BlockSpec((chunk, BRANCH_W), lambda i, j: (i * n + j, 0)),
                   pl.BlockSpec((1, BRANCH_W, BRANCH_W), lambda i, j: (i, 0, 0))],
        out_shape=[jax.ShapeDtypeStruct((b * l, BRANCH_W), F32),
                   jax.ShapeDtypeStruct((b, BRANCH_W, BRANCH_W), F32)],
        scratch_shapes=[pltpu.VMEM((BRANCH_W, BRANCH_W), F32)],
        compiler_params=_cparams("parallel", "arbitrary"),
        name="retention",
    )(ret, ret, ret, ret, r0_bd, intra, q_dec, k_dec, c_dec)


def _bd_from_heads(r):
    b = r.shape[0]
    eye = jnp.eye(N_HEADS, dtype=r.dtype)
    return jnp.einsum("bhij,hg->bhigj", r, eye).reshape(b, BRANCH_W, BRANCH_W)


def _heads_from_bd(r_bd):
    b = r_bd.shape[0]
    r5 = r_bd.reshape(b, N_HEADS, HEAD_DIM, N_HEADS, HEAD_DIM)
    return jnp.stack([r5[:, h, :, h, :] for h in range(N_HEADS)], axis=1)


def _cmul(ar, ai, br, bi):
    return ar * br - ai * bi, ar * bi + ai * br


def _gelu_tanh(x):
    return 0.5 * x * (1.0 + jnp.tanh(math.sqrt(2.0 / math.pi) * (x + 0.044715 * (x * x * x))))


def _s5_kernel(chain, u_ref, x0_ref, are_ref, aim_ref, ldt_ref, bre_ref, bim_ref, cre_ref, cim_ref, dsk_ref,
               wglu_ref, bglu_ref, o_ref, st_ref, xr_s, xi_s, cr_s, ci_s):
    rows = u_ref.shape[0]
    n_grp = rows // SUBLANES
    j = pl.program_id(1)

    ar, ai = are_ref[...], aim_ref[...]
    dt = jnp.exp(ldt_ref[...])
    mag = jnp.exp(ar * dt)
    abr, abi = mag * jnp.cos(ai * dt), mag * jnp.sin(ai * dt)
    nr, ni = abr - 1.0, abi
    den = ar * ar + ai * ai
    fr = (nr * ar + ni * ai) / den
    fi = (ni * ar - nr * ai) / den
    bbr = fr * bre_ref[...] - fi * bim_ref[...]
    bbi = fr * bim_ref[...] + fi * bre_ref[...]

    u = u_ref[...]
    u16 = u.astype(BF16)
    xr_s[...] = jnp.dot(u16, bbr.astype(BF16), preferred_element_type=F32)
    xi_s[...] = jnp.dot(u16, bbi.astype(BF16), preferred_element_type=F32)

    a2r, a2i = _cmul(abr, abi, abr, abi)
    a4r, a4i = _cmul(a2r, a2i, a2r, a2i)
    row = _row_iota((SUBLANES, abr.shape[-1]))
    pr, pi_ = jnp.broadcast_to(abr, row.shape), jnp.broadcast_to(abi, row.shape)
    qr, qi = abr, abi
    for i in range(1, SUBLANES):
        qr, qi = _cmul(qr, qi, abr, abi)
        pr = jnp.where(row == i, qr, pr)
        pi_ = jnp.where(row == i, qi, pi_)
    steps = ((1, abr, abi), (2, a2r, a2i), (4, a4r, a4i))

    if chain:
        @pl.when(j == 0)
        def _():
            cr_s[...] = x0_ref[0, 0:1, :]
            ci_s[...] = x0_ref[0, 1:2, :]

    def group(g, carry):
        r0 = pl.multiple_of(g * SUBLANES, SUBLANES)
        xr = xr_s[pl.ds(r0, SUBLANES), :]
        xi = xi_s[pl.ds(r0, SUBLANES), :]
        for d, er, ei in steps:
            sr = jnp.where(row >= d, pltpu.roll(xr, d, 0), 0.0)
            si = jnp.where(row >= d, pltpu.roll(xi, d, 0), 0.0)
            tr, ti = _cmul(er, ei, sr, si)
            xr, xi = xr + tr, xi + ti
        if chain:
            c_r, c_i = cr_s[...], ci_s[...]
        else:
            c_r, c_i = x0_ref[g, 0:1, :], x0_ref[g, 1:2, :]
        tr, ti = _cmul(pr, pi_, c_r, c_i)
        xr, xi = xr + tr, xi + ti
        xr_s[pl.ds(r0, SUBLANES), :] = xr
        xi_s[pl.ds(r0, SUBLANES), :] = xi
        if chain:
            cr_s[...] = xr[SUBLANES - 1:SUBLANES, :]
            ci_s[...] = xi[SUBLANES - 1:SUBLANES, :]
        else:
            st_ref[g, 0:1, :] = xr[SUBLANES - 1:SUBLANES, :]
            st_ref[g, 1:2, :] = xi[SUBLANES - 1:SUBLANES, :]
        return carry

    lax.fori_loop(0, n_grp, group, 0)

    if chain:
        @pl.when(j == pl.num_programs(1) - 1)
        def _():
            st_ref[0, 0:1, :] = cr_s[...]
            st_ref[0, 1:2, :] = ci_s[...]

    y = (jnp.dot(xr_s[...].astype(BF16), cre_ref[...], preferred_element_type=F32)
         - jnp.dot(xi_s[...].astype(BF16), cim_ref[...], preferred_element_type=F32) + dsk_ref[...] * u)
    zb = _gelu_tanh(y)
    o_ref[...] = zb * _sigmoid(jnp.dot(zb.astype(BF16), wglu_ref[...], preferred_element_type=F32) + bglu_ref[...])


def _s5_call(u2, b, l, x0, w):
    n_state = x0.shape[-1]
    chain = l % 128 == 0
    if chain:
        tl = min(l, 512)
        grid = (b, l // tl)
        rows = tl
        u_spec = pl.BlockSpec((tl, BRANCH_W), lambda i, j: (i * (l // tl) + j, 0))
        x0_spec = pl.BlockSpec((1, 2, n_state), lambda i, j: (i, 0, 0))
    else:
        assert l == SUBLANES
        grid = (1, 1)
        rows = b * l
        u_spec = pl.BlockSpec((rows, BRANCH_W), lambda i, j: (0, 0))
        x0_spec = pl.BlockSpec((b, 2, n_state), lambda i, j: (0, 0, 0))
    full = lambda a: pl.BlockSpec(a.shape, lambda i, j: (0,) * a.ndim)
    params = [w["a_re"], w["a_im"], w["log_dt"], w["b_re_bd"], w["b_im_bd"], w["c_re_bd"], w["c_im_bd"],
              w["d_skip"], w["w_glu"], w["b_glu"]]
    return pl.pallas_call(
        functools.partial(_s5_kernel, chain),
        grid=grid,
        in_specs=[u_spec, x0_spec] + [full(a) for a in params],
        out_specs=[u_spec, x0_spec],
        out_shape=[jax.ShapeDtypeStruct((b * l, BRANCH_W), F32), jax.ShapeDtypeStruct(x0.shape, F32)],
        scratch_shapes=[pltpu.VMEM((rows, n_state), F32), pltpu.VMEM((rows, n_state), F32),
                        pltpu.VMEM((1, n_state), F32), pltpu.VMEM((1, n_state), F32)],
        compiler_params=_cparams("parallel", "arbitrary"),
        name="s5",
    )(u2, x0, *params)


def _s5_params(a_re, a_im, b_re, b_im, c_re, c_im, d_skip, log_dt, w_glu, b_glu):
    g, p = a_re.shape
    cg = b_re.shape[-1]
    eye = jnp.eye(g, dtype=F32)
    b_bd = lambda t: jnp.einsum("gpc,gh->gchp", t, eye).reshape(g * cg, g * p)
    c_bd = lambda t: jnp.einsum("gcp,gh->gphc", t, eye).reshape(g * p, g * cg)
    return {
        "a_re": a_re.reshape(1, g * p), "a_im": a_im.reshape(1, g * p),
        "log_dt": jnp.repeat(log_dt, p).reshape(1, g * p),
        "b_re_bd": b_bd(b_re), "b_im_bd": b_bd(b_im),
        "c_re_bd": c_bd(c_re).astype(BF16), "c_im_bd": c_bd(c_im).astype(BF16),
        "d_skip": d_skip.reshape(1, g * cg), "w_glu": w_glu.astype(BF16), "b_glu": b_glu.reshape(1, -1),
    }


TK = 128


def _stack_heads_bd(q):
    qf = q.astype(F32)
    return jnp.concatenate([jnp.where(_head_mask(qf.shape, h), qf, 0.0) for h in range(N_HEADS)], axis=0).astype(BF16)


def _unstack_heads_bd(acc, tq):
    out = jnp.zeros((tq, acc.shape[-1]), F32)
    for h in range(N_HEADS):
        blk = acc[h * tq:(h + 1) * tq]
        out = out + jnp.where(_head_mask(blk.shape, h), blk, 0.0)
    return out


def _suffix_matrix(tk):
    r = _row_iota((2 * tk, tk)) % tk
    c = _lane_iota((2 * tk, tk))
    return jnp.where(r > c, 1.0, 0.0).astype(BF16)


def _sb_scores(qbd, k16, transposed):
    return jnp.dot(qbd, k16, preferred_element_type=F32) if transposed else _dot_nt(qbd, k16)


def _sb_tile(qbd, k16, v16, mask, u2, carry, acc, transposed):
    return _sb_tile_z(_sb_scores(qbd, k16, transposed), v16, mask, u2, carry, acc, transposed)


def _sb_logw(z, mask, u2):
    log_beta = jnp.minimum(z, 0.0) - jnp.log(1.0 + jnp.exp(-jnp.abs(z)))
    l1m = log_beta - z
    if mask is not None:
        l1m = jnp.where(mask, l1m, 0.0)
    hi = lax.bitcast_convert_type(lax.bitcast_convert_type(l1m, jnp.uint32) & jnp.uint32(0xFFFF0000), F32)
    lo = l1m - hi
    suffix = jnp.dot(jnp.concatenate([hi.astype(BF16), lo.astype(BF16)], axis=1), u2, preferred_element_type=F32)
    return log_beta + suffix, suffix[:, 0:1] + l1m[:, 0:1]


def _sb_accumulate(logw, total, v16, mask, carry, acc, transposed):
    w = jnp.exp(logw + carry)
    if mask is not None:
        w = jnp.where(mask, w, 0.0)
    w16 = w.astype(BF16)
    acc = acc + (_dot_nt(w16, v16) if transposed else jnp.dot(w16, v16, preferred_element_type=F32))
    return carry + total, acc


def _sb_tile_z(z, v16, mask, u2, carry, acc, transposed):
    logw, total = _sb_logw(z, mask, u2)
    return _sb_accumulate(logw, total, v16, mask, carry, acc, transposed)


TKP = 256


def _sb_prompt_kernel(q_ref, kv_ref, o_ref):
    j = pl.program_id(1)
    tq = q_ref.shape[0]
    qbd = _stack_heads_bd(q_ref[...])
    u2 = _suffix_matrix(TKP)
    rows = N_HEADS * tq
    n_full = (j * tq) // TKP
    qpos = j * tq + _row_iota((rows, TKP)) % tq
    diag_mask = (n_full * TKP + _lane_iota((rows, TKP))) < qpos

    def keys(kt):
        return kv_ref[pl.ds(pl.multiple_of(kt * TKP, TKP), TKP), 0:BRANCH_W]

    def values(kt):
        return kv_ref[pl.ds(pl.multiple_of(kt * TKP, TKP), TKP), BRANCH_W:2 * BRANCH_W]

    carry, acc = _sb_tile(qbd, keys(n_full), values(n_full), diag_mask, u2,
                          jnp.zeros((rows, 1), F32), jnp.zeros((rows, BRANCH_W), F32), False)

    def body(it, st):
        carry, acc, z = st
        kt = n_full - 1 - it
        z_next = _sb_scores(qbd, keys(jnp.maximum(kt - 1, 0)), False)
        carry, acc = _sb_tile_z(z, values(kt), None, u2, carry, acc, False)
        return carry, acc, z_next

    z0 = _sb_scores(qbd, keys(jnp.maximum(n_full - 1, 0)), False)
    carry, acc, _ = lax.fori_loop(0, n_full, body, (carry, acc, z0))
    o_ref[...] = _unstack_heads_bd(acc, tq)


def _sb_prompt_call(sbq, sbkv16, b, l):
    tq = TK
    n = l // tq
    assert l % TKP == 0
    return pl.pallas_call(
        _sb_prompt_kernel,
        grid=(b, n),
        in_specs=[pl.BlockSpec((tq, BRANCH_W), lambda i, j: (i * n + j, 0)),
                  pl.BlockSpec((l, 2 * BRANCH_W), lambda i, j: (i, 0))],
        out_specs=pl.BlockSpec((tq, BRANCH_W), lambda i, j: (i * n + j, 0)),
        out_shape=jax.ShapeDtypeStruct((b * l, BRANCH_W), F32),
        compiler_params=_cparams("parallel", "arbitrary"),
        name="sb_prompt",
    )(sbq, sbkv16)


PAGES_PER_STEP = 8


def _sb_decode_kernel(n_pg, pt_ref, q_ref, kvn_ref, *rest):
    page_refs = rest[:n_pg]
    o_ref, carry_s, acc_s = rest[n_pg:]
    s = pl.program_id(1)
    tq = q_ref.shape[1]
    rows = N_HEADS * tq
    qbd = _stack_heads_bd(q_ref[0])
    u2 = _suffix_matrix(TK)

    @pl.when(s == 0)
    def _():
        mask = _lane_iota((rows, TK)) < (_row_iota((rows, TK)) % tq)
        kvn = kvn_ref[0]
        carry, acc = _sb_tile(qbd, kvn[0:BRANCH_W, :], kvn[BRANCH_W:, :], mask, u2,
                              jnp.zeros((rows, 1), F32), jnp.zeros((rows, BRANCH_W), F32), True)
        carry_s[...] = carry
        acc_s[...] = acc

    carry, acc = carry_s[...], acc_s[...]
    group = 2 if n_pg % 2 == 0 else 1
    u2g = _suffix_matrix(group * TK) if group > 1 else u2
    for k in range(0, n_pg, group):
        kv = jnp.concatenate([page_refs[k + g][0, 0] for g in reversed(range(group))], axis=1).astype(BF16)
        carry, acc = _sb_tile(qbd, kv[0:BRANCH_W, :], kv[BRANCH_W:, :], None, u2g, carry, acc, True)
    carry_s[...] = carry
    acc_s[...] = acc

    @pl.when(s == pl.num_programs(1) - 1)
    def _():
        o_ref[0] = _unstack_heads_bd(acc, tq)


def _sb_decode_call(sbq3, kvn_t16, cache_t, layer, page_table):
    b, tq, _ = sbq3.shape
    n_pages = page_table.shape[1]
    n_pg = math.gcd(PAGES_PER_STEP, n_pages)
    n_steps = n_pages // n_pg

    def page_spec(k):
        return pl.BlockSpec((1, 1, 2 * BRANCH_W, TK),
                            lambda i, s, pt: (layer, pt[i, n_pages - 1 - (s * n_pg + k)], 0, 0))

    grid_spec = pltpu.PrefetchScalarGridSpec(
        num_scalar_prefetch=1,
        grid=(b, n_steps),
        in_specs=[pl.BlockSpec((1, tq, BRANCH_W), lambda i, s, pt: (i, 0, 0)),
                  pl.BlockSpec((1, 2 * BRANCH_W, TK), lambda i, s, pt: (i, 0, 0))]
                 + [page_spec(k) for k in range(n_pg)],
        out_specs=pl.BlockSpec((1, tq, BRANCH_W), lambda i, s, pt: (i, 0, 0)),
        scratch_shapes=[pltpu.VMEM((N_HEADS * tq, 1), F32), pltpu.VMEM((N_HEADS * tq, BRANCH_W), F32)],
    )
    return pl.pallas_call(
        functools.partial(_sb_decode_kernel, n_pg),
        grid_spec=grid_spec,
        out_shape=jax.ShapeDtypeStruct((b, tq, BRANCH_W), F32),
        compiler_params=_cparams("parallel", "arbitrary"),
        name="sb_decode",
    )(page_table, sbq3, kvn_t16, *([cache_t] * n_pg))


def _round_up(x, m):
    return -(-x // m) * m


def _stack_heads_128(q):
    qf = q.astype(F32)
    tq = qf.shape[0]
    low = _lane_iota((tq, LANES)) < HEAD_DIM
    parts = []
    for h in range(N_HEADS):
        half = qf[:, LANES * (h // 2):LANES * (h // 2 + 1)]
        if h % 2 == 1:
            half = pltpu.roll(half, HEAD_DIM, 1)
        parts.append(jnp.where(low, half, 0.0))
    return jnp.concatenate(parts, axis=0)


def _compress(rows_ref, wc_ref, n):
    p0 = jnp.zeros((n, LANES), F32)
    p1 = jnp.zeros((n, LANES), F32)
    for s in range(CMP_STRIDE):
        x = rows_ref[pl.ds(s, n, stride=CMP_STRIDE), :].astype(BF16)
        p0 = p0 + jnp.dot(x, wc_ref[s], preferred_element_type=F32)
        p1 = p1 + jnp.dot(x, wc_ref[CMP_STRIDE + s], preferred_element_type=F32)
    return p0 + pltpu.roll(p1, n - 1, 0)


def _cmp_select(qp16, kvc, qpos_rows, qpos_q, tq, nsp, n_top):
    n = kvc.shape[0]
    kvc16 = kvc.astype(BF16)
    s = _dot_nt(qp16, kvc16)
    cmask = (_lane_iota(s.shape) * CMP_STRIDE + (CMP_BLOCK - 1)) <= qpos_rows
    s = jnp.where(cmask, s, NEG_BIG)
    m = jnp.max(s, axis=-1, keepdims=True)
    m = jnp.where(m > 0.5 * NEG_BIG, m, 0.0)
    e = jnp.where(cmask, jnp.exp(s - m), 0.0)
    p16 = (e / jnp.maximum(jnp.sum(e, axis=-1, keepdims=True), 1e-30)).astype(BF16)
    o_cmp = jnp.dot(p16, kvc16, preferred_element_type=F32)
    ci = _row_iota((n, nsp)) * CMP_STRIDE
    sj = _lane_iota((n, nsp)) * SLC_BLOCK
    c2s = jnp.where((ci < sj + SLC_BLOCK) & (sj < ci + CMP_BLOCK), 1.0, 0.0).astype(BF16)
    imp4 = jnp.dot(p16, c2s, preferred_element_type=F32)
    imp = imp4[0:tq]
    for h in range(1, N_HEADS):
        imp = imp + imp4[h * tq:(h + 1) * tq]

    blk = _lane_iota((tq, nsp))
    blkf = blk.astype(F32)
    cur = qpos_q // SLC_BLOCK
    forced = (blk == 0) | (blk == cur) | (blk == cur - 1)
    val = jnp.where(forced, jnp.inf, imp)
    avail = jnp.where(blk * SLC_BLOCK <= qpos_q, 1.0, 0.0)
    sel = jnp.zeros((tq, nsp), F32)
    for _ in range(n_top):
        vm = jnp.where(avail > 0.5, val, -jnp.inf)
        mx = jnp.max(vm, axis=-1, keepdims=True)
        cand = (avail > 0.5) & (vm == mx)
        first = jnp.min(jnp.where(cand, blkf, float(nsp)), axis=-1, keepdims=True)
        pick = blkf == first
        sel = jnp.where(pick, 1.0, sel)
        avail = jnp.where(pick, 0.0, avail)
    return o_cmp, sel.astype(BF16)


def _attn_tile(q16, kv16, mask, m, l, acc, transposed=False):
    s = jnp.dot(q16, kv16, preferred_element_type=F32) if transposed else _dot_nt(q16, kv16)
    s = jnp.where(mask, s, NEG_BIG)
    m_new = jnp.maximum(m, jnp.max(s, axis=-1, keepdims=True))
    p = jnp.where(mask, jnp.exp(s - m_new), 0.0)
    alpha = jnp.exp(m - m_new)
    l = alpha * l + jnp.sum(p, axis=-1, keepdims=True)
    p16 = p.astype(BF16)
    acc = alpha * acc + (_dot_nt(p16, kv16) if transposed else jnp.dot(p16, kv16, preferred_element_type=F32))
    return m_new, l, acc


def _attn_update(s, kv16, m, l, acc):
    m_new = jnp.maximum(m, jnp.max(s, axis=-1, keepdims=True))
    p = jnp.exp(s - m_new)
    alpha = jnp.exp(m - m_new)
    l = alpha * l + jnp.sum(p, axis=-1, keepdims=True)
    acc = alpha * acc + jnp.dot(p.astype(BF16), kv16, preferred_element_type=F32)
    return m_new, l, acc


def _attn_init(rows, width):
    return jnp.full((rows, 1), NEG_BIG, F32), jnp.zeros((rows, 1), F32), jnp.zeros((rows, width), F32)


def _slc_tile(kt, carry, qr16, kv16, selb, qpos_rows, nsp, tk, transposed):
    er = _row_iota((nsp, tk))
    ec = _lane_iota((nsp, tk))
    expand = jnp.where(er == (tk // SLC_BLOCK) * kt + ec // SLC_BLOCK, 1.0, 0.0).astype(BF16)
    seltok = jnp.dot(selb, expand, preferred_element_type=F32)
    seltok4 = jnp.concatenate([seltok] * N_HEADS, axis=0)
    tok = kt * tk + _lane_iota(seltok4.shape)
    mask = (seltok4 > 0.5) & (tok <= qpos_rows)
    return _attn_tile(qr16, kv16, mask, *carry, transposed=transposed)


def _win_tile(kv16, kpos, carry, qw16, qpos_rows):
    dist = qpos_rows - kpos
    mask = (dist >= 0) & (dist < WINDOW)
    return _attn_tile(qw16, kv16, mask, *carry)


def _nsa_combine(g, o_c, o_s, o_w, tq):
    comb = []
    for h in range(N_HEADS):
        sl = slice(h * tq, (h + 1) * tq)
        comb.append(g[:, h:h + 1] * o_c[sl] + g[:, N_HEADS + h:N_HEADS + h + 1] * o_s[sl]
                    + g[:, 2 * N_HEADS + h:2 * N_HEADS + h + 1] * o_w[sl])
    low = _lane_iota((tq, LANES)) < HEAD_DIM
    out01 = jnp.where(low, pltpu.roll(comb[0], HEAD_DIM, 1), comb[1])
    out23 = jnp.where(low, pltpu.roll(comb[2], HEAD_DIM, 1), comb[3])
    return jnp.concatenate([out01, out23], axis=1)


def _nsa_prompt_kernel(n_top, nq_ref, nqr_ref, ng_ref, rowsc_ref, rows_ref, win_ref, onehot_ref, wc_ref, o_ref, kvc_s):
    j = pl.program_id(1)
    tq = nq_ref.shape[0]
    l_rows = rows_ref.shape[0]
    nsp = _round_up(l_rows // SLC_BLOCK, LANES)

    @pl.when(j == 0)
    def _():
        kvc_s[...] = _compress(rowsc_ref, wc_ref, kvc_s.shape[0])

    rows4 = N_HEADS * tq
    qpos_rows = j * tq + lax.broadcasted_iota(jnp.int32, (rows4, 1), 0) % tq
    qpos_q = j * tq + lax.broadcasted_iota(jnp.int32, (tq, 1), 0)
    qp16 = _stack_heads_128(nq_ref[...]).astype(BF16)
    qw = _stack_heads_128(nqr_ref[...])
    qw16 = qw.astype(BF16)

    o_c, selb = _cmp_select(qp16, kvc_s[...], qpos_rows, qpos_q, tq, nsp, n_top)
    last = (j * tq) // TKP

    sel_bias = (selb.astype(F32) - 1.0) * (-NEG_BIG)
    q_aug = jnp.concatenate([qw, jnp.concatenate([sel_bias] * N_HEADS, axis=0)], axis=1).astype(BF16)

    def slc_scores(kt):
        r0 = pl.multiple_of(kt * TKP, TKP)
        return _dot_nt(q_aug, jnp.concatenate([rows_ref[pl.ds(r0, TKP), :], onehot_ref[pl.ds(r0, TKP), :]], axis=1))

    def slc_values(kt):
        return rows_ref[pl.ds(pl.multiple_of(kt * TKP, TKP), TKP), :]

    def slc_body(kt, st):
        m, l, acc, s = st
        s_next = slc_scores(jnp.minimum(kt + 1, last))
        m, l, acc = _attn_update(s, slc_values(kt), m, l, acc)
        return m, l, acc, s_next

    m_s, l_s, acc_s, s_diag = lax.fori_loop(0, last, slc_body, (*_attn_init(rows4, LANES), slc_scores(0)))
    tok = last * TKP + _lane_iota((rows4, TKP))
    s_diag = jnp.where(tok <= qpos_rows, s_diag, NEG_BIG)
    _, l_s, acc_s = _attn_update(s_diag, slc_values(last), m_s, l_s, acc_s)
    o_s = acc_s / l_s

    def win_body(kt, c):
        r0 = pl.multiple_of(kt * TKP, TKP)
        kpos = kt * TKP + _lane_iota((rows4, TKP))
        return _win_tile(win_ref[pl.ds(r0, TKP), :], kpos, c, qw16, qpos_rows)

    first = jnp.maximum(j * tq - WINDOW, 0) // TKP
    _, l_w, acc_w = lax.fori_loop(first, last + 1, win_body, _attn_init(rows4, LANES))
    o_ref[...] = _nsa_combine(ng_ref[...], o_c, o_s, acc_w / l_w, tq)


def _nsa_prompt_call(nq, nqr, ng, rowsc, rowss16, win16, wc, b, l):
    tq = TK
    n = l // tq
    n_top = min(N_SEL, l // SLC_BLOCK)
    nsp = _round_up(l // SLC_BLOCK, LANES)
    onehot = (jnp.arange(l)[:, None] // SLC_BLOCK == jnp.arange(nsp)[None, :]).astype(BF16)
    qspec = lambda w: pl.BlockSpec((tq, w), lambda i, j: (i * n + j, 0))
    return pl.pallas_call(
        functools.partial(_nsa_prompt_kernel, n_top),
        grid=(b, n),
        in_specs=[qspec(BRANCH_W), qspec(BRANCH_W), qspec(LANES),
                  pl.BlockSpec((l, LANES), lambda i, j: (i, 0)),
                  pl.BlockSpec((l, LANES), lambda i, j: (i, 0)),
                  pl.BlockSpec((l, LANES), lambda i, j: (i, 0)),
                  pl.BlockSpec(onehot.shape, lambda i, j: (0, 0)),
                  pl.BlockSpec(wc.shape, lambda i, j: (0, 0, 0))],
        out_specs=qspec(BRANCH_W),
        out_shape=jax.ShapeDtypeStruct((b * l, BRANCH_W), F32),
        scratch_shapes=[pltpu.VMEM((l // CMP_STRIDE, LANES), F32)],
        compiler_params=_cparams("parallel", "arbitrary"),
        name="nsa_prompt",
    )(nq, nqr, ng, rowsc, rowss16, win16, onehot, wc)


def _nsa_decode_kernel(n_pg, past_len, n_cmp, n_top, pt_ref, nq_ref, nqr_ref, ng_ref, rown_ref, rownt_ref, cwin_ref,
                       winn_ref, wc_ref, *rest):
    page_refs = rest[:n_pg]
    o_ref, wst_ref, rowsc_s, rowst_s, win_s = rest[n_pg:]
    s = pl.program_id(1)
    tq = nq_ref.shape[1]
    n_steps = rowst_s.shape[0]
    slab = n_pg * TK
    for k, pg in enumerate(page_refs):
        r0 = pl.multiple_of((s * n_pg + k) * TK, TK)
        rowsc_s[pl.ds(r0, TK), :] = pg[0, 0, 0:LANES, :].T
        rowst_s[s, :, k * TK:(k + 1) * TK] = pg[0, 0, LANES:2 * LANES, :]

    @pl.when(s == pl.num_programs(1) - 1)
    def _():
        t_rows = rowsc_s.shape[0]
        w_rows = win_s.shape[0]
        wb = cwin_ref.shape[2]
        nsp = _round_up(_round_up(past_len + tq, SLC_BLOCK) // SLC_BLOCK, LANES)
        rowsc_s[past_len:past_len + tq, :] = rown_ref[0, :, 0:LANES]
        rowsc_s[past_len + tq:t_rows, :] = jnp.zeros((t_rows - past_len - tq, LANES), F32)
        win_s[0:wb, :] = cwin_ref[0, 0]
        win_s[wb:wb + tq, :] = winn_ref[0]
        win_s[wb + tq:w_rows, :] = jnp.zeros((w_rows - wb - tq, LANES), F32)
        wst_ref[0] = win_s[tq:wb + tq, :]

        rows4 = N_HEADS * tq
        qpos_rows = past_len + lax.broadcasted_iota(jnp.int32, (rows4, 1), 0) % tq
        qpos_q = past_len + lax.broadcasted_iota(jnp.int32, (tq, 1), 0)
        qp16 = _stack_heads_128(nq_ref[0]).astype(BF16)
        qw16 = _stack_heads_128(nqr_ref[0]).astype(BF16)

        kvc = _compress(rowsc_s, wc_ref, n_cmp)
        o_c, selb = _cmp_select(qp16, kvc, qpos_rows, qpos_q, tq, nsp, n_top)

        carry = lax.fori_loop(
            0, n_steps,
            lambda kt, c: _slc_tile(kt, c, qw16, rowst_s[kt].astype(BF16), selb, qpos_rows, nsp, slab, True),
            _attn_init(rows4, LANES))
        _, l_s, acc_s = _slc_tile(past_len // TK, carry, qw16, rownt_ref[0].astype(BF16), selb, qpos_rows, nsp, TK,
                                  True)
        o_s = acc_s / l_s

        carry = _attn_init(rows4, LANES)
        for kt in range(w_rows // TK):
            kpos = past_len - wb + kt * TK + _lane_iota((rows4, TK))
            carry = _win_tile(win_s[kt * TK:(kt + 1) * TK, :].astype(BF16), kpos, carry, qw16, qpos_rows)
        _, l_w, acc_w = carry
        o_ref[0] = _nsa_combine(ng_ref[0], o_c, o_s, acc_w / l_w, tq)


def _nsa_decode_call(nq3, nqr3, ng3, rown3, rownt3, winn3, cache_nsa_t, cache_win4, layer, page_table, wc):
    b, tq, _ = nq3.shape
    n_pages = page_table.shape[1]
    past_len = n_pages * TK
    wb = cache_win4.shape[2]
    n_pg = math.gcd(PAGES_PER_STEP, n_pages)
    n_steps = n_pages // n_pg
    t_pad = _round_up(past_len + tq, SLC_BLOCK)
    n_cmp = _round_up(t_pad // CMP_STRIDE, SUBLANES)
    t_rows = _round_up(n_cmp * CMP_STRIDE, TK)
    assert t_rows == past_len + TK
    w_rows = _round_up(wb + tq, TK)
    n_top = min(N_SEL, t_pad // SLC_BLOCK)

    def page_spec(k):
        return pl.BlockSpec((1, 1, BRANCH_W, TK), lambda i, s, pt: (layer, pt[i, s * n_pg + k], 0, 0))

    row3 = lambda r, w: pl.BlockSpec((1, r, w), lambda i, s, pt: (i, 0, 0))
    grid_spec = pltpu.PrefetchScalarGridSpec(
        num_scalar_prefetch=1,
        grid=(b, n_steps),
        in_specs=[row3(tq, BRANCH_W), row3(tq, BRANCH_W), row3(tq, LANES), row3(tq, BRANCH_W), row3(LANES, TK),
                  pl.BlockSpec((1, 1, wb, LANES), lambda i, s, pt: (layer, i, 0, 0)),
                  row3(tq, LANES),
                  pl.BlockSpec(wc.shape, lambda i, s, pt: (0, 0, 0))]
                 + [page_spec(k) for k in range(n_pg)],
        out_specs=[row3(tq, BRANCH_W), row3(wb, LANES)],
        scratch_shapes=[pltpu.VMEM((t_rows, LANES), F32), pltpu.VMEM((n_steps, LANES, n_pg * TK), F32),
                        pltpu.VMEM((w_rows, LANES), F32)],
    )
    return pl.pallas_call(
        functools.partial(_nsa_decode_kernel, n_pg, past_len, n_cmp, n_top),
        grid_spec=grid_spec,
        out_shape=[jax.ShapeDtypeStruct((b, tq, BRANCH_W), F32), jax.ShapeDtypeStruct((b, wb, LANES), F32)],
        compiler_params=_cparams("parallel", "arbitrary"),
        name="nsa_decode",
    )(page_table, nq3, nqr3, ng3, rown3, rownt3, cache_win4, winn3, wc, *([cache_nsa_t] * n_pg))


def _cmp_weights(w_cmp_k, w_cmp_v):
    z = jnp.zeros_like(w_cmp_k)
    top = jnp.concatenate([w_cmp_k, z], axis=2)
    bot = jnp.concatenate([z, w_cmp_v], axis=2)
    return jnp.concatenate([top, bot], axis=1).astype(BF16)


def _rope_tables(pos):
    posf = pos.astype(F32)[:, None]
    jj = np.arange(LANES) % HEAD_DIM

    def table(half, theta, lanes_on):
        inv = theta ** (-jnp.arange(half, dtype=F32) / half)
        ang = posf * inv[None, :]
        cos_l = jnp.cos(ang)[:, jj % half]
        sin_l = jnp.sin(ang)[:, jj % half] * jnp.where(jj < half, -1.0, 1.0).astype(F32)[None, :]
        on = jnp.asarray((jj < 2 * half) & lanes_on)[None, :]
        return jnp.where(on, cos_l, 1.0), jnp.where(on, sin_l, 0.0)

    every = np.ones(LANES, bool)
    first = np.arange(LANES) < HEAD_DIM
    parts = (*table(HEAD_DIM // 2, RET_THETA, every), *table(ROPE_DIM // 2, ROPE_THETA, every),
             *table(ROPE_DIM // 2, ROPE_THETA, first))
    return jnp.concatenate(parts, axis=1)


def _layer_weights(i, p):
    w_in = p["w_in"][i]
    d = w_in.shape[0]
    c_mg = w_in.shape[1] - N_BRANCH * d
    c_nsa = 8 * BRANCH_W
    w_nsa = w_in[:, c_nsa:c_mg]
    w_nsa = jnp.pad(w_nsa, ((0, 0), (0, 3 * BRANCH_W - w_nsa.shape[1])))
    w = {
        "w_ret": w_in[:, 0:4 * BRANCH_W].astype(BF16), "w_u": w_in[:, 4 * BRANCH_W:5 * BRANCH_W].astype(BF16),
        "w_sb": w_in[:, 5 * BRANCH_W:8 * BRANCH_W].astype(BF16), "w_nsa": w_nsa.astype(BF16),
        "w_mg": w_in[:, c_mg:].astype(BF16), "w_branch": p["w_branch"][i].astype(BF16),
        "w_out": p["w_out"][i].astype(BF16), "ln1_g": p["ln1_g"][i][None, :], "ln1_b": p["ln1_b"][i][None, :],
        "w_up": p["w_ffn_up"][i].astype(BF16), "w_dn": p["w_ffn_down"][i].astype(BF16),
        "ln2_g": p["ln2_g"][i][None, :], "ln2_b": p["ln2_b"][i][None, :],
        "wc": _cmp_weights(p["w_cmp_k"][i], p["w_cmp_v"][i]),
    }
    s5 = _s5_params(p["ssm_a_re"][i], p["ssm_a_im"][i], p["ssm_b_re"][i], p["ssm_b_im"][i], p["ssm_c_re"][i],
                    p["ssm_c_im"][i], p["ssm_d"][i], p["ssm_log_dt"][i], p["w_glu"][i], p["b_glu"][i])
    return w, s5


def _ssm_rows(s):
    b = s.shape[0]
    return jnp.moveaxis(s, -1, 1).reshape(b, 2, -1)


def _ssm_state(rows, g):
    b = rows.shape[0]
    return jnp.moveaxis(rows.reshape(b, 2, g, -1), 1, -1)


def _layer(x, mod, tabs, w, s5, alpha, past):
    b, l, d = x.shape
    t = b * l
    x2 = x.reshape(t, d)
    g = s5["a_re"].shape[-1] // SSM_STATE
    if past is None:
        tm = min(l, 512)
        tpb = l // tm
        mod_in, tab_in = mod, tabs
    else:
        tm, tpb = t, None
        mod_in = jnp.repeat(mod.reshape(b, 8 * d), l, axis=0)
        tab_in = jnp.tile(tabs, (b, 1))
    ret, u, sbq, sbkv, sbkv16, nq, nqr, rows, win, ng, rowsc, rowss16, win16 = _in_call(
        x2, mod_in, tab_in, w, tpb, tm)

    if past is None:
        r0 = jnp.zeros((b, BRANCH_W, BRANCH_W), F32)
        x0 = jnp.zeros((b, 2, g * SSM_STATE), F32)
    else:
        r0 = _bd_from_heads(past["ret"])
        x0 = _ssm_rows(past["ssm"])
    out_a, r_bd = _ret_call(ret, b, l, r0)
    out_b, s_rows = _s5_call(u, b, l, x0, s5)

    if past is None:
        out_c = _sb_prompt_call(sbq, sbkv16, b, l)
        out_d = _nsa_prompt_call(nq, nqr, ng, rowsc, rowss16, win16, w["wc"], b, l)
        win_state = win.reshape(b, l, 2, HEAD_DIM)[:, l - min(WINDOW, l):]
    else:
        kvn_t = jnp.swapaxes(jnp.pad(sbkv16.reshape(b, l, 2 * BRANCH_W), ((0, 0), (0, TK - l), (0, 0))), 1, 2)
        out_c = _sb_decode_call(sbq.reshape(b, l, BRANCH_W), kvn_t, past["sb"], past["layer"], past["page_table"])
        rows3 = rows.reshape(b, l, BRANCH_W)
        rown_t = jnp.swapaxes(jnp.pad(rows3[:, :, LANES:], ((0, 0), (0, TK - l), (0, 0))), 1, 2)
        out_d, wst = _nsa_decode_call(nq.reshape(b, l, BRANCH_W), nqr.reshape(b, l, BRANCH_W),
                                      ng.reshape(b, l, LANES), rows3, rown_t,
                                      win.reshape(b, l, LANES), past["nsa"], past["win"], past["layer"],
                                      past["page_table"], w["wc"])
        out_c = out_c.reshape(t, BRANCH_W)
        out_d = out_d.reshape(t, BRANCH_W)
        win_state = wst.reshape(b, -1, 2, HEAD_DIM)

    tm2 = min(tm, 256)
    tpb2 = None if tpb is None else l // tm2
    x1 = _merge_call(x2, mod_in, [out_a, out_b, out_c, out_d], w, tpb2, tm2, alpha)
    xo = _ffn_call(x1, mod_in, w, tpb2, tm2, alpha)
    state = (sbkv.reshape(b, l, 2, N_HEADS, HEAD_DIM), rows.reshape(b, l, 4, HEAD_DIM), win_state,
             _heads_from_bd(r_bd), _ssm_state(s_rows, g))
    return xo.reshape(b, l, d), state


def kernel(x_prompt, x_sample, c_prompt, c_sample, cache_sb, cache_nsa, cache_win, state_ret, state_ssm, page_table, w_ada, b_ada, w_in, ssm_a_re, ssm_a_im, ssm_b_re, ssm_b_im, ssm_c_re, ssm_c_im, ssm_d, ssm_log_dt, w_glu, b_glu, w_cmp_k, w_cmp_v, w_branch, w_out, ln1_g, ln1_b, w_ffn_up, w_ffn_down, ln2_g, ln2_b):
    p = dict(w_in=w_in, ssm_a_re=ssm_a_re, ssm_a_im=ssm_a_im, ssm_b_re=ssm_b_re, ssm_b_im=ssm_b_im,
             ssm_c_re=ssm_c_re, ssm_c_im=ssm_c_im, ssm_d=ssm_d, ssm_log_dt=ssm_log_dt, w_glu=w_glu, b_glu=b_glu,
             w_cmp_k=w_cmp_k, w_cmp_v=w_cmp_v, w_branch=w_branch, w_out=w_out, ln1_g=ln1_g, ln1_b=ln1_b,
             w_ffn_up=w_ffn_up, w_ffn_down=w_ffn_down, ln2_g=ln2_g, ln2_b=ln2_b)
    depth, d = w_ada.shape[0], w_ada.shape[1]
    bp, lp, _ = x_prompt.shape
    bs, ls, _ = x_sample.shape
    n_pool, page = cache_sb.shape[1], cache_sb.shape[2]
    past_len = page_table.shape[1] * page
    alpha = (2.0 * depth) ** 0.25

    c_all = jnp.concatenate([c_prompt, c_sample], axis=0)
    m_rows = _round_up(bp + bs, SUBLANES)
    c_all = jnp.pad(c_all, ((0, m_rows - bp - bs), (0, 0)))
    mods = _ada_call(c_all, w_ada, b_ada)

    tabs_p = _rope_tables(jnp.arange(lp, dtype=jnp.int32))
    tabs_s = _rope_tables(past_len + jnp.arange(ls, dtype=jnp.int32))
    cache_sb4 = jnp.transpose(cache_sb, (0, 1, 3, 4, 5, 2)).reshape(depth, n_pool, 2 * BRANCH_W, page)
    cache_nsa4 = jnp.transpose(cache_nsa, (0, 1, 3, 4, 2)).reshape(depth, n_pool, BRANCH_W, page)
    cache_win4 = cache_win.reshape(depth, bs, cache_win.shape[2], LANES)

    xp, xs = x_prompt, x_sample
    st_p = [[] for _ in range(5)]
    st_s = [[] for _ in range(5)]
    for i in range(depth):
        w, s5 = _layer_weights(i, p)
        mod_i = jnp.pad(mods[i].reshape(m_rows, 6, d), ((0, 0), (0, 2), (0, 0)))
        xp, new_p = _layer(xp, mod_i[:bp], tabs_p, w, s5, alpha, None)
        past = dict(sb=cache_sb4, nsa=cache_nsa4, win=cache_win4, ret=state_ret[i], ssm=state_ssm[i],
                    layer=i, page_table=page_table)
        xs, new_s = _layer(xs, mod_i[bp:bp + bs], tabs_s, w, s5, alpha, past)
        for k in range(5):
            st_p[k].append(new_p[k])
            st_s[k].append(new_s[k])
    sb_p, nsa_p, win_p, ret_p, ssm_p = [jnp.stack(s, axis=0) for s in st_p]
    sb_s, nsa_s, win_s, ret_s, ssm_s = [jnp.stack(s, axis=0) for s in st_s]
    return (xp, xs, sb_p, sb_s, nsa_p, nsa_s, win_p, win_s, ret_p, ret_s, ssm_p, ssm_s)
```

```python
import functools
import math

import jax
import jax.numpy as jnp
import numpy as np
from jax import lax
from jax.experimental import pallas as pl
from jax.experimental.pallas import tpu as pltpu

F32 = jnp.float32
BF16 = jnp.bfloat16

LANES = 128
SUBLANES = 8
VMEM_LIMIT_BYTES = 56 * 1024 * 1024

HEAD_DIM = 64
N_HEADS = 4
BRANCH_W = N_HEADS * HEAD_DIM
N_BRANCH = 4
RET_THETA = 10000.0
RET_CHUNK = 128
SSM_GROUP_CH = 16
SSM_STATE = 64
CMP_BLOCK = 32
CMP_STRIDE = 16
SLC_BLOCK = 64
N_SEL = 16
WINDOW = 512
ROPE_THETA = 500000.0
ROPE_DIM = HEAD_DIM // 4
LN_EPS = 1e-5
QK_SCALE = HEAD_DIM ** -0.5
NEG_BIG = -1e30


def _cparams(*sem, flags=None):
    return pltpu.CompilerParams(dimension_semantics=tuple(sem), vmem_limit_bytes=VMEM_LIMIT_BYTES, flags=flags)


def _lane_iota(shape):
    return lax.broadcasted_iota(jnp.int32, shape, len(shape) - 1)


def _row_iota(shape):
    return lax.broadcasted_iota(jnp.int32, shape, len(shape) - 2)


def _dot(a, b):
    return jnp.dot(a.astype(BF16), b.astype(BF16), preferred_element_type=F32)


def _dot_nt(a, b):
    return lax.dot_general(a.astype(BF16), b.astype(BF16), (((1,), (1,)), ((), ())), preferred_element_type=F32)


def _dot_tn(a, b):
    return lax.dot_general(a.astype(BF16), b.astype(BF16), (((0,), (0,)), ((), ())), preferred_element_type=F32)


def _ln_rows(x):
    mu = jnp.mean(x, axis=-1, keepdims=True)
    xc = x - mu
    var = jnp.mean(xc * xc, axis=-1, keepdims=True)
    return xc * lax.rsqrt(var + LN_EPS)


def _sigmoid(x):
    return 1.0 / (1.0 + jnp.exp(-x))


def _silu(x):
    return x * _sigmoid(x)


def _head_mask(shape, h):
    lane = _lane_iota(shape)
    return (lane >= h * HEAD_DIM) & (lane < (h + 1) * HEAD_DIM)


def _ada_kernel(c_ref, w_ref, b_ref, o_ref):
    o_ref[0] = _dot(_silu(c_ref[...]), w_ref[0]) + b_ref[0]


def _ada_call(c_all, w_ada, b_ada):
    depth, d, n = w_ada.shape
    m = c_all.shape[0]
    tn = 1536
    return pl.pallas_call(
        _ada_kernel,
        grid=(depth, n // tn),
        in_specs=[pl.BlockSpec((m, d), lambda i, j: (0, 0)),
                  pl.BlockSpec((1, d, tn), lambda i, j: (i, 0, j)),
                  pl.BlockSpec((1, 1, tn), lambda i, j: (i, 0, j))],
        out_specs=pl.BlockSpec((1, m, tn), lambda i, j: (i, 0, j)),
        out_shape=jax.ShapeDtypeStruct((depth, m, n), F32),
        compiler_params=_cparams("parallel", "parallel"),
        name="ada_mod",
    )(c_all, w_ada, b_ada.reshape(depth, 1, n))


def _rope128(x, cos, sin_signed, half):
    first = (_lane_iota(x.shape) % HEAD_DIM) < half
    partner = jnp.where(first, pltpu.roll(x, LANES - half, 1), pltpu.roll(x, half, 1))
    return x * cos + partner * sin_signed


def _mod_rows(m_ref, row, per_token):
    if per_token:
        d = m_ref.shape[-1] // 8
        return m_ref[:, row * d:(row + 1) * d]
    return m_ref[0, row:row + 1, :]


def _in_kernel(per_token, x_ref, m_ref, wr_ref, wu_ref, wsb_ref, wn_ref, tab_ref,
               ret_ref, u_ref, sbq_ref, sbkv_ref, sbkv16_ref, nq_ref, nqr_ref, rows_ref, win_ref, ng_ref,
               rowsc_ref, rowss16_ref, win16_ref):
    x = x_ref[...]
    h = (_ln_rows(x) * (1.0 + _mod_rows(m_ref, 1, per_token)) + _mod_rows(m_ref, 0, per_token)).astype(BF16)
    tab = tab_ref[...]
    rc, rs = tab[:, 0:128], tab[:, 128:256]
    qc, qs = tab[:, 256:384], tab[:, 384:512]
    kc, ks = tab[:, 512:640], tab[:, 640:768]

    zr = _dot_nt(h, wr_ref[...])
    half_r = HEAD_DIM // 2
    for c in range(2):
        ret_ref[:, c * 128:(c + 1) * 128] = _rope128(zr[:, c * 128:(c + 1) * 128], rc, rs, half_r)
    for c in range(2, 4):
        ret_ref[:, c * 128:(c + 1) * 128] = _rope128(zr[:, c * 128:(c + 1) * 128], rc, rs, half_r) * QK_SCALE
    ret_ref[:, 512:1024] = zr[:, 512:1024]

    u_ref[...] = _dot_nt(h, wu_ref[...])

    zs = _dot_nt(h, wsb_ref[...])
    sbq_ref[...] = (zs[:, 0:256] * QK_SCALE).astype(BF16)
    sbkv_ref[...] = zs[:, 256:768]
    sbkv16_ref[...] = zs[:, 256:768].astype(BF16)

    zn = _dot_nt(h, wn_ref[...])
    half_n = ROPE_DIM // 2
    nq_ref[...] = (zn[:, 0:256] * QK_SCALE).astype(BF16)
    for c in range(2):
        nqr_ref[:, c * 128:(c + 1) * 128] = (
            _rope128(zn[:, c * 128:(c + 1) * 128], qc, qs, half_n) * QK_SCALE).astype(BF16)
    rows_slc = _rope128(zn[:, 384:512], kc, ks, half_n)
    rows_ref[:, 0:128] = zn[:, 256:384]
    rows_ref[:, 128:256] = rows_slc
    rowsc_ref[...] = zn[:, 256:384]
    rowss16_ref[...] = rows_slc.astype(BF16)
    win = _rope128(zn[:, 512:640], kc, ks, half_n)
    win_ref[...] = win
    win16_ref[...] = win.astype(BF16)
    ng_ref[...] = _sigmoid(zn[:, 640:768])


def _in_call(x2, mod, tabs, w, tiles_per_batch, tm):
    t, d = x2.shape
    per_token = tiles_per_batch is None
    nt = t // tm
    if per_token:
        mod_spec = pl.BlockSpec((tm, 8 * d), lambda i: (i, 0))
        tab_spec = pl.BlockSpec((tm, 768), lambda i: (i, 0))
    else:
        mod_spec = pl.BlockSpec((1, 8, d), lambda i: (i // tiles_per_batch, 0, 0))
        tab_spec = pl.BlockSpec((tm, 768), lambda i: (i % tiles_per_batch, 0))

    def wspec(a):
        return pl.BlockSpec(a.shape, lambda i: (0, 0))

    def ospec(n):
        return pl.BlockSpec((tm, n), lambda i: (i, 0))

    outs = [(1024, F32), (256, F32), (256, BF16), (512, F32), (512, BF16), (256, BF16), (256, BF16),
            (256, F32), (128, F32), (128, F32), (128, F32), (128, BF16), (128, BF16)]
    return pl.pallas_call(
        functools.partial(_in_kernel, per_token),
        grid=(nt,),
        in_specs=[pl.BlockSpec((tm, d), lambda i: (i, 0)), mod_spec,
                  wspec(w["w_ret"]), wspec(w["w_u"]), wspec(w["w_sb"]), wspec(w["w_nsa"]), tab_spec],
        out_specs=[ospec(n) for n, _ in outs],
        out_shape=[jax.ShapeDtypeStruct((t, n), dt) for n, dt in outs],
        compiler_params=_cparams("parallel"),
        name="in_proj",
    )(x2, mod, w["w_ret"], w["w_u"], w["w_sb"], w["w_nsa"], tabs)


def _merge_kernel(per_token, alpha, x_ref, m_ref, a_ref, b_ref, c_ref, d_ref, wmg_ref, wbr_ref, wout_ref,
                  g_ref, bias_ref, o_ref):
    x = x_ref[...]
    dm = x.shape[-1]
    h = (_ln_rows(x) * (1.0 + _mod_rows(m_ref, 1, per_token)) + _mod_rows(m_ref, 0, per_token)).astype(BF16)
    acc = jnp.zeros(x.shape, F32)
    for n, br_ref in enumerate((a_ref, b_ref, c_ref, d_ref)):
        mg = _dot_nt(h, wmg_ref[n * dm:(n + 1) * dm, :])
        br = jnp.dot(br_ref[...].astype(BF16), wbr_ref[n], preferred_element_type=F32)
        acc = acc + _sigmoid(mg) * br
    y = jnp.dot(acc.astype(BF16), wout_ref[...], preferred_element_type=F32)
    r = alpha * x + (1.0 + _mod_rows(m_ref, 2, per_token)) * y
    o_ref[...] = _ln_rows(r) * g_ref[...] + bias_ref[...]


def _merge_call(x2, mod, branches, w, tiles_per_batch, tm, alpha):
    t, d = x2.shape
    per_token = tiles_per_batch is None
    if per_token:
        mod_spec = pl.BlockSpec((tm, 8 * d), lambda i: (i, 0))
    else:
        mod_spec = pl.BlockSpec((1, 8, d), lambda i: (i // tiles_per_batch, 0, 0))
    row = lambda n: pl.BlockSpec((tm, n), lambda i: (i, 0))
    return pl.pallas_call(
        functools.partial(_merge_kernel, per_token, alpha),
        grid=(t // tm,),
        in_specs=[row(d), mod_spec, row(BRANCH_W), row(BRANCH_W), row(BRANCH_W), row(BRANCH_W),
                  pl.BlockSpec(w["w_mg"].shape, lambda i: (0, 0)),
                  pl.BlockSpec(w["w_branch"].shape, lambda i: (0, 0, 0)),
                  pl.BlockSpec(w["w_out"].shape, lambda i: (0, 0)),
                  pl.BlockSpec((1, d), lambda i: (0, 0)), pl.BlockSpec((1, d), lambda i: (0, 0))],
        out_specs=row(d),
        out_shape=jax.ShapeDtypeStruct((t, d), F32),
        compiler_params=_cparams("parallel"),
        name="merge_out",
    )(x2, mod, *branches, w["w_mg"], w["w_branch"], w["w_out"], w["ln1_g"], w["ln1_b"])


def _ffn_kernel(per_token, alpha, n_chunk, x_ref, m_ref, wup_ref, wdn_ref, g_ref, bias_ref, o_ref):
    x = x_ref[...]
    dff = wdn_ref.shape[0]
    ck = dff // n_chunk
    h = (_ln_rows(x) * (1.0 + _mod_rows(m_ref, 4, per_token)) + _mod_rows(m_ref, 3, per_token)).astype(BF16)
    f = jnp.zeros(x.shape, F32)
    for c in range(n_chunk):
        a = jnp.dot(h, wup_ref[:, c * ck:(c + 1) * ck], preferred_element_type=F32)
        u = jnp.dot(h, wup_ref[:, dff + c * ck:dff + (c + 1) * ck], preferred_element_type=F32)
        f = f + jnp.dot((_silu(a) * u).astype(BF16), wdn_ref[c * ck:(c + 1) * ck, :], preferred_element_type=F32)
    r = alpha * x + (1.0 + _mod_rows(m_ref, 5, per_token)) * f
    o_ref[...] = _ln_rows(r) * g_ref[...] + bias_ref[...]


def _ffn_call(x2, mod, w, tiles_per_batch, tm, alpha):
    t, d = x2.shape
    per_token = tiles_per_batch is None
    if per_token:
        mod_spec = pl.BlockSpec((tm, 8 * d), lambda i: (i, 0))
    else:
        mod_spec = pl.BlockSpec((1, 8, d), lambda i: (i // tiles_per_batch, 0, 0))
    dff = w["w_dn"].shape[0]
    n_chunk = 2 if dff % 256 == 0 else 1
    return pl.pallas_call(
        functools.partial(_ffn_kernel, per_token, alpha, n_chunk),
        grid=(t // tm,),
        in_specs=[pl.BlockSpec((tm, d), lambda i: (i, 0)), mod_spec,
                  pl.BlockSpec(w["w_up"].shape, lambda i: (0, 0)),
                  pl.BlockSpec(w["w_dn"].shape, lambda i: (0, 0)),
                  pl.BlockSpec((1, d), lambda i: (0, 0)), pl.BlockSpec((1, d), lambda i: (0, 0))],
        out_specs=pl.BlockSpec((tm, d), lambda i: (i, 0)),
        out_shape=jax.ShapeDtypeStruct((t, d), F32),
        compiler_params=_cparams("parallel"),
        name="ffn",
    )(x2, mod, w["w_up"], w["w_dn"], w["ln2_g"], w["ln2_b"])


def _ret_kernel(chunk, q_ref, k_ref, v_ref, g_ref, r0_ref, intra_ref, qdec_ref, kdec_ref, cdec_ref,
                o_ref, rout_ref, r_s):
    j = pl.program_id(1)

    @pl.when(j == 0)
    def _():
        r_s[...] = r0_ref[0]

    q = q_ref[...]
    k = k_ref[...]
    v16 = v_ref[...].astype(BF16)
    k16 = k.astype(BF16)
    o = _dot(q * qdec_ref[...], r_s[...])
    for h in range(N_HEADS):
        hm = _head_mask(q.shape, h)
        s = _dot_nt(jnp.where(hm, q, 0.0), k16) * intra_ref[h]
        o = o + jnp.where(hm, _dot(s, v16), 0.0)
    upd = _dot_tn(k * kdec_ref[...], v16)
    rr = _row_iota(upd.shape) // HEAD_DIM
    cc = _lane_iota(upd.shape) // HEAD_DIM
    r_new = r_s[...] * cdec_ref[...] + jnp.where(rr == cc, upd, 0.0)
    r_s[...] = r_new

    @pl.when(j == pl.num_programs(1) - 1)
    def _():
        rout_ref[0] = r_new

    mu = jnp.zeros(o.shape, F32)
    for h in range(N_HEADS):
        hm = _head_mask(o.shape, h)
        mu = mu + jnp.where(hm, jnp.sum(jnp.where(hm, o, 0.0), axis=-1, keepdims=True), 0.0)
    oc = o - mu * (1.0 / HEAD_DIM)
    var = jnp.zeros(o.shape, F32)
    oc2 = oc * oc
    for h in range(N_HEADS):
        hm = _head_mask(o.shape, h)
        var = var + jnp.where(hm, jnp.sum(jnp.where(hm, oc2, 0.0), axis=-1, keepdims=True), 0.0)
    o_ref[...] = oc * lax.rsqrt(var * (1.0 / HEAD_DIM) + LN_EPS) * _silu(g_ref[...])


def _ret_tables(chunk):
    h = jnp.arange(N_HEADS, dtype=F32)
    log_g = jnp.log1p(-jnp.exp2(-5.0 - h))
    idx = jnp.arange(chunk, dtype=F32)
    diff = idx[:, None] - idx[None, :]
    intra = jnp.where(diff >= 0, jnp.exp(jnp.maximum(diff, 0.0)[None] * log_g[:, None, None]), 0.0)
    q_dec = jnp.exp((idx + 1.0)[None, :] * log_g[:, None])
    k_dec = jnp.exp((chunk - 1.0 - idx)[None, :] * log_g[:, None])
    c_dec = jnp.exp(chunk * log_g)
    lanes = lambda t: jnp.repeat(t.T, HEAD_DIM, axis=1)
    return intra, lanes(q_dec), lanes(k_dec), jnp.repeat(c_dec, HEAD_DIM)[None, :]


def _ret_call(ret, b, l, r0_bd):
    chunk = RET_CHUNK if l % RET_CHUNK == 0 else l
    n = l // chunk
    intra, q_dec, k_dec, c_dec = _ret_tables(chunk)
    col = lambda c: pl.BlockSpec((chunk, BRANCH_W), lambda i, j: (i * n + j, c))
    full = lambda a: pl.BlockSpec(a.shape, lambda i, j: (0,) * a.ndim)
    return pl.pallas_call(
        functools.partial(_ret_kernel, chunk),
        grid=(b, n),
        in_specs=[col(0), col(1), col(2), col(3),
                  pl.BlockSpec((1, BRANCH_W, BRANCH_W), lambda i, j: (i, 0, 0)),
                  full(intra), full(q_dec), full(k_dec), full(c_dec)],
        out_specs=[pl.BlockSpec((chunk, BRANCH_W), lambda i, j: (i * n + j, 0)),
                   pl.BlockSpec((1, BRANCH_W, BRANCH_W), lambda i, j: (i, 0, 0))],
        out_shape=[jax.ShapeDtypeStruct((b * l, BRANCH_W), F32),
                   jax.ShapeDtypeStruct((b, BRANCH_W, BRANCH_W), F32)],
        scratch_shapes=[pltpu.VMEM((BRANCH_W, BRANCH_W), F32)],
        compiler_params=_cparams("parallel", "arbitrary"),
        name="retention",
    )(ret, ret, ret, ret, r0_bd, intra, q_dec, k_dec, c_dec)


def _bd_from_heads(r):
    b = r.shape[0]
    eye = jnp.eye(N_HEADS, dtype=r.dtype)
    return jnp.einsum("bhij,hg->bhigj", r, eye).reshape(b, BRANCH_W, BRANCH_W)


def _heads_from_bd(r_bd):
    b = r_bd.shape[0]
    r5 = r_bd.reshape(b, N_HEADS, HEAD_DIM, N_HEADS, HEAD_DIM)
    return jnp.stack([r5[:, h, :, h, :] for h in range(N_HEADS)], axis=1)


def _cmul(ar, ai, br, bi):
    return ar * br - ai * bi, ar * bi + ai * br


def _gelu_tanh(x):
    return 0.5 * x * (1.0 + jnp.tanh(math.sqrt(2.0 / math.pi) * (x + 0.044715 * (x * x * x))))


def _s5_kernel(chain, u_ref, x0_ref, are_ref, aim_ref, ldt_ref, bre_ref, bim_ref, cre_ref, cim_ref, dsk_ref,
               wglu_ref, bglu_ref, o_ref, st_ref, xr_s, xi_s, cr_s, ci_s):
    rows = u_ref.shape[0]
    n_grp = rows // SUBLANES
    j = pl.program_id(1)

    ar, ai = are_ref[...], aim_ref[...]
    dt = jnp.exp(ldt_ref[...])
    mag = jnp.exp(ar * dt)
    abr, abi = mag * jnp.cos(ai * dt), mag * jnp.sin(ai * dt)
    nr, ni = abr - 1.0, abi
    den = ar * ar + ai * ai
    fr = (nr * ar + ni * ai) / den
    fi = (ni * ar - nr * ai) / den
    bbr = fr * bre_ref[...] - fi * bim_ref[...]
    bbi = fr * bim_ref[...] + fi * bre_ref[...]

    u = u_ref[...]
    u16 = u.astype(BF16)
    xr_s[...] = jnp.dot(u16, bbr.astype(BF16), preferred_element_type=F32)
    xi_s[...] = jnp.dot(u16, bbi.astype(BF16), preferred_element_type=F32)

    a2r, a2i = _cmul(abr, abi, abr, abi)
    a4r, a4i = _cmul(a2r, a2i, a2r, a2i)
    row = _row_iota((SUBLANES, abr.shape[-1]))
    pr, pi_ = jnp.broadcast_to(abr, row.shape), jnp.broadcast_to(abi, row.shape)
    qr, qi = abr, abi
    for i in range(1, SUBLANES):
        qr, qi = _cmul(qr, qi, abr, abi)
        pr = jnp.where(row == i, qr, pr)
        pi_ = jnp.where(row == i, qi, pi_)
    steps = ((1, abr, abi), (2, a2r, a2i), (4, a4r, a4i))

    if chain:
        @pl.when(j == 0)
        def _():
            cr_s[...] = x0_ref[0, 0:1, :]
            ci_s[...] = x0_ref[0, 1:2, :]

    def group(g, carry):
        r0 = pl.multiple_of(g * SUBLANES, SUBLANES)
        xr = xr_s[pl.ds(r0, SUBLANES), :]
        xi = xi_s[pl.ds(r0, SUBLANES), :]
        for d, er, ei in steps:
            sr = jnp.where(row >= d, pltpu.roll(xr, d, 0), 0.0)
            si = jnp.where(row >= d, pltpu.roll(xi, d, 0), 0.0)
            tr, ti = _cmul(er, ei, sr, si)
            xr, xi = xr + tr, xi + ti
        if chain:
            c_r, c_i = cr_s[...], ci_s[...]
        else:
            c_r, c_i = x0_ref[g, 0:1, :], x0_ref[g, 1:2, :]
        tr, ti = _cmul(pr, pi_, c_r, c_i)
        xr, xi = xr + tr, xi + ti
        xr_s[pl.ds(r0, SUBLANES), :] = xr
        xi_s[pl.ds(r0, SUBLANES), :] = xi
        if chain:
            cr_s[...] = xr[SUBLANES - 1:SUBLANES, :]
            ci_s[...] = xi[SUBLANES - 1:SUBLANES, :]
        else:
            st_ref[g, 0:1, :] = xr[SUBLANES - 1:SUBLANES, :]
            st_ref[g, 1:2, :] = xi[SUBLANES - 1:SUBLANES, :]
        return carry

    lax.fori_loop(0, n_grp, group, 0)

    if chain:
        @pl.when(j == pl.num_programs(1) - 1)
        def _():
            st_ref[0, 0:1, :] = cr_s[...]
            st_ref[0, 1:2, :] = ci_s[...]

    y = (jnp.dot(xr_s[...].astype(BF16), cre_ref[...], preferred_element_type=F32)
         - jnp.dot(xi_s[...].astype(BF16), cim_ref[...], preferred_element_type=F32) + dsk_ref[...] * u)
    zb = _gelu_tanh(y)
    o_ref[...] = zb * _sigmoid(jnp.dot(zb.astype(BF16), wglu_ref[...], preferred_element_type=F32) + bglu_ref[...])


def _s5_call(u2, b, l, x0, w):
    n_state = x0.shape[-1]
    chain = l % 128 == 0
    if chain:
        tl = min(l, 512)
        grid = (b, l // tl)
        rows = tl
        u_spec = pl.BlockSpec((tl, BRANCH_W), lambda i, j: (i * (l // tl) + j, 0))
        x0_spec = pl.BlockSpec((1, 2, n_state), lambda i, j: (i, 0, 0))
    else:
        assert l == SUBLANES
        grid = (1, 1)
        rows = b * l
        u_spec = pl.BlockSpec((rows, BRANCH_W), lambda i, j: (0, 0))
        x0_spec = pl.BlockSpec((b, 2, n_state), lambda i, j: (0, 0, 0))
    full = lambda a: pl.BlockSpec(a.shape, lambda i, j: (0,) * a.ndim)
    params = [w["a_re"], w["a_im"], w["log_dt"], w["b_re_bd"], w["b_im_bd"], w["c_re_bd"], w["c_im_bd"],
              w["d_skip"], w["w_glu"], w["b_glu"]]
    return pl.pallas_call(
        functools.partial(_s5_kernel, chain),
        grid=grid,
        in_specs=[u_spec, x0_spec] + [full(a) for a in params],
        out_specs=[u_spec, x0_spec],
        out_shape=[jax.ShapeDtypeStruct((b * l, BRANCH_W), F32), jax.ShapeDtypeStruct(x0.shape, F32)],
        scratch_shapes=[pltpu.VMEM((rows, n_state), F32), pltpu.VMEM((rows, n_state), F32),
                        pltpu.VMEM((1, n_state), F32), pltpu.VMEM((1, n_state), F32)],
        compiler_params=_cparams("parallel", "arbitrary"),
        name="s5",
    )(u2, x0, *params)


def _s5_params(a_re, a_im, b_re, b_im, c_re, c_im, d_skip, log_dt, w_glu, b_glu):
    g, p = a_re.shape
    cg = b_re.shape[-1]
    eye = jnp.eye(g, dtype=F32)
    b_bd = lambda t: jnp.einsum("gpc,gh->gchp", t, eye).reshape(g * cg, g * p)
    c_bd = lambda t: jnp.einsum("gcp,gh->gphc", t, eye).reshape(g * p, g * cg)
    return {
        "a_re": a_re.reshape(1, g * p), "a_im": a_im.reshape(1, g * p),
        "log_dt": jnp.repeat(log_dt, p).reshape(1, g * p),
        "b_re_bd": b_bd(b_re), "b_im_bd": b_bd(b_im),
        "c_re_bd": c_bd(c_re).astype(BF16), "c_im_bd": c_bd(c_im).astype(BF16),
        "d_skip": d_skip.reshape(1, g * cg), "w_glu": w_glu.astype(BF16), "b_glu": b_glu.reshape(1, -1),
    }


TK = 128


def _stack_heads_bd(q):
    qf = q.astype(F32)
    return jnp.concatenate([jnp.where(_head_mask(qf.shape, h), qf, 0.0) for h in range(N_HEADS)], axis=0).astype(BF16)


def _unstack_heads_bd(acc, tq):
    out = jnp.zeros((tq, acc.shape[-1]), F32)
    for h in range(N_HEADS):
        blk = acc[h * tq:(h + 1) * tq]
        out = out + jnp.where(_head_mask(blk.shape, h), blk, 0.0)
    return out


def _suffix_matrix(tk):
    r = _row_iota((2 * tk, tk)) % tk
    c = _lane_iota((2 * tk, tk))
    return jnp.where(r > c, 1.0, 0.0).astype(BF16)


def _sb_scores(qbd, k16, transposed):
    return jnp.dot(qbd, k16, preferred_element_type=F32) if transposed else _dot_nt(qbd, k16)


def _sb_tile(qbd, k16, v16, mask, u2, carry, acc, transposed):
    return _sb_tile_z(_sb_scores(qbd, k16, transposed), v16, mask, u2, carry, acc, transposed)


def _sb_logw(z, mask, u2):
    log_beta = jnp.minimum(z, 0.0) - jnp.log(1.0 + jnp.exp(-jnp.abs(z)))
    l1m = log_beta - z
    if mask is not None:
        l1m = jnp.where(mask, l1m, 0.0)
    hi = lax.bitcast_convert_type(lax.bitcast_convert_type(l1m, jnp.uint32) & jnp.uint32(0xFFFF0000), F32)
    lo = l1m - hi
    suffix = jnp.dot(jnp.concatenate([hi.astype(BF16), lo.astype(BF16)], axis=1), u2, preferred_element_type=F32)
    return log_beta + suffix, suffix[:, 0:1] + l1m[:, 0:1]


def _sb_accumulate(logw, total, v16, mask, carry, acc, transposed):
    w = jnp.exp(logw + carry)
    if mask is not None:
        w = jnp.where(mask, w, 0.0)
    w16 = w.astype(BF16)
    acc = acc + (_dot_nt(w16, v16) if transposed else jnp.dot(w16, v16, preferred_element_type=F32))
    return carry + total, acc


def _sb_tile_z(z, v16, mask, u2, carry, acc, transposed):
    logw, total = _sb_logw(z, mask, u2)
    return _sb_accumulate(logw, total, v16, mask, carry, acc, transposed)


TKP = 256
SB_DEAD_LOG = -104.0


def _sb_prompt_kernel(q_ref, kv_ref, o_ref):
    j = pl.program_id(1)
    tq = q_ref.shape[0]
    qbd = _stack_heads_bd(q_ref[...])
    u2 = _suffix_matrix(TKP)
    rows = N_HEADS * tq
    n_full = (j * tq) // TKP
    qpos = j * tq + _row_iota((rows, TKP)) % tq
    diag_mask = (n_full * TKP + _lane_iota((rows, TKP))) < qpos

    def keys(kt):
        return kv_ref[pl.ds(pl.multiple_of(kt * TKP, TKP), TKP), 0:BRANCH_W]

    def values(kt):
        return kv_ref[pl.ds(pl.multiple_of(kt * TKP, TKP), TKP), BRANCH_W:2 * BRANCH_W]

    carry, acc = _sb_tile(qbd, keys(n_full), values(n_full), diag_mask, u2,
                          jnp.zeros((rows, 1), F32), jnp.zeros((rows, BRANCH_W), F32), False)

    def cond(st):
        it, carry, _ = st
        return (it < n_full) & (jnp.max(carry) >= SB_DEAD_LOG)

    def body(st):
        it, carry, acc = st
        kt = n_full - 1 - it
        carry, acc = _sb_tile(qbd, keys(kt), values(kt), None, u2, carry, acc, False)
        return it + 1, carry, acc

    _, carry, acc = lax.while_loop(cond, body, (jnp.int32(0), carry, acc))
    o_ref[...] = _unstack_heads_bd(acc, tq)


def _sb_prompt_call(sbq, sbkv16, b, l):
    tq = TK
    n = l // tq
    assert l % TKP == 0
    return pl.pallas_call(
        _sb_prompt_kernel,
        grid=(b, n),
        in_specs=[pl.BlockSpec((tq, BRANCH_W), lambda i, j: (i * n + j, 0)),
                  pl.BlockSpec((l, 2 * BRANCH_W), lambda i, j: (i, 0))],
        out_specs=pl.BlockSpec((tq, BRANCH_W), lambda i, j: (i * n + j, 0)),
        out_shape=jax.ShapeDtypeStruct((b * l, BRANCH_W), F32),
        compiler_params=_cparams("parallel", "arbitrary"),
        name="sb_prompt",
    )(sbq, sbkv16)


PAGES_PER_STEP = 8


def _sb_decode_kernel(n_pg, pt_ref, q_ref, kvn_ref, *rest):
    page_refs = rest[:n_pg]
    o_ref, carry_s, acc_s = rest[n_pg:]
    s = pl.program_id(1)
    tq = q_ref.shape[1]
    rows = N_HEADS * tq
    qbd = _stack_heads_bd(q_ref[0])
    u2 = _suffix_matrix(TK)

    @pl.when(s == 0)
    def _():
        mask = _lane_iota((rows, TK)) < (_row_iota((rows, TK)) % tq)
        kvn = kvn_ref[0]
        carry, acc = _sb_tile(qbd, kvn[0:BRANCH_W, :], kvn[BRANCH_W:, :], mask, u2,
                              jnp.zeros((rows, 1), F32), jnp.zeros((rows, BRANCH_W), F32), True)
        carry_s[...] = carry
        acc_s[...] = acc

    @pl.when(jnp.max(carry_s[...]) >= SB_DEAD_LOG)
    def _():
        carry, acc = carry_s[...], acc_s[...]
        group = 2 if n_pg % 2 == 0 else 1
        u2g = _suffix_matrix(group * TK) if group > 1 else u2
        for k in range(0, n_pg, group):
            kv = jnp.concatenate([page_refs[k + g][0, 0] for g in reversed(range(group))], axis=1).astype(BF16)
            carry, acc = _sb_tile(qbd, kv[0:BRANCH_W, :], kv[BRANCH_W:, :], None, u2g, carry, acc, True)
        carry_s[...] = carry
        acc_s[...] = acc

    @pl.when(s == pl.num_programs(1) - 1)
    def _():
        o_ref[0] = _unstack_heads_bd(acc_s[...], tq)


def _sb_decode_call(sbq3, kvn_t16, cache_t, layer, page_table):
    b, tq, _ = sbq3.shape
    n_pages = page_table.shape[1]
    n_pg = math.gcd(PAGES_PER_STEP, n_pages)
    n_steps = n_pages // n_pg

    def page_spec(k):
        return pl.BlockSpec((1, 1, 2 * BRANCH_W, TK),
                            lambda i, s, pt: (layer, pt[i, n_pages - 1 - (s * n_pg + k)], 0, 0))

    grid_spec = pltpu.PrefetchScalarGridSpec(
        num_scalar_prefetch=1,
        grid=(b, n_steps),
        in_specs=[pl.BlockSpec((1, tq, BRANCH_W), lambda i, s, pt: (i, 0, 0)),
                  pl.BlockSpec((1, 2 * BRANCH_W, TK), lambda i, s, pt: (i, 0, 0))]
                 + [page_spec(k) for k in range(n_pg)],
        out_specs=pl.BlockSpec((1, tq, BRANCH_W), lambda i, s, pt: (i, 0, 0)),
        scratch_shapes=[pltpu.VMEM((N_HEADS * tq, 1), F32), pltpu.VMEM((N_HEADS * tq, BRANCH_W), F32)],
    )
    return pl.pallas_call(
        functools.partial(_sb_decode_kernel, n_pg),
        grid_spec=grid_spec,
        out_shape=jax.ShapeDtypeStruct((b, tq, BRANCH_W), F32),
        compiler_params=_cparams("parallel", "arbitrary"),
        name="sb_decode",
    )(page_table, sbq3, kvn_t16, *([cache_t] * n_pg))


def _round_up(x, m):
    return -(-x // m) * m


def _stack_heads_128(q):
    qf = q.astype(F32)
    tq = qf.shape[0]
    low = _lane_iota((tq, LANES)) < HEAD_DIM
    parts = []
    for h in range(N_HEADS):
        half = qf[:, LANES * (h // 2):LANES * (h // 2 + 1)]
        if h % 2 == 1:
            half = pltpu.roll(half, HEAD_DIM, 1)
        parts.append(jnp.where(low, half, 0.0))
    return jnp.concatenate(parts, axis=0)


def _compress(rows_ref, wc_ref, n):
    p0 = jnp.zeros((n, LANES), F32)
    p1 = jnp.zeros((n, LANES), F32)
    for s in range(CMP_STRIDE):
        x = rows_ref[pl.ds(s, n, stride=CMP_STRIDE), :].astype(BF16)
        p0 = p0 + jnp.dot(x, wc_ref[s], preferred_element_type=F32)
        p1 = p1 + jnp.dot(x, wc_ref[CMP_STRIDE + s], preferred_element_type=F32)
    return p0 + pltpu.roll(p1, n - 1, 0)


def _cmp_select(qp16, kvc, qpos_rows, qpos_q, tq, nsp, n_top):
    n = kvc.shape[0]
    kvc16 = kvc.astype(BF16)
    s = _dot_nt(qp16, kvc16)
    cmask = (_lane_iota(s.shape) * CMP_STRIDE + (CMP_BLOCK - 1)) <= qpos_rows
    s = jnp.where(cmask, s, NEG_BIG)
    m = jnp.max(s, axis=-1, keepdims=True)
    m = jnp.where(m > 0.5 * NEG_BIG, m, 0.0)
    e = jnp.where(cmask, jnp.exp(s - m), 0.0)
    p16 = (e / jnp.maximum(jnp.sum(e, axis=-1, keepdims=True), 1e-30)).astype(BF16)
    o_cmp = jnp.dot(p16, kvc16, preferred_element_type=F32)
    ci = _row_iota((n, nsp)) * CMP_STRIDE
    sj = _lane_iota((n, nsp)) * SLC_BLOCK
    c2s = jnp.where((ci < sj + SLC_BLOCK) & (sj < ci + CMP_BLOCK), 1.0, 0.0).astype(BF16)
    imp4 = jnp.dot(p16, c2s, preferred_element_type=F32)
    imp = imp4[0:tq]
    for h in range(1, N_HEADS):
        imp = imp + imp4[h * tq:(h + 1) * tq]

    blk = _lane_iota((tq, nsp))
    blkf = blk.astype(F32)
    cur = qpos_q // SLC_BLOCK
    forced = (blk == 0) | (blk == cur) | (blk == cur - 1)
    val = jnp.where(forced, jnp.inf, imp)
    avail = jnp.where(blk * SLC_BLOCK <= qpos_q, 1.0, 0.0)
    sel = jnp.zeros((tq, nsp), F32)
    for _ in range(n_top):
        vm = jnp.where(avail > 0.5, val, -jnp.inf)
        mx = jnp.max(vm, axis=-1, keepdims=True)
        cand = (avail > 0.5) & (vm == mx)
        first = jnp.min(jnp.where(cand, blkf, float(nsp)), axis=-1, keepdims=True)
        pick = blkf == first
        sel = jnp.where(pick, 1.0, sel)
        avail = jnp.where(pick, 0.0, avail)
    return o_cmp, sel.astype(BF16)


def _attn_tile(q16, kv16, mask, m, l, acc, transposed=False):
    s = jnp.dot(q16, kv16, preferred_element_type=F32) if transposed else _dot_nt(q16, kv16)
    s = jnp.where(mask, s, NEG_BIG)
    m_new = jnp.maximum(m, jnp.max(s, axis=-1, keepdims=True))
    p = jnp.where(mask, jnp.exp(s - m_new), 0.0)
    alpha = jnp.exp(m - m_new)
    l = alpha * l + jnp.sum(p, axis=-1, keepdims=True)
    p16 = p.astype(BF16)
    acc = alpha * acc + (_dot_nt(p16, kv16) if transposed else jnp.dot(p16, kv16, preferred_element_type=F32))
    return m_new, l, acc


def _attn_update(s, kv16, m, l, acc):
    m_new = jnp.maximum(m, jnp.max(s, axis=-1, keepdims=True))
    p = jnp.exp(s - m_new)
    alpha = jnp.exp(m - m_new)
    l = alpha * l + jnp.sum(p, axis=-1, keepdims=True)
    acc = alpha * acc + jnp.dot(p.astype(BF16), kv16, preferred_element_type=F32)
    return m_new, l, acc


def _attn_init(rows, width):
    return jnp.full((rows, 1), NEG_BIG, F32), jnp.zeros((rows, 1), F32), jnp.zeros((rows, width), F32)


def _slc_tile(kt, carry, qr16, kv16, selb, qpos_rows, nsp, tk, transposed):
    er = _row_iota((nsp, tk))
    ec = _lane_iota((nsp, tk))
    expand = jnp.where(er == (tk // SLC_BLOCK) * kt + ec // SLC_BLOCK, 1.0, 0.0).astype(BF16)
    seltok = jnp.dot(selb, expand, preferred_element_type=F32)
    seltok4 = jnp.concatenate([seltok] * N_HEADS, axis=0)
    tok = kt * tk + _lane_iota(seltok4.shape)
    mask = (seltok4 > 0.5) & (tok <= qpos_rows)
    return _attn_tile(qr16, kv16, mask, *carry, transposed=transposed)


def _win_tile(kv16, kpos, carry, qw16, qpos_rows):
    dist = qpos_rows - kpos
    mask = (dist >= 0) & (dist < WINDOW)
    return _attn_tile(qw16, kv16, mask, *carry)


def _nsa_combine(g, o_c, o_s, o_w, tq):
    comb = []
    for h in range(N_HEADS):
        sl = slice(h * tq, (h + 1) * tq)
        comb.append(g[:, h:h + 1] * o_c[sl] + g[:, N_HEADS + h:N_HEADS + h + 1] * o_s[sl]
                    + g[:, 2 * N_HEADS + h:2 * N_HEADS + h + 1] * o_w[sl])
    low = _lane_iota((tq, LANES)) < HEAD_DIM
    out01 = jnp.where(low, pltpu.roll(comb[0], HEAD_DIM, 1), comb[1])
    out23 = jnp.where(low, pltpu.roll(comb[2], HEAD_DIM, 1), comb[3])
    return jnp.concatenate([out01, out23], axis=1)


def _nsa_prompt_kernel(n_top, nq_ref, nqr_ref, ng_ref, rowsc_ref, rows_ref, win_ref, onehot_ref, wc_ref, o_ref, kvc_s):
    j = pl.program_id(1)
    tq = nq_ref.shape[0]
    l_rows = rows_ref.shape[0]
    nsp = _round_up(l_rows // SLC_BLOCK, LANES)

    @pl.when(j == 0)
    def _():
        kvc_s[...] = _compress(rowsc_ref, wc_ref, kvc_s.shape[0])

    rows4 = N_HEADS * tq
    qpos_rows = j * tq + lax.broadcasted_iota(jnp.int32, (rows4, 1), 0) % tq
    qpos_q = j * tq + lax.broadcasted_iota(jnp.int32, (tq, 1), 0)
    qp16 = _stack_heads_128(nq_ref[...]).astype(BF16)
    qw = _stack_heads_128(nqr_ref[...])
    qw16 = qw.astype(BF16)

    o_c, selb = _cmp_select(qp16, kvc_s[...], qpos_rows, qpos_q, tq, nsp, n_top)
    last = (j * tq) // TKP

    sel_bias = (selb.astype(F32) - 1.0) * (-NEG_BIG)
    q_aug = jnp.concatenate([qw, jnp.concatenate([sel_bias] * N_HEADS, axis=0)], axis=1).astype(BF16)

    def slc_scores(kt):
        r0 = pl.multiple_of(kt * TKP, TKP)
        return _dot_nt(q_aug, jnp.concatenate([rows_ref[pl.ds(r0, TKP), :], onehot_ref[pl.ds(r0, TKP), :]], axis=1))

    def slc_values(kt):
        return rows_ref[pl.ds(pl.multiple_of(kt * TKP, TKP), TKP), :]

    def slc_body(kt, st):
        m, l, acc, s = st
        s_next = slc_scores(jnp.minimum(kt + 1, last))
        m, l, acc = _attn_update(s, slc_values(kt), m, l, acc)
        return m, l, acc, s_next

    m_s, l_s, acc_s, s_diag = lax.fori_loop(0, last, slc_body, (*_attn_init(rows4, LANES), slc_scores(0)))
    tok = last * TKP + _lane_iota((rows4, TKP))
    s_diag = jnp.where(tok <= qpos_rows, s_diag, NEG_BIG)
    _, l_s, acc_s = _attn_update(s_diag, slc_values(last), m_s, l_s, acc_s)
    o_s = acc_s / l_s

    def win_body(kt, c):
        r0 = pl.multiple_of(kt * TKP, TKP)
        kpos = kt * TKP + _lane_iota((rows4, TKP))
        return _win_tile(win_ref[pl.ds(r0, TKP), :], kpos, c, qw16, qpos_rows)

    first = jnp.maximum(j * tq - WINDOW, 0) // TKP
    _, l_w, acc_w = lax.fori_loop(first, last + 1, win_body, _attn_init(rows4, LANES))
    o_ref[...] = _nsa_combine(ng_ref[...], o_c, o_s, acc_w / l_w, tq)


def _nsa_prompt_call(nq, nqr, ng, rowsc, rowss16, win16, wc, b, l):
    tq = TK
    n = l // tq
    n_top = min(N_SEL, l // SLC_BLOCK)
    nsp = _round_up(l // SLC_BLOCK, LANES)
    onehot = (jnp.arange(l)[:, None] // SLC_BLOCK == jnp.arange(nsp)[None, :]).astype(BF16)
    qspec = lambda w: pl.BlockSpec((tq, w), lambda i, j: (i * n + j, 0))
    return pl.pallas_call(
        functools.partial(_nsa_prompt_kernel, n_top),
        grid=(b, n),
        in_specs=[qspec(BRANCH_W), qspec(BRANCH_W), qspec(LANES),
                  pl.BlockSpec((l, LANES), lambda i, j: (i, 0)),
                  pl.BlockSpec((l, LANES), lambda i, j: (i, 0)),
                  pl.BlockSpec((l, LANES), lambda i, j: (i, 0)),
                  pl.BlockSpec(onehot.shape, lambda i, j: (0, 0)),
                  pl.BlockSpec(wc.shape, lambda i, j: (0, 0, 0))],
        out_specs=qspec(BRANCH_W),
        out_shape=jax.ShapeDtypeStruct((b * l, BRANCH_W), F32),
        scratch_shapes=[pltpu.VMEM((l // CMP_STRIDE, LANES), F32)],
        compiler_params=_cparams("parallel", "arbitrary"),
        name="nsa_prompt",
    )(nq, nqr, ng, rowsc, rowss16, win16, onehot, wc)


def _nsa_decode_kernel(n_pg, past_len, n_cmp, n_top, pt_ref, nq_ref, nqr_ref, ng_ref, rown_ref, rownt_ref, cwin_ref,
                       winn_ref, wc_ref, *rest):
    page_refs = rest[:n_pg]
    o_ref, wst_ref, rowsc_s, rowst_s, win_s = rest[n_pg:]
    s = pl.program_id(1)
    tq = nq_ref.shape[1]
    n_steps = rowst_s.shape[0]
    slab = n_pg * TK
    for k, pg in enumerate(page_refs):
        r0 = pl.multiple_of((s * n_pg + k) * TK, TK)
        rowsc_s[pl.ds(r0, TK), :] = pg[0, 0, 0:LANES, :].T
        rowst_s[s, :, k * TK:(k + 1) * TK] = pg[0, 0, LANES:2 * LANES, :]

    @pl.when(s == pl.num_programs(1) - 1)
    def _():
        t_rows = rowsc_s.shape[0]
        w_rows = win_s.shape[0]
        wb = cwin_ref.shape[2]
        nsp = _round_up(_round_up(past_len + tq, SLC_BLOCK) // SLC_BLOCK, LANES)
        rowsc_s[past_len:past_len + tq, :] = rown_ref[0, :, 0:LANES]
        rowsc_s[past_len + tq:t_rows, :] = jnp.zeros((t_rows - past_len - tq, LANES), F32)
        win_s[0:wb, :] = cwin_ref[0, 0]
        win_s[wb:wb + tq, :] = winn_ref[0]
        win_s[wb + tq:w_rows, :] = jnp.zeros((w_rows - wb - tq, LANES), F32)
        wst_ref[0] = win_s[tq:wb + tq, :]

        rows4 = N_HEADS * tq
        qpos_rows = past_len + lax.broadcasted_iota(jnp.int32, (rows4, 1), 0) % tq
        qpos_q = past_len + lax.broadcasted_iota(jnp.int32, (tq, 1), 0)
        qp16 = _stack_heads_128(nq_ref[0]).astype(BF16)
        qw16 = _stack_heads_128(nqr_ref[0]).astype(BF16)

        kvc = _compress(rowsc_s, wc_ref, n_cmp)
        o_c, selb = _cmp_select(qp16, kvc, qpos_rows, qpos_q, tq, nsp, n_top)

        carry = lax.fori_loop(
            0, n_steps,
            lambda kt, c: _slc_tile(kt, c, qw16, rowst_s[kt].astype(BF16), selb, qpos_rows, nsp, slab, True),
            _attn_init(rows4, LANES))
        _, l_s, acc_s = _slc_tile(past_len // TK, carry, qw16, rownt_ref[0].astype(BF16), selb, qpos_rows, nsp, TK,
                                  True)
        o_s = acc_s / l_s

        carry = _attn_init(rows4, LANES)
        for kt in range(w_rows // TK):
            kpos = past_len - wb + kt * TK + _lane_iota((rows4, TK))
            carry = _win_tile(win_s[kt * TK:(kt + 1) * TK, :].astype(BF16), kpos, carry, qw16, qpos_rows)
        _, l_w, acc_w = carry
        o_ref[0] = _nsa_combine(ng_ref[0], o_c, o_s, acc_w / l_w, tq)


def _nsa_decode_call(nq3, nqr3, ng3, rown3, rownt3, winn3, cache_nsa_t, cache_win4, layer, page_table, wc):
    b, tq, _ = nq3.shape
    n_pages = page_table.shape[1]
    past_len = n_pages * TK
    wb = cache_win4.shape[2]
    n_pg = math.gcd(PAGES_PER_STEP, n_pages)
    n_steps = n_pages // n_pg
    t_pad = _round_up(past_len + tq, SLC_BLOCK)
    n_cmp = _round_up(t_pad // CMP_STRIDE, SUBLANES)
    t_rows = _round_up(n_cmp * CMP_STRIDE, TK)
    assert t_rows == past_len + TK
    w_rows = _round_up(wb + tq, TK)
    n_top = min(N_SEL, t_pad // SLC_BLOCK)

    def page_spec(k):
        return pl.BlockSpec((1, 1, BRANCH_W, TK), lambda i, s, pt: (layer, pt[i, s * n_pg + k], 0, 0))

    row3 = lambda r, w: pl.BlockSpec((1, r, w), lambda i, s, pt: (i, 0, 0))
    grid_spec = pltpu.PrefetchScalarGridSpec(
        num_scalar_prefetch=1,
        grid=(b, n_steps),
        in_specs=[row3(tq, BRANCH_W), row3(tq, BRANCH_W), row3(tq, LANES), row3(tq, BRANCH_W), row3(LANES, TK),
                  pl.BlockSpec((1, 1, wb, LANES), lambda i, s, pt: (layer, i, 0, 0)),
                  row3(tq, LANES),
                  pl.BlockSpec(wc.shape, lambda i, s, pt: (0, 0, 0))]
                 + [page_spec(k) for k in range(n_pg)],
        out_specs=[row3(tq, BRANCH_W), row3(wb, LANES)],
        scratch_shapes=[pltpu.VMEM((t_rows, LANES), F32), pltpu.VMEM((n_steps, LANES, n_pg * TK), F32),
                        pltpu.VMEM((w_rows, LANES), F32)],
    )
    return pl.pallas_call(
        functools.partial(_nsa_decode_kernel, n_pg, past_len, n_cmp, n_top),
        grid_spec=grid_spec,
        out_shape=[jax.ShapeDtypeStruct((b, tq, BRANCH_W), F32), jax.ShapeDtypeStruct((b, wb, LANES), F32)],
        compiler_params=_cparams("parallel", "arbitrary"),
        name="nsa_decode",
    )(page_table, nq3, nqr3, ng3, rown3, rownt3, cache_win4, winn3, wc, *([cache_nsa_t] * n_pg))


def _cmp_weights(w_cmp_k, w_cmp_v):
    z = jnp.zeros_like(w_cmp_k)
    top = jnp.concatenate([w_cmp_k, z], axis=2)
    bot = jnp.concatenate([z, w_cmp_v], axis=2)
    return jnp.concatenate([top, bot], axis=1).astype(BF16)


def _rope_tables(pos):
    posf = pos.astype(F32)[:, None]
    jj = np.arange(LANES) % HEAD_DIM

    def table(half, theta, lanes_on):
        inv = theta ** (-jnp.arange(half, dtype=F32) / half)
        ang = posf * inv[None, :]
        cos_l = jnp.cos(ang)[:, jj % half]
        sin_l = jnp.sin(ang)[:, jj % half] * jnp.where(jj < half, -1.0, 1.0).astype(F32)[None, :]
        on = jnp.asarray((jj < 2 * half) & lanes_on)[None, :]
        return jnp.where(on, cos_l, 1.0), jnp.where(on, sin_l, 0.0)

    every = np.ones(LANES, bool)
    first = np.arange(LANES) < HEAD_DIM
    parts = (*table(HEAD_DIM // 2, RET_THETA, every), *table(ROPE_DIM // 2, ROPE_THETA, every),
             *table(ROPE_DIM // 2, ROPE_THETA, first))
    return jnp.concatenate(parts, axis=1)


def _layer_weights(i, p):
    w_t = p["w_in_t"][:, i, :]
    d = w_t.shape[1]
    c_mg = w_t.shape[0] - N_BRANCH * d
    c_nsa = 8 * BRANCH_W
    w_nsa = w_t[c_nsa:c_mg]
    w_nsa = jnp.pad(w_nsa, ((0, 3 * BRANCH_W - w_nsa.shape[0]), (0, 0)))
    w = {
        "w_ret": w_t[0:4 * BRANCH_W].astype(BF16), "w_u": w_t[4 * BRANCH_W:5 * BRANCH_W].astype(BF16),
        "w_sb": w_t[5 * BRANCH_W:8 * BRANCH_W].astype(BF16), "w_nsa": w_nsa.astype(BF16),
        "w_mg": w_t[c_mg:].astype(BF16), "w_branch": p["w_branch"][i].astype(BF16),
        "w_out": p["w_out"][i].astype(BF16), "ln1_g": p["ln1_g"][i][None, :], "ln1_b": p["ln1_b"][i][None, :],
        "w_up": p["w_ffn_up"][i].astype(BF16), "w_dn": p["w_ffn_down"][i].astype(BF16),
        "ln2_g": p["ln2_g"][i][None, :], "ln2_b": p["ln2_b"][i][None, :],
        "wc": _cmp_weights(p["w_cmp_k"][i], p["w_cmp_v"][i]),
    }
    s5 = _s5_params(p["ssm_a_re"][i], p["ssm_a_im"][i], p["ssm_b_re"][i], p["ssm_b_im"][i], p["ssm_c_re"][i],
                    p["ssm_c_im"][i], p["ssm_d"][i], p["ssm_log_dt"][i], p["w_glu"][i], p["b_glu"][i])
    return w, s5


def _ssm_rows(s):
    b = s.shape[0]
    return jnp.moveaxis(s, -1, 1).reshape(b, 2, -1)


def _ssm_state(rows, g):
    b = rows.shape[0]
    return jnp.moveaxis(rows.reshape(b, 2, g, -1), 1, -1)


def _layer(x, mod, tabs, w, s5, alpha, past):
    b, l, d = x.shape
    t = b * l
    x2 = x.reshape(t, d)
    g = s5["a_re"].shape[-1] // SSM_STATE
    if past is None:
        tm = min(l, 512)
        tpb = l // tm
        mod_in, tab_in = mod, tabs
    else:
        tm, tpb = t, None
        mod_in = jnp.repeat(mod.reshape(b, 8 * d), l, axis=0)
        tab_in = jnp.tile(tabs, (b, 1))
    ret, u, sbq, sbkv, sbkv16, nq, nqr, rows, win, ng, rowsc, rowss16, win16 = _in_call(
        x2, mod_in, tab_in, w, tpb, tm)

    if past is None:
        r0 = jnp.zeros((b, BRANCH_W, BRANCH_W), F32)
        x0 = jnp.zeros((b, 2, g * SSM_STATE), F32)
    else:
        r0 = _bd_from_heads(past["ret"])
        x0 = _ssm_rows(past["ssm"])
    out_a, r_bd = _ret_call(ret, b, l, r0)
    out_b, s_rows = _s5_call(u, b, l, x0, s5)

    if past is None:
        out_c = _sb_prompt_call(sbq, sbkv16, b, l)
        out_d = _nsa_prompt_call(nq, nqr, ng, rowsc, rowss16, win16, w["wc"], b, l)
        win_state = win.reshape(b, l, 2, HEAD_DIM)[:, l - min(WINDOW, l):]
    else:
        kvn_t = jnp.swapaxes(jnp.pad(sbkv16.reshape(b, l, 2 * BRANCH_W), ((0, 0), (0, TK - l), (0, 0))), 1, 2)
        out_c = _sb_decode_call(sbq.reshape(b, l, BRANCH_W), kvn_t, past["sb"], past["layer"], past["page_table"])
        rows3 = rows.reshape(b, l, BRANCH_W)
        rown_t = jnp.swapaxes(jnp.pad(rows3[:, :, LANES:], ((0, 0), (0, TK - l), (0, 0))), 1, 2)
        out_d, wst = _nsa_decode_call(nq.reshape(b, l, BRANCH_W), nqr.reshape(b, l, BRANCH_W),
                                      ng.reshape(b, l, LANES), rows3, rown_t,
                                      win.reshape(b, l, LANES), past["nsa"], past["win"], past["layer"],
                                      past["page_table"], w["wc"])
        out_c = out_c.reshape(t, BRANCH_W)
        out_d = out_d.reshape(t, BRANCH_W)
        win_state = wst.reshape(b, -1, 2, HEAD_DIM)

    tm2 = min(tm, 256)
    tpb2 = None if tpb is None else l // tm2
    x1 = _merge_call(x2, mod_in, [out_a, out_b, out_c, out_d], w, tpb2, tm2, alpha)
    xo = _ffn_call(x1, mod_in, w, tpb2, tm2, alpha)
    state = (sbkv.reshape(b, l, 2, N_HEADS, HEAD_DIM), rows.reshape(b, l, 4, HEAD_DIM), win_state,
             _heads_from_bd(r_bd), _ssm_state(s_rows, g))
    return xo.reshape(b, l, d), state


def kernel(x_prompt, x_sample, c_prompt, c_sample, cache_sb, cache_nsa, cache_win, state_ret, state_ssm, page_table, w_ada, b_ada, w_in, ssm_a_re, ssm_a_im, ssm_b_re, ssm_b_im, ssm_c_re, ssm_c_im, ssm_d, ssm_log_dt, w_glu, b_glu, w_cmp_k, w_cmp_v, w_branch, w_out, ln1_g, ln1_b, w_ffn_up, w_ffn_down, ln2_g, ln2_b):
    p = dict(w_in_t=jnp.transpose(w_in, (2, 0, 1)), ssm_a_re=ssm_a_re, ssm_a_im=ssm_a_im, ssm_b_re=ssm_b_re, ssm_b_im=ssm_b_im,
             ssm_c_re=ssm_c_re, ssm_c_im=ssm_c_im, ssm_d=ssm_d, ssm_log_dt=ssm_log_dt, w_glu=w_glu, b_glu=b_glu,
             w_cmp_k=w_cmp_k, w_cmp_v=w_cmp_v, w_branch=w_branch, w_out=w_out, ln1_g=ln1_g, ln1_b=ln1_b,
             w_ffn_up=w_ffn_up, w_ffn_down=w_ffn_down, ln2_g=ln2_g, ln2_b=ln2_b)
    depth, d = w_ada.shape[0], w_ada.shape[1]
    bp, lp, _ = x_prompt.shape
    bs, ls, _ = x_sample.shape
    n_pool, page = cache_sb.shape[1], cache_sb.shape[2]
    past_len = page_table.shape[1] * page
    alpha = (2.0 * depth) ** 0.25

    c_all = jnp.concatenate([c_prompt, c_sample], axis=0)
    m_rows = _round_up(bp + bs, SUBLANES)
    c_all = jnp.pad(c_all, ((0, m_rows - bp - bs), (0, 0)))
    mods = _ada_call(c_all, w_ada, b_ada)

    tabs_p = _rope_tables(jnp.arange(lp, dtype=jnp.int32))
    tabs_s = _rope_tables(past_len + jnp.arange(ls, dtype=jnp.int32))
    cache_sb4 = jnp.transpose(cache_sb, (0, 1, 3, 4, 5, 2)).reshape(depth, n_pool, 2 * BRANCH_W, page)
    cache_nsa4 = jnp.transpose(cache_nsa, (0, 1, 3, 4, 2)).reshape(depth, n_pool, BRANCH_W, page)
    cache_win4 = cache_win.reshape(depth, bs, cache_win.shape[2], LANES)

    xp, xs = x_prompt, x_sample
    st_p = [[] for _ in range(5)]
    st_s = [[] for _ in range(5)]
    for i in range(depth):
        w, s5 = _layer_weights(i, p)
        mod_i = jnp.pad(mods[i].reshape(m_rows, 6, d), ((0, 0), (0, 2), (0, 0)))
        xp, new_p = _layer(xp, mod_i[:bp], tabs_p, w, s5, alpha, None)
        past = dict(sb=cache_sb4, nsa=cache_nsa4, win=cache_win4, ret=state_ret[i], ssm=state_ssm[i],
                    layer=i, page_table=page_table)
        xs, new_s = _layer(xs, mod_i[bp:bp + bs], tabs_s, w, s5, alpha, past)
        for k in range(5):
            st_p[k].append(new_p[k])
            st_s[k].append(new_s[k])
    sb_p, nsa_p, win_p, ret_p, ssm_p = [jnp.stack(s, axis=0) for s in st_p]
    sb_s, nsa_s, win_s, ret_s, ssm_s = [jnp.stack(s, axis=0) for s in st_s]
    return (xp, xs, sb_p, sb_s, nsa_p, nsa_s, win_p, win_s, ret_p, ret_s, ssm_p, ssm_s)
```

```python
import functools
import math

import jax
import jax.numpy as jnp
import numpy as np
from jax import lax
from jax.experimental import pallas as pl
from jax.experimental.pallas import tpu as pltpu

F32 = jnp.float32
BF16 = jnp.bfloat16

LANES = 128
SUBLANES = 8
VMEM_LIMIT_BYTES = 56 * 1024 * 1024

HEAD_DIM = 64
N_HEADS = 4
BRANCH_W = N_HEADS * HEAD_DIM
N_BRANCH = 4
RET_THETA = 10000.0
RET_CHUNK = 128
SSM_GROUP_CH = 16
SSM_STATE = 64
CMP_BLOCK = 32
CMP_STRIDE = 16
SLC_BLOCK = 64
N_SEL = 16
WINDOW = 512
ROPE_THETA = 500000.0
ROPE_DIM = HEAD_DIM // 4
LN_EPS = 1e-5
QK_SCALE = HEAD_DIM ** -0.5
NEG_BIG = -1e30


def _cparams(*sem, flags=None):
    return pltpu.CompilerParams(dimension_semantics=tuple(sem), vmem_limit_bytes=VMEM_LIMIT_BYTES, flags=flags)


def _lane_iota(shape):
    return lax.broadcasted_iota(jnp.int32, shape, len(shape) - 1)


def _row_iota(shape):
    return lax.broadcasted_iota(jnp.int32, shape, len(shape) - 2)


def _dot(a, b):
    return jnp.dot(a.astype(BF16), b.astype(BF16), preferred_element_type=F32)


def _dot_nt(a, b):
    return lax.dot_general(a.astype(BF16), b.astype(BF16), (((1,), (1,)), ((), ())), preferred_element_type=F32)


def _dot_tn(a, b):
    return lax.dot_general(a.astype(BF16), b.astype(BF16), (((0,), (0,)), ((), ())), preferred_element_type=F32)


def _ln_rows(x):
    mu = jnp.mean(x, axis=-1, keepdims=True)
    xc = x - mu
    var = jnp.mean(xc * xc, axis=-1, keepdims=True)
    return xc * lax.rsqrt(var + LN_EPS)


def _sigmoid(x):
    return 1.0 / (1.0 + jnp.exp(-x))


def _silu(x):
    return x * _sigmoid(x)


def _head_mask(shape, h):
    lane = _lane_iota(shape)
    return (lane >= h * HEAD_DIM) & (lane < (h + 1) * HEAD_DIM)


def _ada_kernel(c_ref, w_ref, b_ref, o_ref):
    o_ref[0] = _dot(_silu(c_ref[...]), w_ref[0]) + b_ref[0]


def _ada_call(c_all, w_ada, b_ada):
    depth, d, n = w_ada.shape
    m = c_all.shape[0]
    tn = 1536
    return pl.pallas_call(
        _ada_kernel,
        grid=(depth, n // tn),
        in_specs=[pl.BlockSpec((m, d), lambda i, j: (0, 0)),
                  pl.BlockSpec((1, d, tn), lambda i, j: (i, 0, j)),
                  pl.BlockSpec((1, 1, tn), lambda i, j: (i, 0, j))],
        out_specs=pl.BlockSpec((1, m, tn), lambda i, j: (i, 0, j)),
        out_shape=jax.ShapeDtypeStruct((depth, m, n), F32),
        compiler_params=_cparams("parallel", "parallel"),
        name="ada_mod",
    )(c_all, w_ada, b_ada.reshape(depth, 1, n))


def _rope128(x, cos, sin_signed, half):
    first = (_lane_iota(x.shape) % HEAD_DIM) < half
    partner = jnp.where(first, pltpu.roll(x, LANES - half, 1), pltpu.roll(x, half, 1))
    return x * cos + partner * sin_signed


def _mod_rows(m_ref, row, per_token):
    if per_token:
        d = m_ref.shape[-1] // 8
        return m_ref[:, row * d:(row + 1) * d]
    return m_ref[0, row:row + 1, :]


def _in_kernel(per_token, x_ref, m_ref, wr_ref, wu_ref, wsb_ref, wn_ref, tab_ref,
               ret_ref, u_ref, sbq_ref, sbkv_ref, sbkv16_ref, nq_ref, nqr_ref, rows_ref, win_ref, ng_ref,
               rowsc_ref, rowss16_ref, win16_ref, *maybe_rowst16_ref):
    x = x_ref[...]
    h = (_ln_rows(x) * (1.0 + _mod_rows(m_ref, 1, per_token)) + _mod_rows(m_ref, 0, per_token)).astype(BF16)
    tab = tab_ref[...]
    rc, rs = tab[:, 0:128], tab[:, 128:256]
    qc, qs = tab[:, 256:384], tab[:, 384:512]
    kc, ks = tab[:, 512:640], tab[:, 640:768]

    zr = _dot_nt(h, wr_ref[...])
    half_r = HEAD_DIM // 2
    for c in range(2):
        ret_ref[:, c * 128:(c + 1) * 128] = _rope128(zr[:, c * 128:(c + 1) * 128], rc, rs, half_r)
    for c in range(2, 4):
        ret_ref[:, c * 128:(c + 1) * 128] = _rope128(zr[:, c * 128:(c + 1) * 128], rc, rs, half_r) * QK_SCALE
    ret_ref[:, 512:1024] = zr[:, 512:1024]

    u_ref[...] = _dot_nt(h, wu_ref[...])

    zs = _dot_nt(h, wsb_ref[...])
    sbq_ref[...] = (zs[:, 0:256] * QK_SCALE).astype(BF16)
    sbkv_ref[...] = zs[:, 256:768]
    sbkv16_ref[...] = zs[:, 256:768].astype(BF16)

    zn = _dot_nt(h, wn_ref[...])
    half_n = ROPE_DIM // 2
    nq_ref[...] = (zn[:, 0:256] * QK_SCALE).astype(BF16)
    for c in range(2):
        nqr_ref[:, c * 128:(c + 1) * 128] = (
            _rope128(zn[:, c * 128:(c + 1) * 128], qc, qs, half_n) * QK_SCALE).astype(BF16)
    rows_slc = _rope128(zn[:, 384:512], kc, ks, half_n)
    rows_ref[:, 0:128] = zn[:, 256:384]
    rows_ref[:, 128:256] = rows_slc
    rowsc_ref[...] = zn[:, 256:384]
    rowss16_ref[...] = rows_slc.astype(BF16)
    if maybe_rowst16_ref:
        maybe_rowst16_ref[0][...] = rows_slc.T.astype(BF16)
    win = _rope128(zn[:, 512:640], kc, ks, half_n)
    win_ref[...] = win
    win16_ref[...] = win.astype(BF16)
    if maybe_rowst16_ref:
        maybe_rowst16_ref[1][...] = win.T.astype(BF16)
    ng_ref[...] = _sigmoid(zn[:, 640:768])


def _in_call(x2, mod, tabs, w, tiles_per_batch, tm):
    t, d = x2.shape
    per_token = tiles_per_batch is None
    nt = t // tm
    if per_token:
        mod_spec = pl.BlockSpec((tm, 8 * d), lambda i: (i, 0))
        tab_spec = pl.BlockSpec((tm, 768), lambda i: (i, 0))
    else:
        mod_spec = pl.BlockSpec((1, 8, d), lambda i: (i // tiles_per_batch, 0, 0))
        tab_spec = pl.BlockSpec((tm, 768), lambda i: (i % tiles_per_batch, 0))

    def wspec(a):
        return pl.BlockSpec(a.shape, lambda i: (0, 0))

    def ospec(n):
        return pl.BlockSpec((tm, n), lambda i: (i, 0))

    outs = [(1024, F32), (256, F32), (256, BF16), (512, F32), (512, BF16), (256, BF16), (256, BF16),
            (256, F32), (128, F32), (128, F32), (128, F32), (128, BF16), (128, BF16)]
    return pl.pallas_call(
        functools.partial(_in_kernel, per_token),
        grid=(nt,),
        in_specs=[pl.BlockSpec((tm, d), lambda i: (i, 0)), mod_spec,
                  wspec(w["w_ret"]), wspec(w["w_u"]), wspec(w["w_sb"]), wspec(w["w_nsa"]), tab_spec],
        out_specs=[ospec(n) for n, _ in outs] + ([] if per_token else [pl.BlockSpec((LANES, tm), lambda i: (0, i))] * 2),
        out_shape=[jax.ShapeDtypeStruct((t, n), dt) for n, dt in outs]
        + ([] if per_token else [jax.ShapeDtypeStruct((LANES, t), BF16)] * 2),
        compiler_params=_cparams("parallel"),
        name="in_proj",
    )(x2, mod, w["w_ret"], w["w_u"], w["w_sb"], w["w_nsa"], tabs)


def _merge_kernel(per_token, alpha, x_ref, m_ref, a_ref, b_ref, c_ref, d_ref, wmg_ref, wbr_ref, wout_ref,
                  g_ref, bias_ref, o_ref):
    x = x_ref[...]
    dm = x.shape[-1]
    h = (_ln_rows(x) * (1.0 + _mod_rows(m_ref, 1, per_token)) + _mod_rows(m_ref, 0, per_token)).astype(BF16)
    acc = jnp.zeros(x.shape, F32)
    for n, br_ref in enumerate((a_ref, b_ref, c_ref, d_ref)):
        mg = _dot_nt(h, wmg_ref[n * dm:(n + 1) * dm, :])
        br = jnp.dot(br_ref[...].astype(BF16), wbr_ref[n], preferred_element_type=F32)
        acc = acc + _sigmoid(mg) * br
    y = jnp.dot(acc.astype(BF16), wout_ref[...], preferred_element_type=F32)
    r = alpha * x + (1.0 + _mod_rows(m_ref, 2, per_token)) * y
    o_ref[...] = _ln_rows(r) * g_ref[...] + bias_ref[...]


def _merge_call(x2, mod, branches, w, tiles_per_batch, tm, alpha):
    t, d = x2.shape
    per_token = tiles_per_batch is None
    if per_token:
        mod_spec = pl.BlockSpec((tm, 8 * d), lambda i: (i, 0))
    else:
        mod_spec = pl.BlockSpec((1, 8, d), lambda i: (i // tiles_per_batch, 0, 0))
    row = lambda n: pl.BlockSpec((tm, n), lambda i: (i, 0))
    return pl.pallas_call(
        functools.partial(_merge_kernel, per_token, alpha),
        grid=(t // tm,),
        in_specs=[row(d), mod_spec, row(BRANCH_W), row(BRANCH_W), row(BRANCH_W), row(BRANCH_W),
                  pl.BlockSpec(w["w_mg"].shape, lambda i: (0, 0)),
                  pl.BlockSpec(w["w_branch"].shape, lambda i: (0, 0, 0)),
                  pl.BlockSpec(w["w_out"].shape, lambda i: (0, 0)),
                  pl.BlockSpec((1, d), lambda i: (0, 0)), pl.BlockSpec((1, d), lambda i: (0, 0))],
        out_specs=row(d),
        out_shape=jax.ShapeDtypeStruct((t, d), F32),
        compiler_params=_cparams("parallel"),
        name="merge_out",
    )(x2, mod, *branches, w["w_mg"], w["w_branch"], w["w_out"], w["ln1_g"], w["ln1_b"])


def _ffn_kernel(per_token, alpha, n_chunk, x_ref, m_ref, wup_ref, wdn_ref, g_ref, bias_ref, o_ref):
    x = x_ref[...]
    dff = wdn_ref.shape[0]
    ck = dff // n_chunk
    h = (_ln_rows(x) * (1.0 + _mod_rows(m_ref, 4, per_token)) + _mod_rows(m_ref, 3, per_token)).astype(BF16)
    f = jnp.zeros(x.shape, F32)
    for c in range(n_chunk):
        a = jnp.dot(h, wup_ref[:, c * ck:(c + 1) * ck], preferred_element_type=F32)
        u = jnp.dot(h, wup_ref[:, dff + c * ck:dff + (c + 1) * ck], preferred_element_type=F32)
        f = f + jnp.dot((_silu(a) * u).astype(BF16), wdn_ref[c * ck:(c + 1) * ck, :], preferred_element_type=F32)
    r = alpha * x + (1.0 + _mod_rows(m_ref, 5, per_token)) * f
    o_ref[...] = _ln_rows(r) * g_ref[...] + bias_ref[...]


def _ffn_call(x2, mod, w, tiles_per_batch, tm, alpha):
    t, d = x2.shape
    per_token = tiles_per_batch is None
    if per_token:
        mod_spec = pl.BlockSpec((tm, 8 * d), lambda i: (i, 0))
    else:
        mod_spec = pl.BlockSpec((1, 8, d), lambda i: (i // tiles_per_batch, 0, 0))
    dff = w["w_dn"].shape[0]
    n_chunk = 2 if dff % 256 == 0 else 1
    return pl.pallas_call(
        functools.partial(_ffn_kernel, per_token, alpha, n_chunk),
        grid=(t // tm,),
        in_specs=[pl.BlockSpec((tm, d), lambda i: (i, 0)), mod_spec,
                  pl.BlockSpec(w["w_up"].shape, lambda i: (0, 0)),
                  pl.BlockSpec(w["w_dn"].shape, lambda i: (0, 0)),
                  pl.BlockSpec((1, d), lambda i: (0, 0)), pl.BlockSpec((1, d), lambda i: (0, 0))],
        out_specs=pl.BlockSpec((tm, d), lambda i: (i, 0)),
        out_shape=jax.ShapeDtypeStruct((t, d), F32),
        compiler_params=_cparams("parallel"),
        name="ffn",
    )(x2, mod, w["w_up"], w["w_dn"], w["ln2_g"], w["ln2_b"])


def _ret_kernel(chunk, q_ref, k_ref, v_ref, g_ref, r0_ref, intra_ref, qdec_ref, kdec_ref, cdec_ref,
                o_ref, rout_ref, r_s):
    j = pl.program_id(1)

    @pl.when(j == 0)
    def _():
        r_s[...] = r0_ref[0]

    q = q_ref[...]
    k = k_ref[...]
    v16 = v_ref[...].astype(BF16)
    k16 = k.astype(BF16)
    o = _dot(q * qdec_ref[...], r_s[...])
    for h in range(N_HEADS):
        hm = _head_mask(q.shape, h)
        s = _dot_nt(jnp.where(hm, q, 0.0), k16) * intra_ref[h]
        o = o + jnp.where(hm, _dot(s, v16), 0.0)
    upd = _dot_tn(k * kdec_ref[...], v16)
    rr = _row_iota(upd.shape) // HEAD_DIM
    cc = _lane_iota(upd.shape) // HEAD_DIM
    r_new = r_s[...] * cdec_ref[...] + jnp.where(rr == cc, upd, 0.0)
    r_s[...] = r_new

    @pl.when(j == pl.num_programs(1) - 1)
    def _():
        rout_ref[0] = r_new

    mu = jnp.zeros(o.shape, F32)
    for h in range(N_HEADS):
        hm = _head_mask(o.shape, h)
        mu = mu + jnp.where(hm, jnp.sum(jnp.where(hm, o, 0.0), axis=-1, keepdims=True), 0.0)
    oc = o - mu * (1.0 / HEAD_DIM)
    var = jnp.zeros(o.shape, F32)
    oc2 = oc * oc
    for h in range(N_HEADS):
        hm = _head_mask(o.shape, h)
        var = var + jnp.where(hm, jnp.sum(jnp.where(hm, oc2, 0.0), axis=-1, keepdims=True), 0.0)
    o_ref[...] = oc * lax.rsqrt(var * (1.0 / HEAD_DIM) + LN_EPS) * _silu(g_ref[...])


def _ret_tables(chunk):
    h = jnp.arange(N_HEADS, dtype=F32)
    log_g = jnp.log1p(-jnp.exp2(-5.0 - h))
    idx = jnp.arange(chunk, dtype=F32)
    diff = idx[:, None] - idx[None, :]
    intra = jnp.where(diff >= 0, jnp.exp(jnp.maximum(diff, 0.0)[None] * log_g[:, None, None]), 0.0)
    q_dec = jnp.exp((idx + 1.0)[None, :] * log_g[:, None])
    k_dec = jnp.exp((chunk - 1.0 - idx)[None, :] * log_g[:, None])
    c_dec = jnp.exp(chunk * log_g)
    lanes = lambda t: jnp.repeat(t.T, HEAD_DIM, axis=1)
    return intra, lanes(q_dec), lanes(k_dec), jnp.repeat(c_dec, HEAD_DIM)[None, :]


def _ret_call(ret, b, l, r0_bd):
    chunk = RET_CHUNK if l % RET_CHUNK == 0 else l
    n = l // chunk
    intra, q_dec, k_dec, c_dec = _ret_tables(chunk)
    col = lambda c: pl.BlockSpec((chunk, BRANCH_W), lambda i, j: (i * n + j, c))
    full = lambda a: pl.BlockSpec(a.shape, lambda i, j: (0,) * a.ndim)
    return pl.pallas_call(
        functools.partial(_ret_kernel, chunk),
        grid=(b, n),
        in_specs=[col(0), col(1), col(2), col(3),
                  pl.BlockSpec((1, BRANCH_W, BRANCH_W), lambda i, j: (i, 0, 0)),
                  full(intra), full(q_dec), full(k_dec), full(c_dec)],
        out_specs=[pl.BlockSpec((chunk, BRANCH_W), lambda i, j: (i * n + j, 0)),
                   pl.BlockSpec((1, BRANCH_W, BRANCH_W), lambda i, j: (i, 0, 0))],
        out_shape=[jax.ShapeDtypeStruct((b * l, BRANCH_W), F32),
                   jax.ShapeDtypeStruct((b, BRANCH_W, BRANCH_W), F32)],
        scratch_shapes=[pltpu.VMEM((BRANCH_W, BRANCH_W), F32)],
        compiler_params=_cparams("parallel", "arbitrary"),
        name="retention",
    )(ret, ret, ret, ret, r0_bd, intra, q_dec, k_dec, c_dec)


def _bd_from_heads(r):
    b = r.shape[0]
    eye = jnp.eye(N_HEADS, dtype=r.dtype)
    return jnp.einsum("bhij,hg->bhigj", r, eye).reshape(b, BRANCH_W, BRANCH_W)


def _heads_from_bd(r_bd):
    b = r_bd.shape[0]
    r5 = r_bd.reshape(b, N_HEADS, HEAD_DIM, N_HEADS, HEAD_DIM)
    return jnp.stack([r5[:, h, :, h, :] for h in range(N_HEADS)], axis=1)


def _cmul(ar, ai, br, bi):
    return ar * br - ai * bi, ar * bi + ai * br


def _gelu_tanh(x):
    return 0.5 * x * (1.0 + jnp.tanh(math.sqrt(2.0 / math.pi) * (x + 0.044715 * (x * x * x))))


def _s5_kernel(chain, u_ref, x0_ref, are_ref, aim_ref, ldt_ref, bre_ref, bim_ref, cre_ref, cim_ref, dsk_ref,
               wglu_ref, bglu_ref, o_ref, st_ref, xr_s, xi_s, cr_s, ci_s):
    rows = u_ref.shape[0]
    n_grp = rows // SUBLANES
    j = pl.program_id(1)

    ar, ai = are_ref[...], aim_ref[...]
    dt = jnp.exp(ldt_ref[...])
    mag = jnp.exp(ar * dt)
    abr, abi = mag * jnp.cos(ai * dt), mag * jnp.sin(ai * dt)
    nr, ni = abr - 1.0, abi
    den = ar * ar + ai * ai
    fr = (nr * ar + ni * ai) / den
    fi = (ni * ar - nr * ai) / den
    bbr = fr * bre_ref[...] - fi * bim_ref[...]
    bbi = fr * bim_ref[...] + fi * bre_ref[...]

    u = u_ref[...]
    u16 = u.astype(BF16)
    xr_s[...] = jnp.dot(u16, bbr.astype(BF16), preferred_element_type=F32)
    xi_s[...] = jnp.dot(u16, bbi.astype(BF16), preferred_element_type=F32)

    a2r, a2i = _cmul(abr, abi, abr, abi)
    a4r, a4i = _cmul(a2r, a2i, a2r, a2i)
    row = _row_iota((SUBLANES, abr.shape[-1]))
    pr, pi_ = jnp.broadcast_to(abr, row.shape), jnp.broadcast_to(abi, row.shape)
    qr, qi = abr, abi
    for i in range(1, SUBLANES):
        qr, qi = _cmul(qr, qi, abr, abi)
        pr = jnp.where(row == i, qr, pr)
        pi_ = jnp.where(row == i, qi, pi_)
    steps = ((1, abr, abi), (2, a2r, a2i), (4, a4r, a4i))

    if chain:
        @pl.when(j == 0)
        def _():
            cr_s[...] = x0_ref[0, 0:1, :]
            ci_s[...] = x0_ref[0, 1:2, :]

    def group(g, carry):
        r0 = pl.multiple_of(g * SUBLANES, SUBLANES)
        xr = xr_s[pl.ds(r0, SUBLANES), :]
        xi = xi_s[pl.ds(r0, SUBLANES), :]
        for d, er, ei in steps:
            sr = jnp.where(row >= d, pltpu.roll(xr, d, 0), 0.0)
            si = jnp.where(row >= d, pltpu.roll(xi, d, 0), 0.0)
            tr, ti = _cmul(er, ei, sr, si)
            xr, xi = xr + tr, xi + ti
        if chain:
            c_r, c_i = cr_s[...], ci_s[...]
        else:
            c_r, c_i = x0_ref[g, 0:1, :], x0_ref[g, 1:2, :]
        tr, ti = _cmul(pr, pi_, c_r, c_i)
        xr, xi = xr + tr, xi + ti
        xr_s[pl.ds(r0, SUBLANES), :] = xr
        xi_s[pl.ds(r0, SUBLANES), :] = xi
        if chain:
            cr_s[...] = xr[SUBLANES - 1:SUBLANES, :]
            ci_s[...] = xi[SUBLANES - 1:SUBLANES, :]
        else:
            st_ref[g, 0:1, :] = xr[SUBLANES - 1:SUBLANES, :]
            st_ref[g, 1:2, :] = xi[SUBLANES - 1:SUBLANES, :]
        return carry

    lax.fori_loop(0, n_grp, group, 0)

    if chain:
        @pl.when(j == pl.num_programs(1) - 1)
        def _():
            st_ref[0, 0:1, :] = cr_s[...]
            st_ref[0, 1:2, :] = ci_s[...]

    y = (jnp.dot(xr_s[...].astype(BF16), cre_ref[...], preferred_element_type=F32)
         - jnp.dot(xi_s[...].astype(BF16), cim_ref[...], preferred_element_type=F32) + dsk_ref[...] * u)
    zb = _gelu_tanh(y)
    o_ref[...] = zb * _sigmoid(jnp.dot(zb.astype(BF16), wglu_ref[...], preferred_element_type=F32) + bglu_ref[...])


def _s5_call(u2, b, l, x0, w):
    n_state = x0.shape[-1]
    chain = l % 128 == 0
    if chain:
        tl = min(l, 512)
        grid = (b, l // tl)
        rows = tl
        u_spec = pl.BlockSpec((tl, BRANCH_W), lambda i, j: (i * (l // tl) + j, 0))
        x0_spec = pl.BlockSpec((1, 2, n_state), lambda i, j: (i, 0, 0))
    else:
        assert l == SUBLANES
        grid = (1, 1)
        rows = b * l
        u_spec = pl.BlockSpec((rows, BRANCH_W), lambda i, j: (0, 0))
        x0_spec = pl.BlockSpec((b, 2, n_state), lambda i, j: (0, 0, 0))
    full = lambda a: pl.BlockSpec(a.shape, lambda i, j: (0,) * a.ndim)
    params = [w["a_re"], w["a_im"], w["log_dt"], w["b_re_bd"], w["b_im_bd"], w["c_re_bd"], w["c_im_bd"],
              w["d_skip"], w["w_glu"], w["b_glu"]]
    return pl.pallas_call(
        functools.partial(_s5_kernel, chain),
        grid=grid,
        in_specs=[u_spec, x0_spec] + [full(a) for a in params],
        out_specs=[u_spec, x0_spec],
        out_shape=[jax.ShapeDtypeStruct((b * l, BRANCH_W), F32), jax.ShapeDtypeStruct(x0.shape, F32)],
        scratch_shapes=[pltpu.VMEM((rows, n_state), F32), pltpu.VMEM((rows, n_state), F32),
                        pltpu.VMEM((1, n_state), F32), pltpu.VMEM((1, n_state), F32)],
        compiler_params=_cparams("parallel", "arbitrary"),
        name="s5",
    )(u2, x0, *params)


def _s5_params(a_re, a_im, b_re, b_im, c_re, c_im, d_skip, log_dt, w_glu, b_glu):
    g, p = a_re.shape
    cg = b_re.shape[-1]
    eye = jnp.eye(g, dtype=F32)
    b_bd = lambda t: jnp.einsum("gpc,gh->gchp", t, eye).reshape(g * cg, g * p)
    c_bd = lambda t: jnp.einsum("gcp,gh->gphc", t, eye).reshape(g * p, g * cg)
    return {
        "a_re": a_re.reshape(1, g * p), "a_im": a_im.reshape(1, g * p),
        "log_dt": jnp.repeat(log_dt, p).reshape(1, g * p),
        "b_re_bd": b_bd(b_re), "b_im_bd": b_bd(b_im),
        "c_re_bd": c_bd(c_re).astype(BF16), "c_im_bd": c_bd(c_im).astype(BF16),
        "d_skip": d_skip.reshape(1, g * cg), "w_glu": w_glu.astype(BF16), "b_glu": b_glu.reshape(1, -1),
    }


TK = 128


def _stack_heads_bd(q):
    qf = q.astype(F32)
    return jnp.concatenate([jnp.where(_head_mask(qf.shape, h), qf, 0.0) for h in range(N_HEADS)], axis=0).astype(BF16)


def _unstack_heads_bd(acc, tq):
    out = jnp.zeros((tq, acc.shape[-1]), F32)
    for h in range(N_HEADS):
        blk = acc[h * tq:(h + 1) * tq]
        out = out + jnp.where(_head_mask(blk.shape, h), blk, 0.0)
    return out


def _suffix_matrix(tk):
    r = _row_iota((2 * tk, tk)) % tk
    c = _lane_iota((2 * tk, tk))
    return jnp.where(r > c, 1.0, 0.0).astype(BF16)


def _sb_scores(qbd, k16, transposed):
    return jnp.dot(qbd, k16, preferred_element_type=F32) if transposed else _dot_nt(qbd, k16)


def _sb_tile(qbd, k16, v16, mask, u2, carry, acc, transposed):
    return _sb_tile_z(_sb_scores(qbd, k16, transposed), v16, mask, u2, carry, acc, transposed)


def _sb_logw(z, mask, u2):
    log_beta = jnp.minimum(z, 0.0) - jnp.log(1.0 + jnp.exp(-jnp.abs(z)))
    l1m = log_beta - z
    if mask is not None:
        l1m = jnp.where(mask, l1m, 0.0)
    hi = lax.bitcast_convert_type(lax.bitcast_convert_type(l1m, jnp.uint32) & jnp.uint32(0xFFFF0000), F32)
    lo = l1m - hi
    suffix = jnp.dot(jnp.concatenate([hi.astype(BF16), lo.astype(BF16)], axis=1), u2, preferred_element_type=F32)
    return log_beta + suffix, suffix[:, 0:1] + l1m[:, 0:1]


def _sb_accumulate(logw, total, v16, mask, carry, acc, transposed):
    w = jnp.exp(logw + carry)
    if mask is not None:
        w = jnp.where(mask, w, 0.0)
    w16 = w.astype(BF16)
    acc = acc + (_dot_nt(w16, v16) if transposed else jnp.dot(w16, v16, preferred_element_type=F32))
    return carry + total, acc


def _sb_tile_z(z, v16, mask, u2, carry, acc, transposed):
    logw, total = _sb_logw(z, mask, u2)
    return _sb_accumulate(logw, total, v16, mask, carry, acc, transposed)


TKP = 256
SB_DEAD_LOG = -104.0


def _sb_prompt_kernel(q_ref, kv_ref, o_ref):
    j = pl.program_id(1)
    tq = q_ref.shape[0]
    qbd = _stack_heads_bd(q_ref[...])
    u2 = _suffix_matrix(TKP)
    rows = N_HEADS * tq
    n_full = (j * tq) // TKP
    qpos = j * tq + _row_iota((rows, TKP)) % tq
    diag_mask = (n_full * TKP + _lane_iota((rows, TKP))) < qpos

    def keys(kt):
        return kv_ref[pl.ds(pl.multiple_of(kt * TKP, TKP), TKP), 0:BRANCH_W]

    def values(kt):
        return kv_ref[pl.ds(pl.multiple_of(kt * TKP, TKP), TKP), BRANCH_W:2 * BRANCH_W]

    carry, acc = _sb_tile(qbd, keys(n_full), values(n_full), diag_mask, u2,
                          jnp.zeros((rows, 1), F32), jnp.zeros((rows, BRANCH_W), F32), False)

    def cond(st):
        it, carry, _ = st
        return (it < n_full) & (jnp.max(carry) >= SB_DEAD_LOG)

    def body(st):
        it, carry, acc = st
        kt = n_full - 1 - it
        carry, acc = _sb_tile(qbd, keys(kt), values(kt), None, u2, carry, acc, False)
        return it + 1, carry, acc

    _, carry, acc = lax.while_loop(cond, body, (jnp.int32(0), carry, acc))
    o_ref[...] = _unstack_heads_bd(acc, tq)


def _sb_prompt_call(sbq, sbkv16, b, l):
    tq = TK
    n = l // tq
    assert l % TKP == 0
    return pl.pallas_call(
        _sb_prompt_kernel,
        grid=(b, n),
        in_specs=[pl.BlockSpec((tq, BRANCH_W), lambda i, j: (i * n + j, 0)),
                  pl.BlockSpec((l, 2 * BRANCH_W), lambda i, j: (i, 0))],
        out_specs=pl.BlockSpec((tq, BRANCH_W), lambda i, j: (i * n + j, 0)),
        out_shape=jax.ShapeDtypeStruct((b * l, BRANCH_W), F32),
        compiler_params=_cparams("parallel", "arbitrary"),
        name="sb_prompt",
    )(sbq, sbkv16)


PAGES_PER_STEP = 8


def _sb_decode_kernel(n_pg, pt_ref, q_ref, kvn_ref, *rest):
    page_refs = rest[:n_pg]
    o_ref, carry_s, acc_s = rest[n_pg:]
    s = pl.program_id(1)
    tq = q_ref.shape[1]
    rows = N_HEADS * tq
    qbd = _stack_heads_bd(q_ref[0])
    u2 = _suffix_matrix(TK)

    @pl.when(s == 0)
    def _():
        mask = _lane_iota((rows, TK)) < (_row_iota((rows, TK)) % tq)
        kvn = kvn_ref[0]
        carry, acc = _sb_tile(qbd, kvn[0:BRANCH_W, :], kvn[BRANCH_W:, :], mask, u2,
                              jnp.zeros((rows, 1), F32), jnp.zeros((rows, BRANCH_W), F32), True)
        carry_s[...] = carry
        acc_s[...] = acc

    @pl.when(jnp.max(carry_s[...]) >= SB_DEAD_LOG)
    def _():
        carry, acc = carry_s[...], acc_s[...]
        group = 2 if n_pg % 2 == 0 else 1
        u2g = _suffix_matrix(group * TK) if group > 1 else u2
        for k in range(0, n_pg, group):
            kv = jnp.concatenate([page_refs[k + g][0, 0] for g in reversed(range(group))], axis=1).astype(BF16)
            carry, acc = _sb_tile(qbd, kv[0:BRANCH_W, :], kv[BRANCH_W:, :], None, u2g, carry, acc, True)
        carry_s[...] = carry
        acc_s[...] = acc

    @pl.when(s == pl.num_programs(1) - 1)
    def _():
        o_ref[0] = _unstack_heads_bd(acc_s[...], tq)


def _sb_decode_call(sbq3, kvn_t16, cache_t, layer, page_table):
    b, tq, _ = sbq3.shape
    n_pages = page_table.shape[1]
    n_pg = math.gcd(PAGES_PER_STEP, n_pages)
    n_steps = n_pages // n_pg

    def page_spec(k):
        return pl.BlockSpec((1, 1, 2 * BRANCH_W, TK),
                            lambda i, s, pt: (layer, pt[i, n_pages - 1 - (s * n_pg + k)], 0, 0))

    grid_spec = pltpu.PrefetchScalarGridSpec(
        num_scalar_prefetch=1,
        grid=(b, n_steps),
        in_specs=[pl.BlockSpec((1, tq, BRANCH_W), lambda i, s, pt: (i, 0, 0)),
                  pl.BlockSpec((1, 2 * BRANCH_W, TK), lambda i, s, pt: (i, 0, 0))]
                 + [page_spec(k) for k in range(n_pg)],
        out_specs=pl.BlockSpec((1, tq, BRANCH_W), lambda i, s, pt: (i, 0, 0)),
        scratch_shapes=[pltpu.VMEM((N_HEADS * tq, 1), F32), pltpu.VMEM((N_HEADS * tq, BRANCH_W), F32)],
    )
    return pl.pallas_call(
        functools.partial(_sb_decode_kernel, n_pg),
        grid_spec=grid_spec,
        out_shape=jax.ShapeDtypeStruct((b, tq, BRANCH_W), F32),
        compiler_params=_cparams("parallel", "arbitrary"),
        name="sb_decode",
    )(page_table, sbq3, kvn_t16, *([cache_t] * n_pg))


def _round_up(x, m):
    return -(-x // m) * m


def _stack_heads_128(q):
    qf = q.astype(F32)
    tq = qf.shape[0]
    low = _lane_iota((tq, LANES)) < HEAD_DIM
    parts = []
    for h in range(N_HEADS):
        half = qf[:, LANES * (h // 2):LANES * (h // 2 + 1)]
        if h % 2 == 1:
            half = pltpu.roll(half, HEAD_DIM, 1)
        parts.append(jnp.where(low, half, 0.0))
    return jnp.concatenate(parts, axis=0)


def _compress(rows_ref, wc_ref, n):
    p0 = jnp.zeros((n, LANES), F32)
    p1 = jnp.zeros((n, LANES), F32)
    for s in range(CMP_STRIDE):
        x = rows_ref[pl.ds(s, n, stride=CMP_STRIDE), :].astype(BF16)
        p0 = p0 + jnp.dot(x, wc_ref[s], preferred_element_type=F32)
        p1 = p1 + jnp.dot(x, wc_ref[CMP_STRIDE + s], preferred_element_type=F32)
    return p0 + pltpu.roll(p1, n - 1, 0)


def _cmp_select(qp16, kvc, qpos_rows, qpos_q, tq, nsp, n_top):
    n = kvc.shape[0]
    kvc16 = kvc.astype(BF16)
    s = _dot_nt(qp16, kvc16)
    cmask = (_lane_iota(s.shape) * CMP_STRIDE + (CMP_BLOCK - 1)) <= qpos_rows
    s = jnp.where(cmask, s, NEG_BIG)
    m = jnp.max(s, axis=-1, keepdims=True)
    m = jnp.where(m > 0.5 * NEG_BIG, m, 0.0)
    e = jnp.where(cmask, jnp.exp(s - m), 0.0)
    p16 = (e / jnp.maximum(jnp.sum(e, axis=-1, keepdims=True), 1e-30)).astype(BF16)
    o_cmp = jnp.dot(p16, kvc16, preferred_element_type=F32)
    ci = _row_iota((n, nsp)) * CMP_STRIDE
    sj = _lane_iota((n, nsp)) * SLC_BLOCK
    c2s = jnp.where((ci < sj + SLC_BLOCK) & (sj < ci + CMP_BLOCK), 1.0, 0.0).astype(BF16)
    imp4 = jnp.dot(p16, c2s, preferred_element_type=F32)
    imp = imp4[0:tq]
    for h in range(1, N_HEADS):
        imp = imp + imp4[h * tq:(h + 1) * tq]

    blk = _lane_iota((tq, nsp))
    blkf = blk.astype(F32)
    cur = qpos_q // SLC_BLOCK
    forced = (blk == 0) | (blk == cur) | (blk == cur - 1)
    val = jnp.where(forced, jnp.inf, imp)
    avail = jnp.where(blk * SLC_BLOCK <= qpos_q, 1.0, 0.0)
    sel = jnp.zeros((tq, nsp), F32)
    for _ in range(n_top):
        vm = jnp.where(avail > 0.5, val, -jnp.inf)
        mx = jnp.max(vm, axis=-1, keepdims=True)
        cand = (avail > 0.5) & (vm == mx)
        first = jnp.min(jnp.where(cand, blkf, float(nsp)), axis=-1, keepdims=True)
        pick = blkf == first
        sel = jnp.where(pick, 1.0, sel)
        avail = jnp.where(pick, 0.0, avail)
    return o_cmp, sel.astype(BF16)


def _cmp_select_t(qp_t16, kvc, kvc_t, c2s_t16, j, tq, n_top):
    n = kvc.shape[0]
    nsp = c2s_t16.shape[0]
    rows4 = N_HEADS * tq
    qpos_cols = j * tq + _lane_iota((n, rows4)) % tq
    s = jnp.dot(kvc.astype(BF16), qp_t16, preferred_element_type=F32)
    cmask = (_row_iota((n, rows4)) * CMP_STRIDE + (CMP_BLOCK - 1)) <= qpos_cols
    s = jnp.where(cmask, s, NEG_BIG)
    m = jnp.max(s, axis=0, keepdims=True)
    m = jnp.where(m > 0.5 * NEG_BIG, m, 0.0)
    e = jnp.where(cmask, jnp.exp(s - m), 0.0)
    p16 = (e / jnp.maximum(jnp.sum(e, axis=0, keepdims=True), 1e-30)).astype(BF16)
    o_cmp_t = jnp.dot(kvc_t.astype(BF16), p16, preferred_element_type=F32)
    imp4 = jnp.dot(c2s_t16, p16, preferred_element_type=F32)
    imp = imp4[:, 0:tq]
    for h in range(1, N_HEADS):
        imp = imp + imp4[:, h * tq:(h + 1) * tq]

    blk = _row_iota((nsp, tq))
    blkf = blk.astype(F32)
    qpos = j * tq + _lane_iota((nsp, tq))
    cur = qpos // SLC_BLOCK
    forced = (blk == 0) | (blk == cur) | (blk == cur - 1)
    val = jnp.where(forced, jnp.inf, imp)
    avail = jnp.where(blk * SLC_BLOCK <= qpos, 1.0, 0.0)
    sel = jnp.zeros((nsp, tq), F32)
    for _ in range(n_top):
        vm = jnp.where(avail > 0.5, val, -jnp.inf)
        mx = jnp.max(vm, axis=0, keepdims=True)
        cand = (avail > 0.5) & (vm == mx)
        first = jnp.min(jnp.where(cand, blkf, float(nsp)), axis=0, keepdims=True)
        pick = blkf == first
        sel = jnp.where(pick, 1.0, sel)
        avail = jnp.where(pick, 0.0, avail)
    return o_cmp_t, sel


def _attn_tile(q16, kv16, mask, m, l, acc, transposed=False):
    s = jnp.dot(q16, kv16, preferred_element_type=F32) if transposed else _dot_nt(q16, kv16)
    s = jnp.where(mask, s, NEG_BIG)
    m_new = jnp.maximum(m, jnp.max(s, axis=-1, keepdims=True))
    p = jnp.where(mask, jnp.exp(s - m_new), 0.0)
    alpha = jnp.exp(m - m_new)
    l = alpha * l + jnp.sum(p, axis=-1, keepdims=True)
    p16 = p.astype(BF16)
    acc = alpha * acc + (_dot_nt(p16, kv16) if transposed else jnp.dot(p16, kv16, preferred_element_type=F32))
    return m_new, l, acc


def _attn_update(s, kv16, m, l, acc):
    m_new = jnp.maximum(m, jnp.max(s, axis=-1, keepdims=True))
    p = jnp.exp(s - m_new)
    alpha = jnp.exp(m - m_new)
    l = alpha * l + jnp.sum(p, axis=-1, keepdims=True)
    acc = alpha * acc + jnp.dot(p.astype(BF16), kv16, preferred_element_type=F32)
    return m_new, l, acc


def _attn_init(rows, width):
    return jnp.full((rows, 1), NEG_BIG, F32), jnp.zeros((rows, 1), F32), jnp.zeros((rows, width), F32)


def _slc_tile(kt, carry, qr16, kv16, selb, qpos_rows, nsp, tk, transposed):
    er = _row_iota((nsp, tk))
    ec = _lane_iota((nsp, tk))
    expand = jnp.where(er == (tk // SLC_BLOCK) * kt + ec // SLC_BLOCK, 1.0, 0.0).astype(BF16)
    seltok = jnp.dot(selb, expand, preferred_element_type=F32)
    seltok4 = jnp.concatenate([seltok] * N_HEADS, axis=0)
    tok = kt * tk + _lane_iota(seltok4.shape)
    mask = (seltok4 > 0.5) & (tok <= qpos_rows)
    return _attn_tile(qr16, kv16, mask, *carry, transposed=transposed)


def _win_tile(kv16, kpos, carry, qw16, qpos_rows):
    dist = qpos_rows - kpos
    mask = (dist >= 0) & (dist < WINDOW)
    return _attn_tile(qw16, kv16, mask, *carry)


def _nsa_combine(g, o_c, o_s, o_w, tq):
    comb = []
    for h in range(N_HEADS):
        sl = slice(h * tq, (h + 1) * tq)
        comb.append(g[:, h:h + 1] * o_c[sl] + g[:, N_HEADS + h:N_HEADS + h + 1] * o_s[sl]
                    + g[:, 2 * N_HEADS + h:2 * N_HEADS + h + 1] * o_w[sl])
    low = _lane_iota((tq, LANES)) < HEAD_DIM
    out01 = jnp.where(low, pltpu.roll(comb[0], HEAD_DIM, 1), comb[1])
    out23 = jnp.where(low, pltpu.roll(comb[2], HEAD_DIM, 1), comb[3])
    return jnp.concatenate([out01, out23], axis=1)


def _nsa_prompt_kernel(n_top, nq_ref, nqr_ref, ng_ref, rowsc_ref, rows_ref, rowst_ref, win_ref, wint_ref, onehot_ref,
                       c2st_ref, wc_ref, o_ref, kvc_s, kvct_s):
    j = pl.program_id(1)
    tq = nq_ref.shape[0]

    @pl.when(j == 0)
    def _():
        kvc = _compress(rowsc_ref, wc_ref, kvc_s.shape[0])
        kvc_s[...] = kvc
        kvct_s[...] = kvc.T

    rows4 = N_HEADS * tq
    qp_t16 = _stack_heads_128(nq_ref[...]).T.astype(BF16)
    qw_t = _stack_heads_128(nqr_ref[...]).T
    qw_t16 = qw_t.astype(BF16)

    o_c_t, sel_t = _cmp_select_t(qp_t16, kvc_s[...], kvct_s[...], c2st_ref[...], j, tq, n_top)
    last = (j * tq) // TKP

    sel_bias_t = (sel_t - 1.0) * (-NEG_BIG)
    q_aug_t = jnp.concatenate([qw_t, jnp.concatenate([sel_bias_t] * N_HEADS, axis=1)], axis=0).astype(BF16)

    def slc_scores(kt):
        r0 = pl.multiple_of(kt * TKP, TKP)
        k_aug = jnp.concatenate([rows_ref[pl.ds(r0, TKP), :], onehot_ref[pl.ds(r0, TKP), :]], axis=1)
        return jnp.dot(k_aug, q_aug_t, preferred_element_type=F32)

    def slc_update(s, kt, m, l, acc):
        m_new = jnp.maximum(m, jnp.max(s, axis=0, keepdims=True))
        p = jnp.exp(s - m_new)
        alpha = jnp.exp(m - m_new)
        l = alpha * l + jnp.sum(p, axis=0, keepdims=True)
        vt = rowst_ref[:, pl.ds(pl.multiple_of(kt * TKP, TKP), TKP)]
        acc = alpha * acc + jnp.dot(vt, p.astype(BF16), preferred_element_type=F32)
        return m_new, l, acc

    def slc_body(kt, st):
        m, l, acc, s = st
        s_next = slc_scores(jnp.minimum(kt + 1, last))
        m, l, acc = slc_update(s, kt, m, l, acc)
        return m, l, acc, s_next

    init = (jnp.full((1, rows4), NEG_BIG, F32), jnp.zeros((1, rows4), F32), jnp.zeros((LANES, rows4), F32))
    m_s, l_s, acc_s, s_diag = lax.fori_loop(0, last, slc_body, (*init, slc_scores(0)))
    tok = last * TKP + _row_iota((TKP, rows4))
    qpos_cols = j * tq + _lane_iota((TKP, rows4)) % tq
    s_diag = jnp.where(tok <= qpos_cols, s_diag, NEG_BIG)
    _, l_s, acc_s = slc_update(s_diag, last, m_s, l_s, acc_s)
    o_s = (acc_s / l_s).T

    def win_tile(kt, mask, st):
        m, l, acc = st
        r0 = pl.multiple_of(kt * TKP, TKP)
        s = jnp.dot(win_ref[pl.ds(r0, TKP), :], qw_t16, preferred_element_type=F32)
        s = jnp.where(mask, s, NEG_BIG)
        m_new = jnp.maximum(m, jnp.max(s, axis=0, keepdims=True))
        p = jnp.exp(s - m_new)
        alpha = jnp.exp(m - m_new)
        l = alpha * l + jnp.sum(p, axis=0, keepdims=True)
        acc = alpha * acc + jnp.dot(wint_ref[:, pl.ds(r0, TKP)], p.astype(BF16), preferred_element_type=F32)
        return m_new, l, acc

    krow = _row_iota((TKP, rows4))
    st_w = win_tile(last, last * TKP + krow <= qpos_cols, init)
    kb = jnp.maximum(last - 1, 0)
    st_w = win_tile(kb, kb * TKP + krow < last * TKP, st_w)
    ka = jnp.maximum(last - 2, 0)
    kpos_a = ka * TKP + krow
    _, l_w, acc_w = win_tile(ka, (qpos_cols - kpos_a < WINDOW) & (kpos_a < (last - 1) * TKP), st_w)
    o_ref[...] = _nsa_combine(ng_ref[...], o_c_t.T, o_s, (acc_w / l_w).T, tq)


def _nsa_prompt_call(nq, nqr, ng, rowsc, rowss16, rowst16, win16, wint16, wc, b, l):
    tq = TK
    n = l // tq
    n_top = min(N_SEL, l // SLC_BLOCK)
    nsp = _round_up(l // SLC_BLOCK, LANES)
    onehot = (jnp.arange(l)[:, None] // SLC_BLOCK == jnp.arange(nsp)[None, :]).astype(BF16)
    cs = jnp.arange(l // CMP_STRIDE)[None, :] * CMP_STRIDE
    ss = jnp.arange(nsp)[:, None] * SLC_BLOCK
    c2s_t = ((cs < ss + SLC_BLOCK) & (ss < cs + CMP_BLOCK)).astype(BF16)
    qspec = lambda w: pl.BlockSpec((tq, w), lambda i, j: (i * n + j, 0))
    return pl.pallas_call(
        functools.partial(_nsa_prompt_kernel, n_top),
        grid=(b, n),
        in_specs=[qspec(BRANCH_W), qspec(BRANCH_W), qspec(LANES),
                  pl.BlockSpec((l, LANES), lambda i, j: (i, 0)),
                  pl.BlockSpec((l, LANES), lambda i, j: (i, 0)),
                  pl.BlockSpec((LANES, l), lambda i, j: (0, i)),
                  pl.BlockSpec((l, LANES), lambda i, j: (i, 0)),
                  pl.BlockSpec((LANES, l), lambda i, j: (0, i)),
                  pl.BlockSpec(onehot.shape, lambda i, j: (0, 0)),
                  pl.BlockSpec(c2s_t.shape, lambda i, j: (0, 0)),
                  pl.BlockSpec(wc.shape, lambda i, j: (0, 0, 0))],
        out_specs=qspec(BRANCH_W),
        out_shape=jax.ShapeDtypeStruct((b * l, BRANCH_W), F32),
        scratch_shapes=[pltpu.VMEM((l // CMP_STRIDE, LANES), F32), pltpu.VMEM((LANES, l // CMP_STRIDE), F32)],
        compiler_params=_cparams("parallel", "arbitrary"),
        name="nsa_prompt",
    )(nq, nqr, ng, rowsc, rowss16, rowst16, win16, wint16, onehot, c2s_t, wc)


def _nsa_decode_kernel(n_pg, past_len, n_cmp, n_top, pt_ref, nq_ref, nqr_ref, ng_ref, rown_ref, rownt_ref, cwin_ref,
                       winn_ref, wc_ref, *rest):
    page_refs = rest[:n_pg]
    o_ref, wst_ref, rowsc_s, rowst_s, win_s = rest[n_pg:]
    s = pl.program_id(1)
    tq = nq_ref.shape[1]
    n_steps = rowst_s.shape[0]
    slab = n_pg * TK
    for k, pg in enumerate(page_refs):
        r0 = pl.multiple_of((s * n_pg + k) * TK, TK)
        rowsc_s[pl.ds(r0, TK), :] = pg[0, 0, 0:LANES, :].T
        rowst_s[s, :, k * TK:(k + 1) * TK] = pg[0, 0, LANES:2 * LANES, :]

    @pl.when(s == pl.num_programs(1) - 1)
    def _():
        t_rows = rowsc_s.shape[0]
        w_rows = win_s.shape[0]
        wb = cwin_ref.shape[2]
        nsp = _round_up(_round_up(past_len + tq, SLC_BLOCK) // SLC_BLOCK, LANES)
        rowsc_s[past_len:past_len + tq, :] = rown_ref[0, :, 0:LANES]
        rowsc_s[past_len + tq:t_rows, :] = jnp.zeros((t_rows - past_len - tq, LANES), F32)
        win_s[0:wb, :] = cwin_ref[0, 0]
        win_s[wb:wb + tq, :] = winn_ref[0]
        win_s[wb + tq:w_rows, :] = jnp.zeros((w_rows - wb - tq, LANES), F32)
        wst_ref[0] = win_s[tq:wb + tq, :]

        rows4 = N_HEADS * tq
        qpos_rows = past_len + lax.broadcasted_iota(jnp.int32, (rows4, 1), 0) % tq
        qpos_q = past_len + lax.broadcasted_iota(jnp.int32, (tq, 1), 0)
        qp16 = _stack_heads_128(nq_ref[0]).astype(BF16)
        qw16 = _stack_heads_128(nqr_ref[0]).astype(BF16)

        kvc = _compress(rowsc_s, wc_ref, n_cmp)
        o_c, selb = _cmp_select(qp16, kvc, qpos_rows, qpos_q, tq, nsp, n_top)

        carry = lax.fori_loop(
            0, n_steps,
            lambda kt, c: _slc_tile(kt, c, qw16, rowst_s[kt].astype(BF16), selb, qpos_rows, nsp, slab, True),
            _attn_init(rows4, LANES))
        _, l_s, acc_s = _slc_tile(past_len // TK, carry, qw16, rownt_ref[0].astype(BF16), selb, qpos_rows, nsp, TK,
                                  True)
        o_s = acc_s / l_s

        carry = _attn_init(rows4, LANES)
        for kt in range(w_rows // TK):
            kpos = past_len - wb + kt * TK + _lane_iota((rows4, TK))
            carry = _win_tile(win_s[kt * TK:(kt + 1) * TK, :].astype(BF16), kpos, carry, qw16, qpos_rows)
        _, l_w, acc_w = carry
        o_ref[0] = _nsa_combine(ng_ref[0], o_c, o_s, acc_w / l_w, tq)


def _nsa_decode_call(nq3, nqr3, ng3, rown3, rownt3, winn3, cache_nsa_t, cache_win4, layer, page_table, wc):
    b, tq, _ = nq3.shape
    n_pages = page_table.shape[1]
    past_len = n_pages * TK
    wb = cache_win4.shape[2]
    n_pg = math.gcd(PAGES_PER_STEP, n_pages)
    n_steps = n_pages // n_pg
    t_pad = _round_up(past_len + tq, SLC_BLOCK)
    n_cmp = _round_up(t_pad // CMP_STRIDE, SUBLANES)
    t_rows = _round_up(n_cmp * CMP_STRIDE, TK)
    assert t_rows == past_len + TK
    w_rows = _round_up(wb + tq, TK)
    n_top = min(N_SEL, t_pad // SLC_BLOCK)

    def page_spec(k):
        return pl.BlockSpec((1, 1, BRANCH_W, TK), lambda i, s, pt: (layer, pt[i, s * n_pg + k], 0, 0))

    row3 = lambda r, w: pl.BlockSpec((1, r, w), lambda i, s, pt: (i, 0, 0))
    grid_spec = pltpu.PrefetchScalarGridSpec(
        num_scalar_prefetch=1,
        grid=(b, n_steps),
        in_specs=[row3(tq, BRANCH_W), row3(tq, BRANCH_W), row3(tq, LANES), row3(tq, BRANCH_W), row3(LANES, TK),
                  pl.BlockSpec((1, 1, wb, LANES), lambda i, s, pt: (layer, i, 0, 0)),
                  row3(tq, LANES),
                  pl.BlockSpec(wc.shape, lambda i, s, pt: (0, 0, 0))]
                 + [page_spec(k) for k in range(n_pg)],
        out_specs=[row3(tq, BRANCH_W), row3(wb, LANES)],
        scratch_shapes=[pltpu.VMEM((t_rows, LANES), F32), pltpu.VMEM((n_steps, LANES, n_pg * TK), F32),
                        pltpu.VMEM((w_rows, LANES), F32)],
    )
    return pl.pallas_call(
        functools.partial(_nsa_decode_kernel, n_pg, past_len, n_cmp, n_top),
        grid_spec=grid_spec,
        out_shape=[jax.ShapeDtypeStruct((b, tq, BRANCH_W), F32), jax.ShapeDtypeStruct((b, wb, LANES), F32)],
        compiler_params=_cparams("parallel", "arbitrary"),
        name="nsa_decode",
    )(page_table, nq3, nqr3, ng3, rown3, rownt3, cache_win4, winn3, wc, *([cache_nsa_t] * n_pg))


def _cmp_weights(w_cmp_k, w_cmp_v):
    z = jnp.zeros_like(w_cmp_k)
    top = jnp.concatenate([w_cmp_k, z], axis=2)
    bot = jnp.concatenate([z, w_cmp_v], axis=2)
    return jnp.concatenate([top, bot], axis=1).astype(BF16)


def _rope_tables(pos):
    posf = pos.astype(F32)[:, None]
    jj = np.arange(LANES) % HEAD_DIM

    def table(half, theta, lanes_on):
        inv = theta ** (-jnp.arange(half, dtype=F32) / half)
        ang = posf * inv[None, :]
        cos_l = jnp.cos(ang)[:, jj % half]
        sin_l = jnp.sin(ang)[:, jj % half] * jnp.where(jj < half, -1.0, 1.0).astype(F32)[None, :]
        on = jnp.asarray((jj < 2 * half) & lanes_on)[None, :]
        return jnp.where(on, cos_l, 1.0), jnp.where(on, sin_l, 0.0)

    every = np.ones(LANES, bool)
    first = np.arange(LANES) < HEAD_DIM
    parts = (*table(HEAD_DIM // 2, RET_THETA, every), *table(ROPE_DIM // 2, ROPE_THETA, every),
             *table(ROPE_DIM // 2, ROPE_THETA, first))
    return jnp.concatenate(parts, axis=1)


def _layer_weights(i, p):
    w_t = p["w_in_t"][:, i, :]
    d = w_t.shape[1]
    c_mg = w_t.shape[0] - N_BRANCH * d
    c_nsa = 8 * BRANCH_W
    w_nsa = w_t[c_nsa:c_mg]
    w_nsa = jnp.pad(w_nsa, ((0, 3 * BRANCH_W - w_nsa.shape[0]), (0, 0)))
    w = {
        "w_ret": w_t[0:4 * BRANCH_W].astype(BF16), "w_u": w_t[4 * BRANCH_W:5 * BRANCH_W].astype(BF16),
        "w_sb": w_t[5 * BRANCH_W:8 * BRANCH_W].astype(BF16), "w_nsa": w_nsa.astype(BF16),
        "w_mg": w_t[c_mg:].astype(BF16), "w_branch": p["w_branch"][i].astype(BF16),
        "w_out": p["w_out"][i].astype(BF16), "ln1_g": p["ln1_g"][i][None, :], "ln1_b": p["ln1_b"][i][None, :],
        "w_up": p["w_ffn_up"][i].astype(BF16), "w_dn": p["w_ffn_down"][i].astype(BF16),
        "ln2_g": p["ln2_g"][i][None, :], "ln2_b": p["ln2_b"][i][None, :],
        "wc": _cmp_weights(p["w_cmp_k"][i], p["w_cmp_v"][i]),
    }
    s5 = _s5_params(p["ssm_a_re"][i], p["ssm_a_im"][i], p["ssm_b_re"][i], p["ssm_b_im"][i], p["ssm_c_re"][i],
                    p["ssm_c_im"][i], p["ssm_d"][i], p["ssm_log_dt"][i], p["w_glu"][i], p["b_glu"][i])
    return w, s5


def _ssm_rows(s):
    b = s.shape[0]
    return jnp.moveaxis(s, -1, 1).reshape(b, 2, -1)


def _ssm_state(rows, g):
    b = rows.shape[0]
    return jnp.moveaxis(rows.reshape(b, 2, g, -1), 1, -1)


def _layer(x, mod, tabs, w, s5, alpha, past):
    b, l, d = x.shape
    t = b * l
    x2 = x.reshape(t, d)
    g = s5["a_re"].shape[-1] // SSM_STATE
    if past is None:
        tm = min(l, 512)
        tpb = l // tm
        mod_in, tab_in = mod, tabs
    else:
        tm, tpb = t, None
        mod_in = jnp.repeat(mod.reshape(b, 8 * d), l, axis=0)
        tab_in = jnp.tile(tabs, (b, 1))
    ret, u, sbq, sbkv, sbkv16, nq, nqr, rows, win, ng, rowsc, rowss16, win16, *rowst16 = _in_call(
        x2, mod_in, tab_in, w, tpb, tm)

    if past is None:
        r0 = jnp.zeros((b, BRANCH_W, BRANCH_W), F32)
        x0 = jnp.zeros((b, 2, g * SSM_STATE), F32)
    else:
        r0 = _bd_from_heads(past["ret"])
        x0 = _ssm_rows(past["ssm"])
    out_a, r_bd = _ret_call(ret, b, l, r0)
    out_b, s_rows = _s5_call(u, b, l, x0, s5)

    if past is None:
        out_c = _sb_prompt_call(sbq, sbkv16, b, l)
        out_d = _nsa_prompt_call(nq, nqr, ng, rowsc, rowss16, rowst16[0], win16, rowst16[1], w["wc"], b, l)
        win_state = win.reshape(b, l, 2, HEAD_DIM)[:, l - min(WINDOW, l):]
    else:
        kvn_t = jnp.swapaxes(jnp.pad(sbkv16.reshape(b, l, 2 * BRANCH_W), ((0, 0), (0, TK - l), (0, 0))), 1, 2)
        out_c = _sb_decode_call(sbq.reshape(b, l, BRANCH_W), kvn_t, past["sb"], past["layer"], past["page_table"])
        rows3 = rows.reshape(b, l, BRANCH_W)
        rown_t = jnp.swapaxes(jnp.pad(rows3[:, :, LANES:], ((0, 0), (0, TK - l), (0, 0))), 1, 2)
        out_d, wst = _nsa_decode_call(nq.reshape(b, l, BRANCH_W), nqr.reshape(b, l, BRANCH_W),
                                      ng.reshape(b, l, LANES), rows3, rown_t,
                                      win.reshape(b, l, LANES), past["nsa"], past["win"], past["layer"],
                                      past["page_table"], w["wc"])
        out_c = out_c.reshape(t, BRANCH_W)
        out_d = out_d.reshape(t, BRANCH_W)
        win_state = wst.reshape(b, -1, 2, HEAD_DIM)

    tm2 = min(tm, 256)
    tpb2 = None if tpb is None else l // tm2
    x1 = _merge_call(x2, mod_in, [out_a, out_b, out_c, out_d], w, tpb2, tm2, alpha)
    xo = _ffn_call(x1, mod_in, w, tpb2, tm2, alpha)
    state = (sbkv.reshape(b, l, 2, N_HEADS, HEAD_DIM), rows.reshape(b, l, 4, HEAD_DIM), win_state,
             _heads_from_bd(r_bd), _ssm_state(s_rows, g))
    return xo.reshape(b, l, d), state


def kernel(x_prompt, x_sample, c_prompt, c_sample, cache_sb, cache_nsa, cache_win, state_ret, state_ssm, page_table, w_ada, b_ada, w_in, ssm_a_re, ssm_a_im, ssm_b_re, ssm_b_im, ssm_c_re, ssm_c_im, ssm_d, ssm_log_dt, w_glu, b_glu, w_cmp_k, w_cmp_v, w_branch, w_out, ln1_g, ln1_b, w_ffn_up, w_ffn_down, ln2_g, ln2_b):
    p = dict(w_in_t=jnp.transpose(w_in, (2, 0, 1)), ssm_a_re=ssm_a_re, ssm_a_im=ssm_a_im, ssm_b_re=ssm_b_re, ssm_b_im=ssm_b_im,
             ssm_c_re=ssm_c_re, ssm_c_im=ssm_c_im, ssm_d=ssm_d, ssm_log_dt=ssm_log_dt, w_glu=w_glu, b_glu=b_glu,
             w_cmp_k=w_cmp_k, w_cmp_v=w_cmp_v, w_branch=w_branch, w_out=w_out, ln1_g=ln1_g, ln1_b=ln1_b,
             w_ffn_up=w_ffn_up, w_ffn_down=w_ffn_down, ln2_g=ln2_g, ln2_b=ln2_b)
    depth, d = w_ada.shape[0], w_ada.shape[1]
    bp, lp, _ = x_prompt.shape
    bs, ls, _ = x_sample.shape
    n_pool, page = cache_sb.shape[1], cache_sb.shape[2]
    past_len = page_table.shape[1] * page
    alpha = (2.0 * depth) ** 0.25

    c_all = jnp.concatenate([c_prompt, c_sample], axis=0)
    m_rows = _round_up(bp + bs, SUBLANES)
    c_all = jnp.pad(c_all, ((0, m_rows - bp - bs), (0, 0)))
    mods = _ada_call(c_all, w_ada, b_ada)

    tabs_p = _rope_tables(jnp.arange(lp, dtype=jnp.int32))
    tabs_s = _rope_tables(past_len + jnp.arange(ls, dtype=jnp.int32))
    cache_sb4 = jnp.transpose(cache_sb, (0, 1, 3, 4, 5, 2)).reshape(depth, n_pool, 2 * BRANCH_W, page)
    cache_nsa4 = jnp.transpose(cache_nsa, (0, 1, 3, 4, 2)).reshape(depth, n_pool, BRANCH_W, page)
    cache_win4 = cache_win.reshape(depth, bs, cache_win.shape[2], LANES)

    xp, xs = x_prompt, x_sample
    st_p = [[] for _ in range(5)]
    st_s = [[] for _ in range(5)]
    for i in range(depth):
        w, s5 = _layer_weights(i, p)
        mod_i = jnp.pad(mods[i].reshape(m_rows, 6, d), ((0, 0), (0, 2), (0, 0)))
        xp, new_p = _layer(xp, mod_i[:bp], tabs_p, w, s5, alpha, None)
        past = dict(sb=cache_sb4, nsa=cache_nsa4, win=cache_win4, ret=state_ret[i], ssm=state_ssm[i],
                    layer=i, page_table=page_table)
        xs, new_s = _layer(xs, mod_i[bp:bp + bs], tabs_s, w, s5, alpha, past)
        for k in range(5):
            st_p[k].append(new_p[k])
            st_s[k].append(new_s[k])
    sb_p, nsa_p, win_p, ret_p, ssm_p = [jnp.stack(s, axis=0) for s in st_p]
    sb_s, nsa_s, win_s, ret_s, ssm_s = [jnp.stack(s, axis=0) for s in st_s]
    return (xp, xs, sb_p, sb_s, nsa_p, nsa_s, win_p, win_s, ret_p, ret_s, ssm_p, ssm_s)
```

```python
import functools
import math

import jax
import jax.numpy as jnp
import numpy as np
from jax import lax
from jax.experimental import pallas as pl
from jax.experimental.pallas import tpu as pltpu

F32 = jnp.float32
BF16 = jnp.bfloat16

LANES = 128
SUBLANES = 8
VMEM_LIMIT_BYTES = 56 * 1024 * 1024

HEAD_DIM = 64
N_HEADS = 4
BRANCH_W = N_HEADS * HEAD_DIM
N_BRANCH = 4
RET_THETA = 10000.0
RET_CHUNK = 128
SSM_GROUP_CH = 16
SSM_STATE = 64
CMP_BLOCK = 32
CMP_STRIDE = 16
SLC_BLOCK = 64
N_SEL = 16
WINDOW = 512
ROPE_THETA = 500000.0
ROPE_DIM = HEAD_DIM // 4
LN_EPS = 1e-5
QK_SCALE = HEAD_DIM ** -0.5
NEG_BIG = -1e30


def _cparams(*sem, flags=None):
    return pltpu.CompilerParams(dimension_semantics=tuple(sem), vmem_limit_bytes=VMEM_LIMIT_BYTES, flags=flags)


def _lane_iota(shape):
    return lax.broadcasted_iota(jnp.int32, shape, len(shape) - 1)


def _row_iota(shape):
    return lax.broadcasted_iota(jnp.int32, shape, len(shape) - 2)


def _dot(a, b):
    return jnp.dot(a.astype(BF16), b.astype(BF16), preferred_element_type=F32)


def _dot_nt(a, b):
    return lax.dot_general(a.astype(BF16), b.astype(BF16), (((1,), (1,)), ((), ())), preferred_element_type=F32)


def _dot_tn(a, b):
    return lax.dot_general(a.astype(BF16), b.astype(BF16), (((0,), (0,)), ((), ())), preferred_element_type=F32)


def _ln_rows(x):
    mu = jnp.mean(x, axis=-1, keepdims=True)
    xc = x - mu
    var = jnp.mean(xc * xc, axis=-1, keepdims=True)
    return xc * lax.rsqrt(var + LN_EPS)


def _sigmoid(x):
    return 1.0 / (1.0 + jnp.exp(-x))


def _silu(x):
    return x * _sigmoid(x)


def _head_mask(shape, h):
    lane = _lane_iota(shape)
    return (lane >= h * HEAD_DIM) & (lane < (h + 1) * HEAD_DIM)


def _ada_kernel(c_ref, w_ref, b_ref, o_ref):
    o_ref[0] = _dot(_silu(c_ref[...]), w_ref[0]) + b_ref[0]


def _ada_call(c_all, w_ada, b_ada):
    depth, d, n = w_ada.shape
    m = c_all.shape[0]
    tn = 1536
    return pl.pallas_call(
        _ada_kernel,
        grid=(depth, n // tn),
        in_specs=[pl.BlockSpec((m, d), lambda i, j: (0, 0)),
                  pl.BlockSpec((1, d, tn), lambda i, j: (i, 0, j)),
                  pl.BlockSpec((1, 1, tn), lambda i, j: (i, 0, j))],
        out_specs=pl.BlockSpec((1, m, tn), lambda i, j: (i, 0, j)),
        out_shape=jax.ShapeDtypeStruct((depth, m, n), F32),
        compiler_params=_cparams("parallel", "parallel"),
        name="ada_mod",
    )(c_all, w_ada, b_ada.reshape(depth, 1, n))


def _rope128(x, cos, sin_signed, half):
    first = (_lane_iota(x.shape) % HEAD_DIM) < half
    partner = jnp.where(first, pltpu.roll(x, LANES - half, 1), pltpu.roll(x, half, 1))
    return x * cos + partner * sin_signed


def _mod_rows(m_ref, row, per_token):
    if per_token:
        d = m_ref.shape[-1] // 8
        return m_ref[:, row * d:(row + 1) * d]
    return m_ref[0, row:row + 1, :]


def _in_kernel(per_token, x_ref, m_ref, wr_ref, wu_ref, wsb_ref, wn_ref, tab_ref,
               ret_ref, u_ref, sbq_ref, sbkv_ref, sbkv16_ref, nq_ref, nqr_ref, rows_ref, win_ref, ng_ref,
               rowsc_ref, rowss16_ref, win16_ref, *maybe_rowst16_ref):
    x = x_ref[...]
    h = (_ln_rows(x) * (1.0 + _mod_rows(m_ref, 1, per_token)) + _mod_rows(m_ref, 0, per_token)).astype(BF16)
    tab = tab_ref[...]
    rc, rs = tab[:, 0:128], tab[:, 128:256]
    qc, qs = tab[:, 256:384], tab[:, 384:512]
    kc, ks = tab[:, 512:640], tab[:, 640:768]

    zr = _dot_nt(h, wr_ref[...])
    half_r = HEAD_DIM // 2
    for c in range(2):
        ret_ref[:, c * 128:(c + 1) * 128] = _rope128(zr[:, c * 128:(c + 1) * 128], rc, rs, half_r)
    for c in range(2, 4):
        ret_ref[:, c * 128:(c + 1) * 128] = _rope128(zr[:, c * 128:(c + 1) * 128], rc, rs, half_r) * QK_SCALE
    ret_ref[:, 512:1024] = zr[:, 512:1024]

    u_ref[...] = _dot_nt(h, wu_ref[...])

    zs = _dot_nt(h, wsb_ref[...])
    sbq_ref[...] = (zs[:, 0:256] * QK_SCALE).astype(BF16)
    sbkv_ref[...] = zs[:, 256:768]
    sbkv16_ref[...] = zs[:, 256:768].astype(BF16)

    zn = _dot_nt(h, wn_ref[...])
    half_n = ROPE_DIM // 2
    nq_ref[...] = (zn[:, 0:256] * QK_SCALE).astype(BF16)
    for c in range(2):
        nqr_ref[:, c * 128:(c + 1) * 128] = (
            _rope128(zn[:, c * 128:(c + 1) * 128], qc, qs, half_n) * QK_SCALE).astype(BF16)
    rows_slc = _rope128(zn[:, 384:512], kc, ks, half_n)
    rows_ref[:, 0:128] = zn[:, 256:384]
    rows_ref[:, 128:256] = rows_slc
    rowsc_ref[...] = zn[:, 256:384]
    rowss16_ref[...] = rows_slc.astype(BF16)
    if maybe_rowst16_ref:
        maybe_rowst16_ref[0][...] = rows_slc.T.astype(BF16)
    win = _rope128(zn[:, 512:640], kc, ks, half_n)
    win_ref[...] = win
    win16_ref[...] = win.astype(BF16)
    if maybe_rowst16_ref:
        maybe_rowst16_ref[1][...] = win.T.astype(BF16)
    ng_ref[...] = _sigmoid(zn[:, 640:768])


def _in_call(x2, mod, tabs, w, tiles_per_batch, tm):
    t, d = x2.shape
    per_token = tiles_per_batch is None
    nt = t // tm
    if per_token:
        mod_spec = pl.BlockSpec((tm, 8 * d), lambda i: (i, 0))
        tab_spec = pl.BlockSpec((tm, 768), lambda i: (i, 0))
    else:
        mod_spec = pl.BlockSpec((1, 8, d), lambda i: (i // tiles_per_batch, 0, 0))
        tab_spec = pl.BlockSpec((tm, 768), lambda i: (i % tiles_per_batch, 0))

    def wspec(a):
        return pl.BlockSpec(a.shape, lambda i: (0, 0))

    def ospec(n):
        return pl.BlockSpec((tm, n), lambda i: (i, 0))

    outs = [(1024, F32), (256, F32), (256, BF16), (512, F32), (512, BF16), (256, BF16), (256, BF16),
            (256, F32), (128, F32), (128, F32), (128, F32), (128, BF16), (128, BF16)]
    return pl.pallas_call(
        functools.partial(_in_kernel, per_token),
        grid=(nt,),
        in_specs=[pl.BlockSpec((tm, d), lambda i: (i, 0)), mod_spec,
                  wspec(w["w_ret"]), wspec(w["w_u"]), wspec(w["w_sb"]), wspec(w["w_nsa"]), tab_spec],
        out_specs=[ospec(n) for n, _ in outs] + ([] if per_token else [pl.BlockSpec((LANES, tm), lambda i: (0, i))] * 2),
        out_shape=[jax.ShapeDtypeStruct((t, n), dt) for n, dt in outs]
        + ([] if per_token else [jax.ShapeDtypeStruct((LANES, t), BF16)] * 2),
        compiler_params=_cparams("parallel"),
        name="in_proj",
    )(x2, mod, w["w_ret"], w["w_u"], w["w_sb"], w["w_nsa"], tabs)


def _merge_kernel(per_token, alpha, x_ref, m_ref, a_ref, b_ref, c_ref, d_ref, wmg_ref, wbr_ref, wout_ref,
                  g_ref, bias_ref, o_ref):
    x = x_ref[...]
    dm = x.shape[-1]
    h = (_ln_rows(x) * (1.0 + _mod_rows(m_ref, 1, per_token)) + _mod_rows(m_ref, 0, per_token)).astype(BF16)
    acc = jnp.zeros(x.shape, F32)
    for n, br_ref in enumerate((a_ref, b_ref, c_ref, d_ref)):
        mg = _dot_nt(h, wmg_ref[n * dm:(n + 1) * dm, :])
        br = jnp.dot(br_ref[...].astype(BF16), wbr_ref[n], preferred_element_type=F32)
        acc = acc + _sigmoid(mg) * br
    y = jnp.dot(acc.astype(BF16), wout_ref[...], preferred_element_type=F32)
    r = alpha * x + (1.0 + _mod_rows(m_ref, 2, per_token)) * y
    o_ref[...] = _ln_rows(r) * g_ref[...] + bias_ref[...]


def _merge_call(x2, mod, branches, w, tiles_per_batch, tm, alpha):
    t, d = x2.shape
    per_token = tiles_per_batch is None
    if per_token:
        mod_spec = pl.BlockSpec((tm, 8 * d), lambda i: (i, 0))
    else:
        mod_spec = pl.BlockSpec((1, 8, d), lambda i: (i // tiles_per_batch, 0, 0))
    row = lambda n: pl.BlockSpec((tm, n), lambda i: (i, 0))
    return pl.pallas_call(
        functools.partial(_merge_kernel, per_token, alpha),
        grid=(t // tm,),
        in_specs=[row(d), mod_spec, row(BRANCH_W), row(BRANCH_W), row(BRANCH_W), row(BRANCH_W),
                  pl.BlockSpec(w["w_mg"].shape, lambda i: (0, 0)),
                  pl.BlockSpec(w["w_branch"].shape, lambda i: (0, 0, 0)),
                  pl.BlockSpec(w["w_out"].shape, lambda i: (0, 0)),
                  pl.BlockSpec((1, d), lambda i: (0, 0)), pl.BlockSpec((1, d), lambda i: (0, 0))],
        out_specs=row(d),
        out_shape=jax.ShapeDtypeStruct((t, d), F32),
        compiler_params=_cparams("parallel"),
        name="merge_out",
    )(x2, mod, *branches, w["w_mg"], w["w_branch"], w["w_out"], w["ln1_g"], w["ln1_b"])


def _ffn_kernel(per_token, alpha, n_chunk, x_ref, m_ref, wup_ref, wdn_ref, g_ref, bias_ref, o_ref):
    x = x_ref[...]
    dff = wdn_ref.shape[0]
    ck = dff // n_chunk
    h = (_ln_rows(x) * (1.0 + _mod_rows(m_ref, 4, per_token)) + _mod_rows(m_ref, 3, per_token)).astype(BF16)
    f = jnp.zeros(x.shape, F32)
    for c in range(n_chunk):
        a = jnp.dot(h, wup_ref[:, c * ck:(c + 1) * ck], preferred_element_type=F32)
        u = jnp.dot(h, wup_ref[:, dff + c * ck:dff + (c + 1) * ck], preferred_element_type=F32)
        f = f + jnp.dot((_silu(a) * u).astype(BF16), wdn_ref[c * ck:(c + 1) * ck, :], preferred_element_type=F32)
    r = alpha * x + (1.0 + _mod_rows(m_ref, 5, per_token)) * f
    o_ref[...] = _ln_rows(r) * g_ref[...] + bias_ref[...]


def _ffn_call(x2, mod, w, tiles_per_batch, tm, alpha):
    t, d = x2.shape
    per_token = tiles_per_batch is None
    if per_token:
        mod_spec = pl.BlockSpec((tm, 8 * d), lambda i: (i, 0))
    else:
        mod_spec = pl.BlockSpec((1, 8, d), lambda i: (i // tiles_per_batch, 0, 0))
    dff = w["w_dn"].shape[0]
    n_chunk = 2 if dff % 256 == 0 else 1
    return pl.pallas_call(
        functools.partial(_ffn_kernel, per_token, alpha, n_chunk),
        grid=(t // tm,),
        in_specs=[pl.BlockSpec((tm, d), lambda i: (i, 0)), mod_spec,
                  pl.BlockSpec(w["w_up"].shape, lambda i: (0, 0)),
                  pl.BlockSpec(w["w_dn"].shape, lambda i: (0, 0)),
                  pl.BlockSpec((1, d), lambda i: (0, 0)), pl.BlockSpec((1, d), lambda i: (0, 0))],
        out_specs=pl.BlockSpec((tm, d), lambda i: (i, 0)),
        out_shape=jax.ShapeDtypeStruct((t, d), F32),
        compiler_params=_cparams("parallel"),
        name="ffn",
    )(x2, mod, w["w_up"], w["w_dn"], w["ln2_g"], w["ln2_b"])


def _ret_kernel(chunk, q_ref, k_ref, v_ref, g_ref, r0_ref, intra_ref, qdec_ref, kdec_ref, cdec_ref,
                o_ref, rout_ref, r_s):
    j = pl.program_id(1)

    @pl.when(j == 0)
    def _():
        r_s[...] = r0_ref[0]

    q = q_ref[...]
    k = k_ref[...]
    v16 = v_ref[...].astype(BF16)
    k16 = k.astype(BF16)
    o = _dot(q * qdec_ref[...], r_s[...])
    for h in range(N_HEADS):
        hm = _head_mask(q.shape, h)
        s = _dot_nt(jnp.where(hm, q, 0.0), k16) * intra_ref[h]
        o = o + jnp.where(hm, _dot(s, v16), 0.0)
    upd = _dot_tn(k * kdec_ref[...], v16)
    rr = _row_iota(upd.shape) // HEAD_DIM
    cc = _lane_iota(upd.shape) // HEAD_DIM
    r_new = r_s[...] * cdec_ref[...] + jnp.where(rr == cc, upd, 0.0)
    r_s[...] = r_new

    @pl.when(j == pl.num_programs(1) - 1)
    def _():
        rout_ref[0] = r_new

    mu = jnp.zeros(o.shape, F32)
    for h in range(N_HEADS):
        hm = _head_mask(o.shape, h)
        mu = mu + jnp.where(hm, jnp.sum(jnp.where(hm, o, 0.0), axis=-1, keepdims=True), 0.0)
    oc = o - mu * (1.0 / HEAD_DIM)
    var = jnp.zeros(o.shape, F32)
    oc2 = oc * oc
    for h in range(N_HEADS):
        hm = _head_mask(o.shape, h)
        var = var + jnp.where(hm, jnp.sum(jnp.where(hm, oc2, 0.0), axis=-1, keepdims=True), 0.0)
    o_ref[...] = oc * lax.rsqrt(var * (1.0 / HEAD_DIM) + LN_EPS) * _silu(g_ref[...])


def _ret_tables(chunk):
    h = jnp.arange(N_HEADS, dtype=F32)
    log_g = jnp.log1p(-jnp.exp2(-5.0 - h))
    idx = jnp.arange(chunk, dtype=F32)
    diff = idx[:, None] - idx[None, :]
    intra = jnp.where(diff >= 0, jnp.exp(jnp.maximum(diff, 0.0)[None] * log_g[:, None, None]), 0.0)
    q_dec = jnp.exp((idx + 1.0)[None, :] * log_g[:, None])
    k_dec = jnp.exp((chunk - 1.0 - idx)[None, :] * log_g[:, None])
    c_dec = jnp.exp(chunk * log_g)
    lanes = lambda t: jnp.repeat(t.T, HEAD_DIM, axis=1)
    return intra, lanes(q_dec), lanes(k_dec), jnp.repeat(c_dec, HEAD_DIM)[None, :]


def _ret_call(ret, b, l, r0_bd):
    chunk = RET_CHUNK if l % RET_CHUNK == 0 else l
    n = l // chunk
    intra, q_dec, k_dec, c_dec = _ret_tables(chunk)
    col = lambda c: pl.BlockSpec((chunk, BRANCH_W), lambda i, j: (i * n + j, c))
    full = lambda a: pl.BlockSpec(a.shape, lambda i, j: (0,) * a.ndim)
    return pl.pallas_call(
        functools.partial(_ret_kernel, chunk),
        grid=(b, n),
        in_specs=[col(0), col(1), col(2), col(3),
                  pl.BlockSpec((1, BRANCH_W, BRANCH_W), lambda i, j: (i, 0, 0)),
                  full(intra), full(q_dec), full(k_dec), full(c_dec)],
        out_specs=[pl.BlockSpec((chunk, BRANCH_W), lambda i, j: (i * n + j, 0)),
                   pl.BlockSpec((1, BRANCH_W, BRANCH_W), lambda i, j: (i, 0, 0))],
        out_shape=[jax.ShapeDtypeStruct((b * l, BRANCH_W), F32),
                   jax.ShapeDtypeStruct((b, BRANCH_W, BRANCH_W), F32)],
        scratch_shapes=[pltpu.VMEM((BRANCH_W, BRANCH_W), F32)],
        compiler_params=_cparams("parallel", "arbitrary"),
        name="retention",
    )(ret, ret, ret, ret, r0_bd, intra, q_dec, k_dec, c_dec)


def _bd_from_heads(r):
    b = r.shape[0]
    eye = jnp.eye(N_HEADS, dtype=r.dtype)
    return jnp.einsum("bhij,hg->bhigj", r, eye).reshape(b, BRANCH_W, BRANCH_W)


def _heads_from_bd(r_bd):
    b = r_bd.shape[0]
    r5 = r_bd.reshape(b, N_HEADS, HEAD_DIM, N_HEADS, HEAD_DIM)
    return jnp.stack([r5[:, h, :, h, :] for h in range(N_HEADS)], axis=1)


def _cmul(ar, ai, br, bi):
    return ar * br - ai * bi, ar * bi + ai * br


def _gelu_tanh(x):
    return 0.5 * x * (1.0 + jnp.tanh(math.sqrt(2.0 / math.pi) * (x + 0.044715 * (x * x * x))))


def _s5_kernel(chain, u_ref, x0_ref, are_ref, aim_ref, ldt_ref, bre_ref, bim_ref, cre_ref, cim_ref, dsk_ref,
               wglu_ref, bglu_ref, o_ref, st_ref, xr_s, xi_s, cr_s, ci_s):
    rows = u_ref.shape[0]
    n_grp = rows // SUBLANES
    j = pl.program_id(1)

    ar, ai = are_ref[...], aim_ref[...]
    dt = jnp.exp(ldt_ref[...])
    mag = jnp.exp(ar * dt)
    abr, abi = mag * jnp.cos(ai * dt), mag * jnp.sin(ai * dt)
    nr, ni = abr - 1.0, abi
    den = ar * ar + ai * ai
    fr = (nr * ar + ni * ai) / den
    fi = (ni * ar - nr * ai) / den
    bbr = fr * bre_ref[...] - fi * bim_ref[...]
    bbi = fr * bim_ref[...] + fi * bre_ref[...]

    u = u_ref[...]
    u16 = u.astype(BF16)
    xr_s[...] = jnp.dot(u16, bbr.astype(BF16), preferred_element_type=F32)
    xi_s[...] = jnp.dot(u16, bbi.astype(BF16), preferred_element_type=F32)

    a2r, a2i = _cmul(abr, abi, abr, abi)
    a4r, a4i = _cmul(a2r, a2i, a2r, a2i)
    row = _row_iota((SUBLANES, abr.shape[-1]))
    pr, pi_ = jnp.broadcast_to(abr, row.shape), jnp.broadcast_to(abi, row.shape)
    qr, qi = abr, abi
    for i in range(1, SUBLANES):
        qr, qi = _cmul(qr, qi, abr, abi)
        pr = jnp.where(row == i, qr, pr)
        pi_ = jnp.where(row == i, qi, pi_)
    steps = ((1, abr, abi), (2, a2r, a2i), (4, a4r, a4i))

    if chain:
        @pl.when(j == 0)
        def _():
            cr_s[...] = x0_ref[0, 0:1, :]
            ci_s[...] = x0_ref[0, 1:2, :]

    def group(g, carry):
        r0 = pl.multiple_of(g * SUBLANES, SUBLANES)
        xr = xr_s[pl.ds(r0, SUBLANES), :]
        xi = xi_s[pl.ds(r0, SUBLANES), :]
        for d, er, ei in steps:
            sr = jnp.where(row >= d, pltpu.roll(xr, d, 0), 0.0)
            si = jnp.where(row >= d, pltpu.roll(xi, d, 0), 0.0)
            tr, ti = _cmul(er, ei, sr, si)
            xr, xi = xr + tr, xi + ti
        if chain:
            c_r, c_i = cr_s[...], ci_s[...]
        else:
            c_r, c_i = x0_ref[g, 0:1, :], x0_ref[g, 1:2, :]
        tr, ti = _cmul(pr, pi_, c_r, c_i)
        xr, xi = xr + tr, xi + ti
        xr_s[pl.ds(r0, SUBLANES), :] = xr
        xi_s[pl.ds(r0, SUBLANES), :] = xi
        if chain:
            cr_s[...] = xr[SUBLANES - 1:SUBLANES, :]
            ci_s[...] = xi[SUBLANES - 1:SUBLANES, :]
        else:
            st_ref[g, 0:1, :] = xr[SUBLANES - 1:SUBLANES, :]
            st_ref[g, 1:2, :] = xi[SUBLANES - 1:SUBLANES, :]
        return carry

    lax.fori_loop(0, n_grp, group, 0)

    if chain:
        @pl.when(j == pl.num_programs(1) - 1)
        def _():
            st_ref[0, 0:1, :] = cr_s[...]
            st_ref[0, 1:2, :] = ci_s[...]

    y = (jnp.dot(xr_s[...].astype(BF16), cre_ref[...], preferred_element_type=F32)
         - jnp.dot(xi_s[...].astype(BF16), cim_ref[...], preferred_element_type=F32) + dsk_ref[...] * u)
    zb = _gelu_tanh(y)
    o_ref[...] = zb * _sigmoid(jnp.dot(zb.astype(BF16), wglu_ref[...], preferred_element_type=F32) + bglu_ref[...])


def _s5_call(u2, b, l, x0, w):
    n_state = x0.shape[-1]
    chain = l % 128 == 0
    if chain:
        tl = min(l, 512)
        grid = (b, l // tl)
        rows = tl
        u_spec = pl.BlockSpec((tl, BRANCH_W), lambda i, j: (i * (l // tl) + j, 0))
        x0_spec = pl.BlockSpec((1, 2, n_state), lambda i, j: (i, 0, 0))
    else:
        assert l == SUBLANES
        grid = (1, 1)
        rows = b * l
        u_spec = pl.BlockSpec((rows, BRANCH_W), lambda i, j: (0, 0))
        x0_spec = pl.BlockSpec((b, 2, n_state), lambda i, j: (0, 0, 0))
    full = lambda a: pl.BlockSpec(a.shape, lambda i, j: (0,) * a.ndim)
    params = [w["a_re"], w["a_im"], w["log_dt"], w["b_re_bd"], w["b_im_bd"], w["c_re_bd"], w["c_im_bd"],
              w["d_skip"], w["w_glu"], w["b_glu"]]
    return pl.pallas_call(
        functools.partial(_s5_kernel, chain),
        grid=grid,
        in_specs=[u_spec, x0_spec] + [full(a) for a in params],
        out_specs=[u_spec, x0_spec],
        out_shape=[jax.ShapeDtypeStruct((b * l, BRANCH_W), F32), jax.ShapeDtypeStruct(x0.shape, F32)],
        scratch_shapes=[pltpu.VMEM((rows, n_state), F32), pltpu.VMEM((rows, n_state), F32),
                        pltpu.VMEM((1, n_state), F32), pltpu.VMEM((1, n_state), F32)],
        compiler_params=_cparams("parallel", "arbitrary"),
        name="s5",
    )(u2, x0, *params)


def _s5_params(a_re, a_im, b_re, b_im, c_re, c_im, d_skip, log_dt, w_glu, b_glu):
    g, p = a_re.shape
    cg = b_re.shape[-1]
    eye = jnp.eye(g, dtype=F32)
    b_bd = lambda t: jnp.einsum("gpc,gh->gchp", t, eye).reshape(g * cg, g * p)
    c_bd = lambda t: jnp.einsum("gcp,gh->gphc", t, eye).reshape(g * p, g * cg)
    return {
        "a_re": a_re.reshape(1, g * p), "a_im": a_im.reshape(1, g * p),
        "log_dt": jnp.repeat(log_dt, p).reshape(1, g * p),
        "b_re_bd": b_bd(b_re), "b_im_bd": b_bd(b_im),
        "c_re_bd": c_bd(c_re).astype(BF16), "c_im_bd": c_bd(c_im).astype(BF16),
        "d_skip": d_skip.reshape(1, g * cg), "w_glu": w_glu.astype(BF16), "b_glu": b_glu.reshape(1, -1),
    }


TK = 128


def _stack_heads_bd(q):
    qf = q.astype(F32)
    return jnp.concatenate([jnp.where(_head_mask(qf.shape, h), qf, 0.0) for h in range(N_HEADS)], axis=0).astype(BF16)


def _unstack_heads_bd(acc, tq):
    out = jnp.zeros((tq, acc.shape[-1]), F32)
    for h in range(N_HEADS):
        blk = acc[h * tq:(h + 1) * tq]
        out = out + jnp.where(_head_mask(blk.shape, h), blk, 0.0)
    return out


def _suffix_matrix(tk):
    r = _row_iota((2 * tk, tk)) % tk
    c = _lane_iota((2 * tk, tk))
    return jnp.where(r > c, 1.0, 0.0).astype(BF16)


def _sb_scores(qbd, k16, transposed):
    return jnp.dot(qbd, k16, preferred_element_type=F32) if transposed else _dot_nt(qbd, k16)


def _sb_tile(qbd, k16, v16, mask, u2, carry, acc, transposed):
    return _sb_tile_z(_sb_scores(qbd, k16, transposed), v16, mask, u2, carry, acc, transposed)


def _sb_logw(z, mask, u2):
    log_beta = jnp.minimum(z, 0.0) - jnp.log(1.0 + jnp.exp(-jnp.abs(z)))
    l1m = log_beta - z
    if mask is not None:
        l1m = jnp.where(mask, l1m, 0.0)
    hi = lax.bitcast_convert_type(lax.bitcast_convert_type(l1m, jnp.uint32) & jnp.uint32(0xFFFF0000), F32)
    lo = l1m - hi
    suffix = jnp.dot(jnp.concatenate([hi.astype(BF16), lo.astype(BF16)], axis=1), u2, preferred_element_type=F32)
    return log_beta + suffix, suffix[:, 0:1] + l1m[:, 0:1]


def _sb_accumulate(logw, total, v16, mask, carry, acc, transposed):
    w = jnp.exp(logw + carry)
    if mask is not None:
        w = jnp.where(mask, w, 0.0)
    w16 = w.astype(BF16)
    acc = acc + (_dot_nt(w16, v16) if transposed else jnp.dot(w16, v16, preferred_element_type=F32))
    return carry + total, acc


def _sb_tile_z(z, v16, mask, u2, carry, acc, transposed):
    logw, total = _sb_logw(z, mask, u2)
    return _sb_accumulate(logw, total, v16, mask, carry, acc, transposed)


TKP = 256
SB_DEAD_LOG = -104.0


def _sb_prompt_kernel(q_ref, kv_ref, o_ref):
    j = pl.program_id(1)
    tq = q_ref.shape[0]
    qbd = _stack_heads_bd(q_ref[...])
    u2 = _suffix_matrix(TKP)
    rows = N_HEADS * tq
    n_full = (j * tq) // TKP
    qpos = j * tq + _row_iota((rows, TKP)) % tq
    diag_mask = (n_full * TKP + _lane_iota((rows, TKP))) < qpos

    def keys(kt):
        return kv_ref[pl.ds(pl.multiple_of(kt * TKP, TKP), TKP), 0:BRANCH_W]

    def values(kt):
        return kv_ref[pl.ds(pl.multiple_of(kt * TKP, TKP), TKP), BRANCH_W:2 * BRANCH_W]

    carry, acc = _sb_tile(qbd, keys(n_full), values(n_full), diag_mask, u2,
                          jnp.zeros((rows, 1), F32), jnp.zeros((rows, BRANCH_W), F32), False)

    def cond(st):
        it, carry, _ = st
        return (it < n_full) & (jnp.max(carry) >= SB_DEAD_LOG)

    def body(st):
        it, carry, acc = st
        kt = n_full - 1 - it
        carry, acc = _sb_tile(qbd, keys(kt), values(kt), None, u2, carry, acc, False)
        return it + 1, carry, acc

    _, carry, acc = lax.while_loop(cond, body, (jnp.int32(0), carry, acc))
    o_ref[...] = _unstack_heads_bd(acc, tq)


def _sb_prompt_call(sbq, sbkv16, b, l):
    tq = TK
    n = l // tq
    assert l % TKP == 0
    return pl.pallas_call(
        _sb_prompt_kernel,
        grid=(b, n),
        in_specs=[pl.BlockSpec((tq, BRANCH_W), lambda i, j: (i * n + j, 0)),
                  pl.BlockSpec((l, 2 * BRANCH_W), lambda i, j: (i, 0))],
        out_specs=pl.BlockSpec((tq, BRANCH_W), lambda i, j: (i * n + j, 0)),
        out_shape=jax.ShapeDtypeStruct((b * l, BRANCH_W), F32),
        compiler_params=_cparams("parallel", "arbitrary"),
        name="sb_prompt",
    )(sbq, sbkv16)


PAGES_PER_STEP = 8


def _sb_decode_kernel(n_pg, pt_ref, q_ref, kvn_ref, *rest):
    page_refs = rest[:n_pg]
    o_ref, carry_s, acc_s = rest[n_pg:]
    s = pl.program_id(1)
    tq = q_ref.shape[1]
    rows = N_HEADS * tq
    qbd = _stack_heads_bd(q_ref[0])
    u2 = _suffix_matrix(TK)

    @pl.when(s == 0)
    def _():
        mask = _lane_iota((rows, TK)) < (_row_iota((rows, TK)) % tq)
        kvn = kvn_ref[0]
        carry, acc = _sb_tile(qbd, kvn[0:BRANCH_W, :], kvn[BRANCH_W:, :], mask, u2,
                              jnp.zeros((rows, 1), F32), jnp.zeros((rows, BRANCH_W), F32), True)
        carry_s[...] = carry
        acc_s[...] = acc

    @pl.when(jnp.max(carry_s[...]) >= SB_DEAD_LOG)
    def _():
        carry, acc = carry_s[...], acc_s[...]
        group = 2 if n_pg % 2 == 0 else 1
        u2g = _suffix_matrix(group * TK) if group > 1 else u2
        for k in range(0, n_pg, group):
            kv = jnp.concatenate([page_refs[k + g][0, 0] for g in reversed(range(group))], axis=1).astype(BF16)
            carry, acc = _sb_tile(qbd, kv[0:BRANCH_W, :], kv[BRANCH_W:, :], None, u2g, carry, acc, True)
        carry_s[...] = carry
        acc_s[...] = acc

    @pl.when(s == pl.num_programs(1) - 1)
    def _():
        o_ref[0] = _unstack_heads_bd(acc_s[...], tq)


def _sb_decode_call(sbq3, kvn_t16, cache_t, layer, page_table):
    b, tq, _ = sbq3.shape
    n_pages = page_table.shape[1]
    n_pg = math.gcd(PAGES_PER_STEP, n_pages)
    n_steps = n_pages // n_pg

    def page_spec(k):
        return pl.BlockSpec((1, 1, 2 * BRANCH_W, TK),
                            lambda i, s, pt: (layer, pt[i, n_pages - 1 - (s * n_pg + k)], 0, 0))

    grid_spec = pltpu.PrefetchScalarGridSpec(
        num_scalar_prefetch=1,
        grid=(b, n_steps),
        in_specs=[pl.BlockSpec((1, tq, BRANCH_W), lambda i, s, pt: (i, 0, 0)),
                  pl.BlockSpec((1, 2 * BRANCH_W, TK), lambda i, s, pt: (i, 0, 0))]
                 + [page_spec(k) for k in range(n_pg)],
        out_specs=pl.BlockSpec((1, tq, BRANCH_W), lambda i, s, pt: (i, 0, 0)),
        scratch_shapes=[pltpu.VMEM((N_HEADS * tq, 1), F32), pltpu.VMEM((N_HEADS * tq, BRANCH_W), F32)],
    )
    return pl.pallas_call(
        functools.partial(_sb_decode_kernel, n_pg),
        grid_spec=grid_spec,
        out_shape=jax.ShapeDtypeStruct((b, tq, BRANCH_W), F32),
        compiler_params=_cparams("parallel", "arbitrary"),
        name="sb_decode",
    )(page_table, sbq3, kvn_t16, *([cache_t] * n_pg))


def _round_up(x, m):
    return -(-x // m) * m


def _stack_heads_128(q):
    qf = q.astype(F32)
    tq = qf.shape[0]
    low = _lane_iota((tq, LANES)) < HEAD_DIM
    parts = []
    for h in range(N_HEADS):
        half = qf[:, LANES * (h // 2):LANES * (h // 2 + 1)]
        if h % 2 == 1:
            half = pltpu.roll(half, HEAD_DIM, 1)
        parts.append(jnp.where(low, half, 0.0))
    return jnp.concatenate(parts, axis=0)


def _compress(rows_ref, wc_ref, n):
    p0 = jnp.zeros((n, LANES), F32)
    p1 = jnp.zeros((n, LANES), F32)
    for s in range(CMP_STRIDE):
        x = rows_ref[pl.ds(s, n, stride=CMP_STRIDE), :].astype(BF16)
        p0 = p0 + jnp.dot(x, wc_ref[s], preferred_element_type=F32)
        p1 = p1 + jnp.dot(x, wc_ref[CMP_STRIDE + s], preferred_element_type=F32)
    return p0 + pltpu.roll(p1, n - 1, 0)


def _compress_chunks(xs_ref, wcc_ref):
    n = xs_ref.shape[1]
    p = jnp.zeros((n, 2 * LANES), F32)
    for i in range(CMP_STRIDE // 2):
        p = p + jnp.dot(xs_ref[i].astype(BF16), wcc_ref[i], preferred_element_type=F32)
    return p[:, 0:LANES] + pltpu.roll(p[:, LANES:2 * LANES], n - 1, 0)


def _cmp_pairs(wc):
    first, second = wc[:CMP_STRIDE], wc[CMP_STRIDE:]
    both = jnp.concatenate([first, second], axis=2)
    return both.reshape(CMP_STRIDE // 2, 2 * LANES, 2 * LANES)


def _cmp_select(qp16, kvc, qpos_rows, qpos_q, tq, nsp, n_top):
    n = kvc.shape[0]
    kvc16 = kvc.astype(BF16)
    s = _dot_nt(qp16, kvc16)
    cmask = (_lane_iota(s.shape) * CMP_STRIDE + (CMP_BLOCK - 1)) <= qpos_rows
    s = jnp.where(cmask, s, NEG_BIG)
    m = jnp.max(s, axis=-1, keepdims=True)
    m = jnp.where(m > 0.5 * NEG_BIG, m, 0.0)
    e = jnp.where(cmask, jnp.exp(s - m), 0.0)
    p16 = (e / jnp.maximum(jnp.sum(e, axis=-1, keepdims=True), 1e-30)).astype(BF16)
    o_cmp = jnp.dot(p16, kvc16, preferred_element_type=F32)
    ci = _row_iota((n, nsp)) * CMP_STRIDE
    sj = _lane_iota((n, nsp)) * SLC_BLOCK
    c2s = jnp.where((ci < sj + SLC_BLOCK) & (sj < ci + CMP_BLOCK), 1.0, 0.0).astype(BF16)
    imp4 = jnp.dot(p16, c2s, preferred_element_type=F32)
    imp = imp4[0:tq]
    for h in range(1, N_HEADS):
        imp = imp + imp4[h * tq:(h + 1) * tq]

    blk = _lane_iota((tq, nsp))
    blkf = blk.astype(F32)
    cur = qpos_q // SLC_BLOCK
    forced = (blk == 0) | (blk == cur) | (blk == cur - 1)
    val = jnp.where(forced, jnp.inf, imp)
    avail = jnp.where(blk * SLC_BLOCK <= qpos_q, 1.0, 0.0)
    sel = jnp.zeros((tq, nsp), F32)
    for _ in range(n_top):
        vm = jnp.where(avail > 0.5, val, -jnp.inf)
        mx = jnp.max(vm, axis=-1, keepdims=True)
        cand = (avail > 0.5) & (vm == mx)
        first = jnp.min(jnp.where(cand, blkf, float(nsp)), axis=-1, keepdims=True)
        pick = blkf == first
        sel = jnp.where(pick, 1.0, sel)
        avail = jnp.where(pick, 0.0, avail)
    return o_cmp, sel.astype(BF16)


def _cmp_select_t(qp_t16, kvc, kvc_t, c2s_t16, j, tq, n_top):
    n = kvc.shape[0]
    nsp = c2s_t16.shape[0]
    rows4 = N_HEADS * tq
    qpos_cols = j * tq + _lane_iota((n, rows4)) % tq
    s = jnp.dot(kvc.astype(BF16), qp_t16, preferred_element_type=F32)
    cmask = (_row_iota((n, rows4)) * CMP_STRIDE + (CMP_BLOCK - 1)) <= qpos_cols
    s = jnp.where(cmask, s, NEG_BIG)
    m = jnp.max(s, axis=0, keepdims=True)
    m = jnp.where(m > 0.5 * NEG_BIG, m, 0.0)
    e = jnp.where(cmask, jnp.exp(s - m), 0.0)
    p16 = (e / jnp.maximum(jnp.sum(e, axis=0, keepdims=True), 1e-30)).astype(BF16)
    o_cmp_t = jnp.dot(kvc_t.astype(BF16), p16, preferred_element_type=F32)
    imp4 = jnp.dot(c2s_t16, p16, preferred_element_type=F32)
    imp = imp4[:, 0:tq]
    for h in range(1, N_HEADS):
        imp = imp + imp4[:, h * tq:(h + 1) * tq]

    blk = _row_iota((nsp, tq))
    blkf = blk.astype(F32)
    qpos = j * tq + _lane_iota((nsp, tq))
    cur = qpos // SLC_BLOCK
    forced = (blk == 0) | (blk == cur) | (blk == cur - 1)
    val = jnp.where(forced, jnp.inf, imp)
    avail = jnp.where(blk * SLC_BLOCK <= qpos, 1.0, 0.0)
    sel = jnp.zeros((nsp, tq), F32)
    for _ in range(n_top):
        vm = jnp.where(avail > 0.5, val, -jnp.inf)
        mx = jnp.max(vm, axis=0, keepdims=True)
        cand = (avail > 0.5) & (vm == mx)
        first = jnp.min(jnp.where(cand, blkf, float(nsp)), axis=0, keepdims=True)
        pick = blkf == first
        sel = jnp.where(pick, 1.0, sel)
        avail = jnp.where(pick, 0.0, avail)
    return o_cmp_t, sel


def _attn_tile(q16, kv16, mask, m, l, acc, transposed=False):
    s = jnp.dot(q16, kv16, preferred_element_type=F32) if transposed else _dot_nt(q16, kv16)
    s = jnp.where(mask, s, NEG_BIG)
    m_new = jnp.maximum(m, jnp.max(s, axis=-1, keepdims=True))
    p = jnp.where(mask, jnp.exp(s - m_new), 0.0)
    alpha = jnp.exp(m - m_new)
    l = alpha * l + jnp.sum(p, axis=-1, keepdims=True)
    p16 = p.astype(BF16)
    acc = alpha * acc + (_dot_nt(p16, kv16) if transposed else jnp.dot(p16, kv16, preferred_element_type=F32))
    return m_new, l, acc


def _attn_update(s, kv16, m, l, acc):
    m_new = jnp.maximum(m, jnp.max(s, axis=-1, keepdims=True))
    p = jnp.exp(s - m_new)
    alpha = jnp.exp(m - m_new)
    l = alpha * l + jnp.sum(p, axis=-1, keepdims=True)
    acc = alpha * acc + jnp.dot(p.astype(BF16), kv16, preferred_element_type=F32)
    return m_new, l, acc


def _attn_init(rows, width):
    return jnp.full((rows, 1), NEG_BIG, F32), jnp.zeros((rows, 1), F32), jnp.zeros((rows, width), F32)


def _slc_tile(kt, carry, qr16, kv16, selb, qpos_rows, nsp, tk, transposed):
    er = _row_iota((nsp, tk))
    ec = _lane_iota((nsp, tk))
    expand = jnp.where(er == (tk // SLC_BLOCK) * kt + ec // SLC_BLOCK, 1.0, 0.0).astype(BF16)
    seltok = jnp.dot(selb, expand, preferred_element_type=F32)
    seltok4 = jnp.concatenate([seltok] * N_HEADS, axis=0)
    tok = kt * tk + _lane_iota(seltok4.shape)
    mask = (seltok4 > 0.5) & (tok <= qpos_rows)
    return _attn_tile(qr16, kv16, mask, *carry, transposed=transposed)


def _win_tile(kv16, kpos, carry, qw16, qpos_rows):
    dist = qpos_rows - kpos
    mask = (dist >= 0) & (dist < WINDOW)
    return _attn_tile(qw16, kv16, mask, *carry)


def _nsa_combine(g, o_c, o_s, o_w, tq):
    comb = []
    for h in range(N_HEADS):
        sl = slice(h * tq, (h + 1) * tq)
        comb.append(g[:, h:h + 1] * o_c[sl] + g[:, N_HEADS + h:N_HEADS + h + 1] * o_s[sl]
                    + g[:, 2 * N_HEADS + h:2 * N_HEADS + h + 1] * o_w[sl])
    low = _lane_iota((tq, LANES)) < HEAD_DIM
    out01 = jnp.where(low, pltpu.roll(comb[0], HEAD_DIM, 1), comb[1])
    out23 = jnp.where(low, pltpu.roll(comb[2], HEAD_DIM, 1), comb[3])
    return jnp.concatenate([out01, out23], axis=1)


def _nsa_prompt_kernel(n_top, nq_ref, nqr_ref, ng_ref, rowsc_ref, rows_ref, rowst_ref, win_ref, wint_ref, onehot_ref,
                       c2st_ref, wc_ref, o_ref, kvc_s, kvct_s):
    j = pl.program_id(1)
    tq = nq_ref.shape[0]

    @pl.when(j == 0)
    def _():
        kvc = _compress(rowsc_ref, wc_ref, kvc_s.shape[0])
        kvc_s[...] = kvc
        kvct_s[...] = kvc.T

    rows4 = N_HEADS * tq
    qp_t16 = _stack_heads_128(nq_ref[...]).T.astype(BF16)
    qw_t = _stack_heads_128(nqr_ref[...]).T
    qw_t16 = qw_t.astype(BF16)

    o_c_t, sel_t = _cmp_select_t(qp_t16, kvc_s[...], kvct_s[...], c2st_ref[...], j, tq, n_top)
    last = (j * tq) // TKP

    sel_bias_t = (sel_t - 1.0) * (-NEG_BIG)
    q_aug_t = jnp.concatenate([qw_t, jnp.concatenate([sel_bias_t] * N_HEADS, axis=1)], axis=0).astype(BF16)

    def slc_scores(kt):
        r0 = pl.multiple_of(kt * TKP, TKP)
        k_aug = jnp.concatenate([rows_ref[pl.ds(r0, TKP), :], onehot_ref[pl.ds(r0, TKP), :]], axis=1)
        return jnp.dot(k_aug, q_aug_t, preferred_element_type=F32)

    def slc_update(s, kt, m, l, acc):
        m_new = jnp.maximum(m, jnp.max(s, axis=0, keepdims=True))
        p = jnp.exp(s - m_new)
        alpha = jnp.exp(m - m_new)
        l = alpha * l + jnp.sum(p, axis=0, keepdims=True)
        vt = rowst_ref[:, pl.ds(pl.multiple_of(kt * TKP, TKP), TKP)]
        acc = alpha * acc + jnp.dot(vt, p.astype(BF16), preferred_element_type=F32)
        return m_new, l, acc

    def slc_body(kt, st):
        m, l, acc, s = st
        s_next = slc_scores(jnp.minimum(kt + 1, last))
        m, l, acc = slc_update(s, kt, m, l, acc)
        return m, l, acc, s_next

    init = (jnp.full((1, rows4), NEG_BIG, F32), jnp.zeros((1, rows4), F32), jnp.zeros((LANES, rows4), F32))
    m_s, l_s, acc_s, s_diag = lax.fori_loop(0, last, slc_body, (*init, slc_scores(0)))
    tok = last * TKP + _row_iota((TKP, rows4))
    qpos_cols = j * tq + _lane_iota((TKP, rows4)) % tq
    s_diag = jnp.where(tok <= qpos_cols, s_diag, NEG_BIG)
    _, l_s, acc_s = slc_update(s_diag, last, m_s, l_s, acc_s)
    o_s = (acc_s / l_s).T

    def win_tile(kt, mask, st):
        m, l, acc = st
        r0 = pl.multiple_of(kt * TKP, TKP)
        s = jnp.dot(win_ref[pl.ds(r0, TKP), :], qw_t16, preferred_element_type=F32)
        s = jnp.where(mask, s, NEG_BIG)
        m_new = jnp.maximum(m, jnp.max(s, axis=0, keepdims=True))
        p = jnp.exp(s - m_new)
        alpha = jnp.exp(m - m_new)
        l = alpha * l + jnp.sum(p, axis=0, keepdims=True)
        acc = alpha * acc + jnp.dot(wint_ref[:, pl.ds(r0, TKP)], p.astype(BF16), preferred_element_type=F32)
        return m_new, l, acc

    krow = _row_iota((TKP, rows4))
    st_w = win_tile(last, last * TKP + krow <= qpos_cols, init)
    kb = jnp.maximum(last - 1, 0)
    st_w = win_tile(kb, kb * TKP + krow < last * TKP, st_w)
    ka = jnp.maximum(last - 2, 0)
    kpos_a = ka * TKP + krow
    _, l_w, acc_w = win_tile(ka, (qpos_cols - kpos_a < WINDOW) & (kpos_a < (last - 1) * TKP), st_w)
    o_ref[...] = _nsa_combine(ng_ref[...], o_c_t.T, o_s, (acc_w / l_w).T, tq)


def _nsa_prompt_call(nq, nqr, ng, rowsc, rowss16, rowst16, win16, wint16, wc, b, l):
    tq = TK
    n = l // tq
    n_top = min(N_SEL, l // SLC_BLOCK)
    nsp = _round_up(l // SLC_BLOCK, LANES)
    onehot = (jnp.arange(l)[:, None] // SLC_BLOCK == jnp.arange(nsp)[None, :]).astype(BF16)
    cs = jnp.arange(l // CMP_STRIDE)[None, :] * CMP_STRIDE
    ss = jnp.arange(nsp)[:, None] * SLC_BLOCK
    c2s_t = ((cs < ss + SLC_BLOCK) & (ss < cs + CMP_BLOCK)).astype(BF16)
    qspec = lambda w: pl.BlockSpec((tq, w), lambda i, j: (i * n + j, 0))
    return pl.pallas_call(
        functools.partial(_nsa_prompt_kernel, n_top),
        grid=(b, n),
        in_specs=[qspec(BRANCH_W), qspec(BRANCH_W), qspec(LANES),
                  pl.BlockSpec((l, LANES), lambda i, j: (i, 0)),
                  pl.BlockSpec((l, LANES), lambda i, j: (i, 0)),
                  pl.BlockSpec((LANES, l), lambda i, j: (0, i)),
                  pl.BlockSpec((l, LANES), lambda i, j: (i, 0)),
                  pl.BlockSpec((LANES, l), lambda i, j: (0, i)),
                  pl.BlockSpec(onehot.shape, lambda i, j: (0, 0)),
                  pl.BlockSpec(c2s_t.shape, lambda i, j: (0, 0)),
                  pl.BlockSpec(wc.shape, lambda i, j: (0, 0, 0))],
        out_specs=qspec(BRANCH_W),
        out_shape=jax.ShapeDtypeStruct((b * l, BRANCH_W), F32),
        scratch_shapes=[pltpu.VMEM((l // CMP_STRIDE, LANES), F32), pltpu.VMEM((LANES, l // CMP_STRIDE), F32)],
        compiler_params=_cparams("parallel", "arbitrary"),
        name="nsa_prompt",
    )(nq, nqr, ng, rowsc, rowss16, rowst16, win16, wint16, onehot, c2s_t, wc)


def _nsa_decode_kernel(n_pg, past_len, n_cmp, n_top, pt_ref, nq_ref, nqr_ref, ng_ref, rown_ref, rownt_ref, cwin_ref,
                       winn_ref, onehot_ref, wcc_ref, *rest):
    page_refs = rest[:n_pg]
    o_ref, wst_ref, xs_s, rowst_s, win_s = rest[n_pg:]
    s = pl.program_id(1)
    tq = nq_ref.shape[1]
    n_steps = rowst_s.shape[0]
    slab = n_pg * TK
    chunks_per_page = TK // CMP_STRIDE
    o_idx = _row_iota((TK, TK))
    perm = jnp.where(_lane_iota((TK, TK)) == CMP_STRIDE * (o_idx % chunks_per_page) + o_idx // chunks_per_page,
                     1.0, 0.0).astype(BF16)
    for k, pg in enumerate(page_refs):
        c0 = pl.multiple_of((s * n_pg + k) * chunks_per_page, chunks_per_page)
        by_offset = _dot_nt(perm, pg[0, 0, 0:LANES, :])
        for off in range(CMP_STRIDE):
            xs_s[off // 2, pl.ds(c0, chunks_per_page), (off % 2) * LANES:(off % 2 + 1) * LANES] = (
                by_offset[off * chunks_per_page:(off + 1) * chunks_per_page])
        rowst_s[s, :, k * TK:(k + 1) * TK] = pg[0, 0, LANES:2 * LANES, :]

    @pl.when(s == pl.num_programs(1) - 1)
    def _():
        w_rows = win_s.shape[0]
        wb = cwin_ref.shape[2]
        nsp = onehot_ref.shape[0]
        c_new = past_len // CMP_STRIDE
        first_row = _row_iota((SUBLANES, LANES)) == 0
        for off in range(CMP_STRIDE):
            new = rown_ref[0, off:off + 1, 0:LANES] if off < tq else jnp.zeros((1, LANES), F32)
            xs_s[off // 2, c_new:c_new + SUBLANES, (off % 2) * LANES:(off % 2 + 1) * LANES] = (
                jnp.where(first_row, new, 0.0))
        win_s[0:wb, :] = cwin_ref[0, 0]
        win_s[wb:wb + tq, :] = winn_ref[0]
        win_s[wb + tq:w_rows, :] = jnp.zeros((w_rows - wb - tq, LANES), F32)
        wst_ref[0] = win_s[tq:wb + tq, :]

        rows4 = N_HEADS * tq
        qpos_rows = past_len + lax.broadcasted_iota(jnp.int32, (rows4, 1), 0) % tq
        qpos_q = past_len + lax.broadcasted_iota(jnp.int32, (tq, 1), 0)
        qp16 = _stack_heads_128(nq_ref[0]).astype(BF16)
        qw16 = _stack_heads_128(nqr_ref[0]).astype(BF16)

        kvc = _compress_chunks(xs_s, wcc_ref)
        o_c, selb = _cmp_select(qp16, kvc, qpos_rows, qpos_q, tq, nsp, n_top)

        sel_bias = (selb.astype(F32) - 1.0) * (-NEG_BIG)
        q_aug = jnp.concatenate([_stack_heads_128(nqr_ref[0]), jnp.concatenate([sel_bias] * N_HEADS, axis=0)],
                                axis=1).astype(BF16)

        def slc_step(s_t, kv_t, st):
            m, l, acc = st
            m_new = jnp.maximum(m, jnp.max(s_t, axis=-1, keepdims=True))
            p = jnp.exp(s_t - m_new)
            alpha = jnp.exp(m - m_new)
            l = alpha * l + jnp.sum(p, axis=-1, keepdims=True)
            return m_new, l, alpha * acc + _dot_nt(p.astype(BF16), kv_t)

        def slab_body(kt, st):
            kv_t = rowst_s[kt].astype(BF16)
            k0 = pl.multiple_of(kt * slab, slab)
            k_aug_t = jnp.concatenate([kv_t, onehot_ref[:, pl.ds(k0, slab)]], axis=0)
            return slc_step(jnp.dot(q_aug, k_aug_t, preferred_element_type=F32), kv_t, st)

        st = lax.fori_loop(0, n_steps, slab_body, _attn_init(rows4, LANES))
        kv_t = rownt_ref[0].astype(BF16)
        s_new = jnp.dot(q_aug, jnp.concatenate([kv_t, onehot_ref[:, past_len:past_len + TK]], axis=0),
                        preferred_element_type=F32)
        s_new = jnp.where(past_len + _lane_iota((rows4, TK)) <= qpos_rows, s_new, NEG_BIG)
        _, l_s, acc_s = slc_step(s_new, kv_t, st)
        o_s = acc_s / l_s

        carry = _attn_init(rows4, LANES)
        for kt in range(w_rows // TK):
            kpos = past_len - wb + kt * TK + _lane_iota((rows4, TK))
            carry = _win_tile(win_s[kt * TK:(kt + 1) * TK, :].astype(BF16), kpos, carry, qw16, qpos_rows)
        _, l_w, acc_w = carry
        o_ref[0] = _nsa_combine(ng_ref[0], o_c, o_s, acc_w / l_w, tq)


def _nsa_decode_call(nq3, nqr3, ng3, rown3, rownt3, winn3, cache_nsa_t, cache_win4, layer, page_table, wc):
    b, tq, _ = nq3.shape
    n_pages = page_table.shape[1]
    past_len = n_pages * TK
    wb = cache_win4.shape[2]
    n_pg = math.gcd(2 * PAGES_PER_STEP, n_pages)
    n_steps = n_pages // n_pg
    t_pad = _round_up(past_len + tq, SLC_BLOCK)
    n_cmp = _round_up(t_pad // CMP_STRIDE, SUBLANES)
    t_rows = _round_up(n_cmp * CMP_STRIDE, TK)
    assert t_rows == past_len + TK and tq <= SUBLANES and n_cmp - past_len // CMP_STRIDE == SUBLANES
    w_rows = _round_up(wb + tq, TK)
    n_top = min(N_SEL, t_pad // SLC_BLOCK)
    nsp = _round_up(t_pad // SLC_BLOCK, LANES)
    onehot_t = (jnp.arange(nsp)[:, None] == jnp.arange(t_rows)[None, :] // SLC_BLOCK).astype(BF16)
    wcc = _cmp_pairs(wc)

    def page_spec(k):
        return pl.BlockSpec((1, 1, BRANCH_W, TK), lambda i, s, pt: (layer, pt[i, s * n_pg + k], 0, 0))

    row3 = lambda r, w: pl.BlockSpec((1, r, w), lambda i, s, pt: (i, 0, 0))
    grid_spec = pltpu.PrefetchScalarGridSpec(
        num_scalar_prefetch=1,
        grid=(b, n_steps),
        in_specs=[row3(tq, BRANCH_W), row3(tq, BRANCH_W), row3(tq, LANES), row3(tq, BRANCH_W), row3(LANES, TK),
                  pl.BlockSpec((1, 1, wb, LANES), lambda i, s, pt: (layer, i, 0, 0)),
                  row3(tq, LANES),
                  pl.BlockSpec(onehot_t.shape, lambda i, s, pt: (0, 0)),
                  pl.BlockSpec(wcc.shape, lambda i, s, pt: (0, 0, 0))]
                 + [page_spec(k) for k in range(n_pg)],
        out_specs=[row3(tq, BRANCH_W), row3(wb, LANES)],
        scratch_shapes=[pltpu.VMEM((CMP_STRIDE // 2, n_cmp, 2 * LANES), F32),
                        pltpu.VMEM((n_steps, LANES, n_pg * TK), F32),
                        pltpu.VMEM((w_rows, LANES), F32)],
    )
    return pl.pallas_call(
        functools.partial(_nsa_decode_kernel, n_pg, past_len, n_cmp, n_top),
        grid_spec=grid_spec,
        out_shape=[jax.ShapeDtypeStruct((b, tq, BRANCH_W), F32), jax.ShapeDtypeStruct((b, wb, LANES), F32)],
        compiler_params=_cparams("parallel", "arbitrary"),
        name="nsa_decode",
    )(page_table, nq3, nqr3, ng3, rown3, rownt3, cache_win4, winn3, onehot_t, wcc, *([cache_nsa_t] * n_pg))


def _cmp_weights(w_cmp_k, w_cmp_v):
    z = jnp.zeros_like(w_cmp_k)
    top = jnp.concatenate([w_cmp_k, z], axis=2)
    bot = jnp.concatenate([z, w_cmp_v], axis=2)
    return jnp.concatenate([top, bot], axis=1).astype(BF16)


def _rope_tables(pos):
    posf = pos.astype(F32)[:, None]
    jj = np.arange(LANES) % HEAD_DIM

    def table(half, theta, lanes_on):
        inv = theta ** (-jnp.arange(half, dtype=F32) / half)
        ang = posf * inv[None, :]
        cos_l = jnp.cos(ang)[:, jj % half]
        sin_l = jnp.sin(ang)[:, jj % half] * jnp.where(jj < half, -1.0, 1.0).astype(F32)[None, :]
        on = jnp.asarray((jj < 2 * half) & lanes_on)[None, :]
        return jnp.where(on, cos_l, 1.0), jnp.where(on, sin_l, 0.0)

    every = np.ones(LANES, bool)
    first = np.arange(LANES) < HEAD_DIM
    parts = (*table(HEAD_DIM // 2, RET_THETA, every), *table(ROPE_DIM // 2, ROPE_THETA, every),
             *table(ROPE_DIM // 2, ROPE_THETA, first))
    return jnp.concatenate(parts, axis=1)


def _layer_weights(i, p):
    w_t = p["w_in_t"][:, i, :]
    d = w_t.shape[1]
    c_mg = w_t.shape[0] - N_BRANCH * d
    c_nsa = 8 * BRANCH_W
    w_nsa = w_t[c_nsa:c_mg]
    w_nsa = jnp.pad(w_nsa, ((0, 3 * BRANCH_W - w_nsa.shape[0]), (0, 0)))
    w = {
        "w_ret": w_t[0:4 * BRANCH_W].astype(BF16), "w_u": w_t[4 * BRANCH_W:5 * BRANCH_W].astype(BF16),
        "w_sb": w_t[5 * BRANCH_W:8 * BRANCH_W].astype(BF16), "w_nsa": w_nsa.astype(BF16),
        "w_mg": w_t[c_mg:].astype(BF16), "w_branch": p["w_branch"][i].astype(BF16),
        "w_out": p["w_out"][i].astype(BF16), "ln1_g": p["ln1_g"][i][None, :], "ln1_b": p["ln1_b"][i][None, :],
        "w_up": p["w_ffn_up"][i].astype(BF16), "w_dn": p["w_ffn_down"][i].astype(BF16),
        "ln2_g": p["ln2_g"][i][None, :], "ln2_b": p["ln2_b"][i][None, :],
        "wc": _cmp_weights(p["w_cmp_k"][i], p["w_cmp_v"][i]),
    }
    s5 = _s5_params(p["ssm_a_re"][i], p["ssm_a_im"][i], p["ssm_b_re"][i], p["ssm_b_im"][i], p["ssm_c_re"][i],
                    p["ssm_c_im"][i], p["ssm_d"][i], p["ssm_log_dt"][i], p["w_glu"][i], p["b_glu"][i])
    return w, s5


def _ssm_rows(s):
    b = s.shape[0]
    return jnp.moveaxis(s, -1, 1).reshape(b, 2, -1)


def _ssm_state(rows, g):
    b = rows.shape[0]
    return jnp.moveaxis(rows.reshape(b, 2, g, -1), 1, -1)


def _layer(x, mod, tabs, w, s5, alpha, past):
    b, l, d = x.shape
    t = b * l
    x2 = x.reshape(t, d)
    g = s5["a_re"].shape[-1] // SSM_STATE
    if past is None:
        tm = min(l, 512)
        tpb = l // tm
        mod_in, tab_in = mod, tabs
    else:
        tm, tpb = t, None
        mod_in = jnp.repeat(mod.reshape(b, 8 * d), l, axis=0)
        tab_in = jnp.tile(tabs, (b, 1))
    ret, u, sbq, sbkv, sbkv16, nq, nqr, rows, win, ng, rowsc, rowss16, win16, *rowst16 = _in_call(
        x2, mod_in, tab_in, w, tpb, tm)

    if past is None:
        r0 = jnp.zeros((b, BRANCH_W, BRANCH_W), F32)
        x0 = jnp.zeros((b, 2, g * SSM_STATE), F32)
    else:
        r0 = _bd_from_heads(past["ret"])
        x0 = _ssm_rows(past["ssm"])
    out_a, r_bd = _ret_call(ret, b, l, r0)
    out_b, s_rows = _s5_call(u, b, l, x0, s5)

    if past is None:
        out_c = _sb_prompt_call(sbq, sbkv16, b, l)
        out_d = _nsa_prompt_call(nq, nqr, ng, rowsc, rowss16, rowst16[0], win16, rowst16[1], w["wc"], b, l)
        win_state = win.reshape(b, l, 2, HEAD_DIM)[:, l - min(WINDOW, l):]
    else:
        kvn_t = jnp.swapaxes(jnp.pad(sbkv16.reshape(b, l, 2 * BRANCH_W), ((0, 0), (0, TK - l), (0, 0))), 1, 2)
        out_c = _sb_decode_call(sbq.reshape(b, l, BRANCH_W), kvn_t, past["sb"], past["layer"], past["page_table"])
        rows3 = rows.reshape(b, l, BRANCH_W)
        rown_t = jnp.swapaxes(jnp.pad(rows3[:, :, LANES:], ((0, 0), (0, TK - l), (0, 0))), 1, 2)
        out_d, wst = _nsa_decode_call(nq.reshape(b, l, BRANCH_W), nqr.reshape(b, l, BRANCH_W),
                                      ng.reshape(b, l, LANES), rows3, rown_t,
                                      win.reshape(b, l, LANES), past["nsa"], past["win"], past["layer"],
                                      past["page_table"], w["wc"])
        out_c = out_c.reshape(t, BRANCH_W)
        out_d = out_d.reshape(t, BRANCH_W)
        win_state = wst.reshape(b, -1, 2, HEAD_DIM)

    tm2 = min(tm, 256)
    tpb2 = None if tpb is None else l // tm2
    x1 = _merge_call(x2, mod_in, [out_a, out_b, out_c, out_d], w, tpb2, tm2, alpha)
    xo = _ffn_call(x1, mod_in, w, tpb2, tm2, alpha)
    state = (sbkv.reshape(b, l, 2, N_HEADS, HEAD_DIM), rows.reshape(b, l, 4, HEAD_DIM), win_state,
             _heads_from_bd(r_bd), _ssm_state(s_rows, g))
    return xo.reshape(b, l, d), state


def kernel(x_prompt, x_sample, c_prompt, c_sample, cache_sb, cache_nsa, cache_win, state_ret, state_ssm, page_table, w_ada, b_ada, w_in, ssm_a_re, ssm_a_im, ssm_b_re, ssm_b_im, ssm_c_re, ssm_c_im, ssm_d, ssm_log_dt, w_glu, b_glu, w_cmp_k, w_cmp_v, w_branch, w_out, ln1_g, ln1_b, w_ffn_up, w_ffn_down, ln2_g, ln2_b):
    p = dict(w_in_t=jnp.transpose(w_in, (2, 0, 1)), ssm_a_re=ssm_a_re, ssm_a_im=ssm_a_im, ssm_b_re=ssm_b_re, ssm_b_im=ssm_b_im,
             ssm_c_re=ssm_c_re, ssm_c_im=ssm_c_im, ssm_d=ssm_d, ssm_log_dt=ssm_log_dt, w_glu=w_glu, b_glu=b_glu,
             w_cmp_k=w_cmp_k, w_cmp_v=w_cmp_v, w_branch=w_branch, w_out=w_out, ln1_g=ln1_g, ln1_b=ln1_b,
             w_ffn_up=w_ffn_up, w_ffn_down=w_ffn_down, ln2_g=ln2_g, ln2_b=ln2_b)
    depth, d = w_ada.shape[0], w_ada.shape[1]
    bp, lp, _ = x_prompt.shape
    bs, ls, _ = x_sample.shape
    n_pool, page = cache_sb.shape[1], cache_sb.shape[2]
    past_len = page_table.shape[1] * page
    alpha = (2.0 * depth) ** 0.25

    c_all = jnp.concatenate([c_prompt, c_sample], axis=0)
    m_rows = _round_up(bp + bs, SUBLANES)
    c_all = jnp.pad(c_all, ((0, m_rows - bp - bs), (0, 0)))
    mods = _ada_call(c_all, w_ada, b_ada)

    tabs_p = _rope_tables(jnp.arange(lp, dtype=jnp.int32))
    tabs_s = _rope_tables(past_len + jnp.arange(ls, dtype=jnp.int32))
    cache_sb4 = jnp.transpose(cache_sb, (0, 1, 3, 4, 5, 2)).reshape(depth, n_pool, 2 * BRANCH_W, page)
    cache_nsa4 = jnp.transpose(cache_nsa, (0, 1, 3, 4, 2)).reshape(depth, n_pool, BRANCH_W, page)
    cache_win4 = cache_win.reshape(depth, bs, cache_win.shape[2], LANES)

    xp, xs = x_prompt, x_sample
    st_p = [[] for _ in range(5)]
    st_s = [[] for _ in range(5)]
    for i in range(depth):
        w, s5 = _layer_weights(i, p)
        mod_i = jnp.pad(mods[i].reshape(m_rows, 6, d), ((0, 0), (0, 2), (0, 0)))
        xp, new_p = _layer(xp, mod_i[:bp], tabs_p, w, s5, alpha, None)
        past = dict(sb=cache_sb4, nsa=cache_nsa4, win=cache_win4, ret=state_ret[i], ssm=state_ssm[i],
                    layer=i, page_table=page_table)
        xs, new_s = _layer(xs, mod_i[bp:bp + bs], tabs_s, w, s5, alpha, past)
        for k in range(5):
            st_p[k].append(new_p[k])
            st_s[k].append(new_s[k])
    sb_p, nsa_p, win_p, ret_p, ssm_p = [jnp.stack(s, axis=0) for s in st_p]
    sb_s, nsa_s, win_s, ret_s, ssm_s = [jnp.stack(s, axis=0) for s in st_s]
    return (xp, xs, sb_p, sb_s, nsa_p, nsa_s, win_p, win_s, ret_p, ret_s, ssm_p, ssm_s)
```

```python
import functools
import math

import jax
import jax.numpy as jnp
import numpy as np
from jax import lax
from jax.experimental import pallas as pl
from jax.experimental.pallas import tpu as pltpu

F32 = jnp.float32
BF16 = jnp.bfloat16

LANES = 128
SUBLANES = 8
VMEM_LIMIT_BYTES = 56 * 1024 * 1024

HEAD_DIM = 64
N_HEADS = 4
BRANCH_W = N_HEADS * HEAD_DIM
N_BRANCH = 4
RET_THETA = 10000.0
RET_CHUNK = 128
SSM_GROUP_CH = 16
SSM_STATE = 64
CMP_BLOCK = 32
CMP_STRIDE = 16
SLC_BLOCK = 64
N_SEL = 16
WINDOW = 512
ROPE_THETA = 500000.0
ROPE_DIM = HEAD_DIM // 4
LN_EPS = 1e-5
QK_SCALE = HEAD_DIM ** -0.5
NEG_BIG = -1e30


def _cparams(*sem, flags=None):
    return pltpu.CompilerParams(dimension_semantics=tuple(sem), vmem_limit_bytes=VMEM_LIMIT_BYTES, flags=flags)


def _lane_iota(shape):
    return lax.broadcasted_iota(jnp.int32, shape, len(shape) - 1)


def _row_iota(shape):
    return lax.broadcasted_iota(jnp.int32, shape, len(shape) - 2)


def _dot(a, b):
    return jnp.dot(a.astype(BF16), b.astype(BF16), preferred_element_type=F32)


def _dot_nt(a, b):
    return lax.dot_general(a.astype(BF16), b.astype(BF16), (((1,), (1,)), ((), ())), preferred_element_type=F32)


def _dot_tn(a, b):
    return lax.dot_general(a.astype(BF16), b.astype(BF16), (((0,), (0,)), ((), ())), preferred_element_type=F32)


def _ln_rows(x):
    mu = jnp.mean(x, axis=-1, keepdims=True)
    xc = x - mu
    var = jnp.mean(xc * xc, axis=-1, keepdims=True)
    return xc * lax.rsqrt(var + LN_EPS)


def _sigmoid(x):
    return 1.0 / (1.0 + jnp.exp(-x))


def _silu(x):
    return x * _sigmoid(x)


def _head_mask(shape, h):
    lane = _lane_iota(shape)
    return (lane >= h * HEAD_DIM) & (lane < (h + 1) * HEAD_DIM)


def _ada_kernel(c_ref, w_ref, b_ref, o_ref):
    o_ref[0] = _dot(_silu(c_ref[...]), w_ref[0]) + b_ref[0]


def _ada_call(c_all, w_ada, b_ada):
    depth, d, n = w_ada.shape
    m = c_all.shape[0]
    tn = 1536
    return pl.pallas_call(
        _ada_kernel,
        grid=(depth, n // tn),
        in_specs=[pl.BlockSpec((m, d), lambda i, j: (0, 0)),
                  pl.BlockSpec((1, d, tn), lambda i, j: (i, 0, j)),
                  pl.BlockSpec((1, 1, tn), lambda i, j: (i, 0, j))],
        out_specs=pl.BlockSpec((1, m, tn), lambda i, j: (i, 0, j)),
        out_shape=jax.ShapeDtypeStruct((depth, m, n), F32),
        compiler_params=_cparams("parallel", "parallel"),
        name="ada_mod",
    )(c_all, w_ada, b_ada.reshape(depth, 1, n))


def _rope128(x, cos, sin_signed, half):
    first = (_lane_iota(x.shape) % HEAD_DIM) < half
    partner = jnp.where(first, pltpu.roll(x, LANES - half, 1), pltpu.roll(x, half, 1))
    return x * cos + partner * sin_signed


def _mod_rows(m_ref, row, per_token):
    if per_token:
        d = m_ref.shape[-1] // 8
        return m_ref[:, row * d:(row + 1) * d]
    return m_ref[0, row:row + 1, :]


def _in_kernel(per_token, x_ref, m_ref, wr_ref, wu_ref, wsb_ref, wn_ref, tab_ref,
               ret_ref, u_ref, sbq_ref, sbkv_ref, sbkv16_ref, nq_ref, nqr_ref, rows_ref, win_ref, ng_ref,
               rowsc_ref, rowss16_ref, win16_ref, *maybe_rowst16_ref):
    x = x_ref[...]
    h = (_ln_rows(x) * (1.0 + _mod_rows(m_ref, 1, per_token)) + _mod_rows(m_ref, 0, per_token)).astype(BF16)
    tab = tab_ref[...]
    rc, rs = tab[:, 0:128], tab[:, 128:256]
    qc, qs = tab[:, 256:384], tab[:, 384:512]
    kc, ks = tab[:, 512:640], tab[:, 640:768]

    zr = _dot_nt(h, wr_ref[...])
    half_r = HEAD_DIM // 2
    for c in range(2):
        ret_ref[:, c * 128:(c + 1) * 128] = _rope128(zr[:, c * 128:(c + 1) * 128], rc, rs, half_r)
    for c in range(2, 4):
        ret_ref[:, c * 128:(c + 1) * 128] = _rope128(zr[:, c * 128:(c + 1) * 128], rc, rs, half_r) * QK_SCALE
    ret_ref[:, 512:1024] = zr[:, 512:1024]

    u_ref[...] = _dot_nt(h, wu_ref[...])

    zs = _dot_nt(h, wsb_ref[...])
    sbq_ref[...] = (zs[:, 0:256] * QK_SCALE).astype(BF16)
    sbkv_ref[...] = zs[:, 256:768]
    sbkv16_ref[...] = zs[:, 256:768].astype(BF16)

    zn = _dot_nt(h, wn_ref[...])
    half_n = ROPE_DIM // 2
    nq_ref[...] = (zn[:, 0:256] * QK_SCALE).astype(BF16)
    for c in range(2):
        nqr_ref[:, c * 128:(c + 1) * 128] = (
            _rope128(zn[:, c * 128:(c + 1) * 128], qc, qs, half_n) * QK_SCALE).astype(BF16)
    rows_slc = _rope128(zn[:, 384:512], kc, ks, half_n)
    rows_ref[:, 0:128] = zn[:, 256:384]
    rows_ref[:, 128:256] = rows_slc
    rowsc_ref[...] = zn[:, 256:384]
    rowss16_ref[...] = rows_slc.astype(BF16)
    if maybe_rowst16_ref:
        maybe_rowst16_ref[0][...] = rows_slc.T.astype(BF16)
    win = _rope128(zn[:, 512:640], kc, ks, half_n)
    win_ref[...] = win
    win16_ref[...] = win.astype(BF16)
    if maybe_rowst16_ref:
        maybe_rowst16_ref[1][...] = win.T.astype(BF16)
    ng_ref[...] = _sigmoid(zn[:, 640:768])


def _in_call(x2, mod, tabs, w, tiles_per_batch, tm):
    t, d = x2.shape
    per_token = tiles_per_batch is None
    nt = t // tm
    if per_token:
        mod_spec = pl.BlockSpec((tm, 8 * d), lambda i: (i, 0))
        tab_spec = pl.BlockSpec((tm, 768), lambda i: (i, 0))
    else:
        mod_spec = pl.BlockSpec((1, 8, d), lambda i: (i // tiles_per_batch, 0, 0))
        tab_spec = pl.BlockSpec((tm, 768), lambda i: (i % tiles_per_batch, 0))

    def wspec(a):
        return pl.BlockSpec(a.shape, lambda i: (0, 0))

    def ospec(n):
        return pl.BlockSpec((tm, n), lambda i: (i, 0))

    outs = [(1024, F32), (256, F32), (256, BF16), (512, F32), (512, BF16), (256, BF16), (256, BF16),
            (256, F32), (128, F32), (128, F32), (128, F32), (128, BF16), (128, BF16)]
    return pl.pallas_call(
        functools.partial(_in_kernel, per_token),
        grid=(nt,),
        in_specs=[pl.BlockSpec((tm, d), lambda i: (i, 0)), mod_spec,
                  wspec(w["w_ret"]), wspec(w["w_u"]), wspec(w["w_sb"]), wspec(w["w_nsa"]), tab_spec],
        out_specs=[ospec(n) for n, _ in outs] + ([] if per_token else [pl.BlockSpec((LANES, tm), lambda i: (0, i))] * 2),
        out_shape=[jax.ShapeDtypeStruct((t, n), dt) for n, dt in outs]
        + ([] if per_token else [jax.ShapeDtypeStruct((LANES, t), BF16)] * 2),
        compiler_params=_cparams("parallel"),
        name="in_proj",
    )(x2, mod, w["w_ret"], w["w_u"], w["w_sb"], w["w_nsa"], tabs)


def _merge_kernel(per_token, alpha, x_ref, m_ref, a_ref, b_ref, c_ref, d_ref, wmg_ref, wbr_ref, wout_ref,
                  g_ref, bias_ref, o_ref):
    x = x_ref[...]
    dm = x.shape[-1]
    h = (_ln_rows(x) * (1.0 + _mod_rows(m_ref, 1, per_token)) + _mod_rows(m_ref, 0, per_token)).astype(BF16)
    acc = jnp.zeros(x.shape, F32)
    for n, br_ref in enumerate((a_ref, b_ref, c_ref, d_ref)):
        mg = _dot_nt(h, wmg_ref[n * dm:(n + 1) * dm, :])
        br = jnp.dot(br_ref[...].astype(BF16), wbr_ref[n], preferred_element_type=F32)
        acc = acc + _sigmoid(mg) * br
    y = jnp.dot(acc.astype(BF16), wout_ref[...], preferred_element_type=F32)
    r = alpha * x + (1.0 + _mod_rows(m_ref, 2, per_token)) * y
    o_ref[...] = _ln_rows(r) * g_ref[...] + bias_ref[...]


def _merge_call(x2, mod, branches, w, tiles_per_batch, tm, alpha):
    t, d = x2.shape
    per_token = tiles_per_batch is None
    if per_token:
        mod_spec = pl.BlockSpec((tm, 8 * d), lambda i: (i, 0))
    else:
        mod_spec = pl.BlockSpec((1, 8, d), lambda i: (i // tiles_per_batch, 0, 0))
    row = lambda n: pl.BlockSpec((tm, n), lambda i: (i, 0))
    return pl.pallas_call(
        functools.partial(_merge_kernel, per_token, alpha),
        grid=(t // tm,),
        in_specs=[row(d), mod_spec, row(BRANCH_W), row(BRANCH_W), row(BRANCH_W), row(BRANCH_W),
                  pl.BlockSpec(w["w_mg"].shape, lambda i: (0, 0)),
                  pl.BlockSpec(w["w_branch"].shape, lambda i: (0, 0, 0)),
                  pl.BlockSpec(w["w_out"].shape, lambda i: (0, 0)),
                  pl.BlockSpec((1, d), lambda i: (0, 0)), pl.BlockSpec((1, d), lambda i: (0, 0))],
        out_specs=row(d),
        out_shape=jax.ShapeDtypeStruct((t, d), F32),
        compiler_params=_cparams("parallel"),
        name="merge_out",
    )(x2, mod, *branches, w["w_mg"], w["w_branch"], w["w_out"], w["ln1_g"], w["ln1_b"])


def _ffn_kernel(per_token, alpha, n_chunk, x_ref, m_ref, wup_ref, wdn_ref, g_ref, bias_ref, o_ref):
    x = x_ref[...]
    dff = wdn_ref.shape[0]
    ck = dff // n_chunk
    h = (_ln_rows(x) * (1.0 + _mod_rows(m_ref, 4, per_token)) + _mod_rows(m_ref, 3, per_token)).astype(BF16)
    f = jnp.zeros(x.shape, F32)
    for c in range(n_chunk):
        a = jnp.dot(h, wup_ref[:, c * ck:(c + 1) * ck], preferred_element_type=F32)
        u = jnp.dot(h, wup_ref[:, dff + c * ck:dff + (c + 1) * ck], preferred_element_type=F32)
        f = f + jnp.dot((_silu(a) * u).astype(BF16), wdn_ref[c * ck:(c + 1) * ck, :], preferred_element_type=F32)
    r = alpha * x + (1.0 + _mod_rows(m_ref, 5, per_token)) * f
    o_ref[...] = _ln_rows(r) * g_ref[...] + bias_ref[...]


def _ffn_call(x2, mod, w, tiles_per_batch, tm, alpha):
    t, d = x2.shape
    per_token = tiles_per_batch is None
    if per_token:
        mod_spec = pl.BlockSpec((tm, 8 * d), lambda i: (i, 0))
    else:
        mod_spec = pl.BlockSpec((1, 8, d), lambda i: (i // tiles_per_batch, 0, 0))
    dff = w["w_dn"].shape[0]
    n_chunk = 2 if dff % 256 == 0 else 1
    return pl.pallas_call(
        functools.partial(_ffn_kernel, per_token, alpha, n_chunk),
        grid=(t // tm,),
        in_specs=[pl.BlockSpec((tm, d), lambda i: (i, 0)), mod_spec,
                  pl.BlockSpec(w["w_up"].shape, lambda i: (0, 0)),
                  pl.BlockSpec(w["w_dn"].shape, lambda i: (0, 0)),
                  pl.BlockSpec((1, d), lambda i: (0, 0)), pl.BlockSpec((1, d), lambda i: (0, 0))],
        out_specs=pl.BlockSpec((tm, d), lambda i: (i, 0)),
        out_shape=jax.ShapeDtypeStruct((t, d), F32),
        compiler_params=_cparams("parallel"),
        name="ffn",
    )(x2, mod, w["w_up"], w["w_dn"], w["ln2_g"], w["ln2_b"])


def _ret_kernel(chunk, q_ref, k_ref, v_ref, g_ref, r0_ref, intra_ref, qdec_ref, kdec_ref, cdec_ref,
                o_ref, rout_ref, r_s):
    j = pl.program_id(1)

    @pl.when(j == 0)
    def _():
        r_s[...] = r0_ref[0]

    for c in range(q_ref.shape[0] // chunk):
        rows = slice(c * chunk, (c + 1) * chunk)
        q = q_ref[rows, :]
        k = k_ref[rows, :]
        v16 = v_ref[rows, :].astype(BF16)
        k16 = k.astype(BF16)
        o = _dot(q * qdec_ref[...], r_s[...])
        for h in range(N_HEADS):
            hm = _head_mask(q.shape, h)
            s = _dot_nt(jnp.where(hm, q, 0.0), k16) * intra_ref[h]
            o = o + jnp.where(hm, _dot(s, v16), 0.0)
        upd = _dot_tn(k * kdec_ref[...], v16)
        rr = _row_iota(upd.shape) // HEAD_DIM
        cc = _lane_iota(upd.shape) // HEAD_DIM
        r_s[...] = r_s[...] * cdec_ref[...] + jnp.where(rr == cc, upd, 0.0)

        mu = jnp.zeros(o.shape, F32)
        for h in range(N_HEADS):
            hm = _head_mask(o.shape, h)
            mu = mu + jnp.where(hm, jnp.sum(jnp.where(hm, o, 0.0), axis=-1, keepdims=True), 0.0)
        oc = o - mu * (1.0 / HEAD_DIM)
        var = jnp.zeros(o.shape, F32)
        oc2 = oc * oc
        for h in range(N_HEADS):
            hm = _head_mask(o.shape, h)
            var = var + jnp.where(hm, jnp.sum(jnp.where(hm, oc2, 0.0), axis=-1, keepdims=True), 0.0)
        o_ref[rows, :] = oc * lax.rsqrt(var * (1.0 / HEAD_DIM) + LN_EPS) * _silu(g_ref[rows, :])

    @pl.when(j == pl.num_programs(1) - 1)
    def _():
        rout_ref[0] = r_s[...]


def _ret_tables(chunk):
    h = jnp.arange(N_HEADS, dtype=F32)
    log_g = jnp.log1p(-jnp.exp2(-5.0 - h))
    idx = jnp.arange(chunk, dtype=F32)
    diff = idx[:, None] - idx[None, :]
    intra = jnp.where(diff >= 0, jnp.exp(jnp.maximum(diff, 0.0)[None] * log_g[:, None, None]), 0.0)
    q_dec = jnp.exp((idx + 1.0)[None, :] * log_g[:, None])
    k_dec = jnp.exp((chunk - 1.0 - idx)[None, :] * log_g[:, None])
    c_dec = jnp.exp(chunk * log_g)
    lanes = lambda t: jnp.repeat(t.T, HEAD_DIM, axis=1)
    return intra, lanes(q_dec), lanes(k_dec), jnp.repeat(c_dec, HEAD_DIM)[None, :]


def _ret_call(ret, b, l, r0_bd):
    chunk = RET_CHUNK if l % RET_CHUNK == 0 else l
    per_step = 4 if (l // chunk) % 4 == 0 else 1
    n = l // (chunk * per_step)
    intra, q_dec, k_dec, c_dec = _ret_tables(chunk)
    col = lambda c: pl.BlockSpec((chunk * per_step, BRANCH_W), lambda i, j: (i * n + j, c))
    full = lambda a: pl.BlockSpec(a.shape, lambda i, j: (0,) * a.ndim)
    return pl.pallas_call(
        functools.partial(_ret_kernel, chunk),
        grid=(b, n),
        in_specs=[col(0), col(1), col(2), col(3),
                  pl.BlockSpec((1, BRANCH_W, BRANCH_W), lambda i, j: (i, 0, 0)),
                  full(intra), full(q_dec), full(k_dec), full(c_dec)],
        out_specs=[pl.BlockSpec((chunk * per_step, BRANCH_W), lambda i, j: (i * n + j, 0)),
                   pl.BlockSpec((1, BRANCH_W, BRANCH_W), lambda i, j: (i, 0, 0))],
        out_shape=[jax.ShapeDtypeStruct((b * l, BRANCH_W), F32),
                   jax.ShapeDtypeStruct((b, BRANCH_W, BRANCH_W), F32)],
        scratch_shapes=[pltpu.VMEM((BRANCH_W, BRANCH_W), F32)],
        compiler_params=_cparams("parallel", "arbitrary"),
        name="retention",
    )(ret, ret, ret, ret, r0_bd, intra, q_dec, k_dec, c_dec)


def _bd_from_heads(r):
    b = r.shape[0]
    eye = jnp.eye(N_HEADS, dtype=r.dtype)
    return jnp.einsum("bhij,hg->bhigj", r, eye).reshape(b, BRANCH_W, BRANCH_W)


def _heads_from_bd(r_bd):
    b = r_bd.shape[0]
    r5 = r_bd.reshape(b, N_HEADS, HEAD_DIM, N_HEADS, HEAD_DIM)
    return jnp.stack([r5[:, h, :, h, :] for h in range(N_HEADS)], axis=1)


def _cmul(ar, ai, br, bi):
    return ar * br - ai * bi, ar * bi + ai * br


def _gelu_tanh(x):
    return 0.5 * x * (1.0 + jnp.tanh(math.sqrt(2.0 / math.pi) * (x + 0.044715 * (x * x * x))))


def _s5_kernel(chain, u_ref, x0_ref, are_ref, aim_ref, ldt_ref, bre_ref, bim_ref, cre_ref, cim_ref, dsk_ref,
               wglu_ref, bglu_ref, o_ref, st_ref, xr_s, xi_s, cr_s, ci_s):
    rows = u_ref.shape[0]
    n_grp = rows // SUBLANES
    j = pl.program_id(1)

    ar, ai = are_ref[...], aim_ref[...]
    dt = jnp.exp(ldt_ref[...])
    mag = jnp.exp(ar * dt)
    abr, abi = mag * jnp.cos(ai * dt), mag * jnp.sin(ai * dt)
    nr, ni = abr - 1.0, abi
    den = ar * ar + ai * ai
    fr = (nr * ar + ni * ai) / den
    fi = (ni * ar - nr * ai) / den
    bbr = fr * bre_ref[...] - fi * bim_ref[...]
    bbi = fr * bim_ref[...] + fi * bre_ref[...]

    u = u_ref[...]
    u16 = u.astype(BF16)
    xr_s[...] = jnp.dot(u16, bbr.astype(BF16), preferred_element_type=F32)
    xi_s[...] = jnp.dot(u16, bbi.astype(BF16), preferred_element_type=F32)

    a2r, a2i = _cmul(abr, abi, abr, abi)
    a4r, a4i = _cmul(a2r, a2i, a2r, a2i)
    row = _row_iota((SUBLANES, abr.shape[-1]))
    pr, pi_ = jnp.broadcast_to(abr, row.shape), jnp.broadcast_to(abi, row.shape)
    qr, qi = abr, abi
    for i in range(1, SUBLANES):
        qr, qi = _cmul(qr, qi, abr, abi)
        pr = jnp.where(row == i, qr, pr)
        pi_ = jnp.where(row == i, qi, pi_)
    steps = ((1, abr, abi), (2, a2r, a2i), (4, a4r, a4i))

    if chain:
        @pl.when(j == 0)
        def _():
            cr_s[...] = x0_ref[0, 0:1, :]
            ci_s[...] = x0_ref[0, 1:2, :]

    def group(g, carry):
        r0 = pl.multiple_of(g * SUBLANES, SUBLANES)
        xr = xr_s[pl.ds(r0, SUBLANES), :]
        xi = xi_s[pl.ds(r0, SUBLANES), :]
        for d, er, ei in steps:
            sr = jnp.where(row >= d, pltpu.roll(xr, d, 0), 0.0)
            si = jnp.where(row >= d, pltpu.roll(xi, d, 0), 0.0)
            tr, ti = _cmul(er, ei, sr, si)
            xr, xi = xr + tr, xi + ti
        if chain:
            c_r, c_i = cr_s[...], ci_s[...]
        else:
            c_r, c_i = x0_ref[g, 0:1, :], x0_ref[g, 1:2, :]
        tr, ti = _cmul(pr, pi_, c_r, c_i)
        xr, xi = xr + tr, xi + ti
        xr_s[pl.ds(r0, SUBLANES), :] = xr
        xi_s[pl.ds(r0, SUBLANES), :] = xi
        if chain:
            cr_s[...] = xr[SUBLANES - 1:SUBLANES, :]
            ci_s[...] = xi[SUBLANES - 1:SUBLANES, :]
        else:
            st_ref[g, 0:1, :] = xr[SUBLANES - 1:SUBLANES, :]
            st_ref[g, 1:2, :] = xi[SUBLANES - 1:SUBLANES, :]
        return carry

    lax.fori_loop(0, n_grp, group, 0)

    if chain:
        @pl.when(j == pl.num_programs(1) - 1)
        def _():
            st_ref[0, 0:1, :] = cr_s[...]
            st_ref[0, 1:2, :] = ci_s[...]

    y = (jnp.dot(xr_s[...].astype(BF16), cre_ref[...], preferred_element_type=F32)
         - jnp.dot(xi_s[...].astype(BF16), cim_ref[...], preferred_element_type=F32) + dsk_ref[...] * u)
    zb = _gelu_tanh(y)
    o_ref[...] = zb * _sigmoid(jnp.dot(zb.astype(BF16), wglu_ref[...], preferred_element_type=F32) + bglu_ref[...])


def _s5_call(u2, b, l, x0, w):
    n_state = x0.shape[-1]
    chain = l % 128 == 0
    if chain:
        tl = min(l, 512)
        grid = (b, l // tl)
        rows = tl
        u_spec = pl.BlockSpec((tl, BRANCH_W), lambda i, j: (i * (l // tl) + j, 0))
        x0_spec = pl.BlockSpec((1, 2, n_state), lambda i, j: (i, 0, 0))
    else:
        assert l == SUBLANES
        grid = (1, 1)
        rows = b * l
        u_spec = pl.BlockSpec((rows, BRANCH_W), lambda i, j: (0, 0))
        x0_spec = pl.BlockSpec((b, 2, n_state), lambda i, j: (0, 0, 0))
    full = lambda a: pl.BlockSpec(a.shape, lambda i, j: (0,) * a.ndim)
    params = [w["a_re"], w["a_im"], w["log_dt"], w["b_re_bd"], w["b_im_bd"], w["c_re_bd"], w["c_im_bd"],
              w["d_skip"], w["w_glu"], w["b_glu"]]
    return pl.pallas_call(
        functools.partial(_s5_kernel, chain),
        grid=grid,
        in_specs=[u_spec, x0_spec] + [full(a) for a in params],
        out_specs=[u_spec, x0_spec],
        out_shape=[jax.ShapeDtypeStruct((b * l, BRANCH_W), F32), jax.ShapeDtypeStruct(x0.shape, F32)],
        scratch_shapes=[pltpu.VMEM((rows, n_state), F32), pltpu.VMEM((rows, n_state), F32),
                        pltpu.VMEM((1, n_state), F32), pltpu.VMEM((1, n_state), F32)],
        compiler_params=_cparams("parallel", "arbitrary"),
        name="s5",
    )(u2, x0, *params)


def _s5_params(a_re, a_im, b_re, b_im, c_re, c_im, d_skip, log_dt, w_glu, b_glu):
    g, p = a_re.shape
    cg = b_re.shape[-1]
    eye = jnp.eye(g, dtype=F32)
    b_bd = lambda t: jnp.einsum("gpc,gh->gchp", t, eye).reshape(g * cg, g * p)
    c_bd = lambda t: jnp.einsum("gcp,gh->gphc", t, eye).reshape(g * p, g * cg)
    return {
        "a_re": a_re.reshape(1, g * p), "a_im": a_im.reshape(1, g * p),
        "log_dt": jnp.repeat(log_dt, p).reshape(1, g * p),
        "b_re_bd": b_bd(b_re), "b_im_bd": b_bd(b_im),
        "c_re_bd": c_bd(c_re).astype(BF16), "c_im_bd": c_bd(c_im).astype(BF16),
        "d_skip": d_skip.reshape(1, g * cg), "w_glu": w_glu.astype(BF16), "b_glu": b_glu.reshape(1, -1),
    }


TK = 128


def _stack_heads_bd(q):
    qf = q.astype(F32)
    return jnp.concatenate([jnp.where(_head_mask(qf.shape, h), qf, 0.0) for h in range(N_HEADS)], axis=0).astype(BF16)


def _unstack_heads_bd(acc, tq):
    out = jnp.zeros((tq, acc.shape[-1]), F32)
    for h in range(N_HEADS):
        blk = acc[h * tq:(h + 1) * tq]
        out = out + jnp.where(_head_mask(blk.shape, h), blk, 0.0)
    return out


def _suffix_matrix(tk):
    r = _row_iota((2 * tk, tk)) % tk
    c = _lane_iota((2 * tk, tk))
    return jnp.where(r > c, 1.0, 0.0).astype(BF16)


def _sb_scores(qbd, k16, transposed):
    return jnp.dot(qbd, k16, preferred_element_type=F32) if transposed else _dot_nt(qbd, k16)


def _sb_tile(qbd, k16, v16, mask, u2, carry, acc, transposed):
    return _sb_tile_z(_sb_scores(qbd, k16, transposed), v16, mask, u2, carry, acc, transposed)


def _sb_logw(z, mask, u2):
    log_beta = jnp.minimum(z, 0.0) - jnp.log(1.0 + jnp.exp(-jnp.abs(z)))
    l1m = log_beta - z
    if mask is not None:
        l1m = jnp.where(mask, l1m, 0.0)
    hi = lax.bitcast_convert_type(lax.bitcast_convert_type(l1m, jnp.uint32) & jnp.uint32(0xFFFF0000), F32)
    lo = l1m - hi
    suffix = jnp.dot(jnp.concatenate([hi.astype(BF16), lo.astype(BF16)], axis=1), u2, preferred_element_type=F32)
    return log_beta + suffix, suffix[:, 0:1] + l1m[:, 0:1]


def _sb_accumulate(logw, total, v16, mask, carry, acc, transposed):
    w = jnp.exp(logw + carry)
    if mask is not None:
        w = jnp.where(mask, w, 0.0)
    w16 = w.astype(BF16)
    acc = acc + (_dot_nt(w16, v16) if transposed else jnp.dot(w16, v16, preferred_element_type=F32))
    return carry + total, acc


def _sb_tile_z(z, v16, mask, u2, carry, acc, transposed):
    logw, total = _sb_logw(z, mask, u2)
    return _sb_accumulate(logw, total, v16, mask, carry, acc, transposed)


TKP = 256
SB_DEAD_LOG = -104.0


def _sb_prompt_kernel(q_ref, kv_ref, o_ref):
    j = pl.program_id(1)
    tq = q_ref.shape[0]
    qbd = _stack_heads_bd(q_ref[...])
    u2 = _suffix_matrix(TKP)
    rows = N_HEADS * tq
    n_full = (j * tq) // TKP
    qpos = j * tq + _row_iota((rows, TKP)) % tq
    diag_mask = (n_full * TKP + _lane_iota((rows, TKP))) < qpos

    def keys(kt):
        return kv_ref[pl.ds(pl.multiple_of(kt * TKP, TKP), TKP), 0:BRANCH_W]

    def values(kt):
        return kv_ref[pl.ds(pl.multiple_of(kt * TKP, TKP), TKP), BRANCH_W:2 * BRANCH_W]

    carry, acc = _sb_tile(qbd, keys(n_full), values(n_full), diag_mask, u2,
                          jnp.zeros((rows, 1), F32), jnp.zeros((rows, BRANCH_W), F32), False)

    def cond(st):
        it, carry, _ = st
        return (it < n_full) & (jnp.max(carry) >= SB_DEAD_LOG)

    def body(st):
        it, carry, acc = st
        kt = n_full - 1 - it
        carry, acc = _sb_tile(qbd, keys(kt), values(kt), None, u2, carry, acc, False)
        return it + 1, carry, acc

    _, carry, acc = lax.while_loop(cond, body, (jnp.int32(0), carry, acc))
    o_ref[...] = _unstack_heads_bd(acc, tq)


def _sb_prompt_call(sbq, sbkv16, b, l):
    tq = TK
    n = l // tq
    assert l % TKP == 0
    return pl.pallas_call(
        _sb_prompt_kernel,
        grid=(b, n),
        in_specs=[pl.BlockSpec((tq, BRANCH_W), lambda i, j: (i * n + j, 0)),
                  pl.BlockSpec((l, 2 * BRANCH_W), lambda i, j: (i, 0))],
        out_specs=pl.BlockSpec((tq, BRANCH_W), lambda i, j: (i * n + j, 0)),
        out_shape=jax.ShapeDtypeStruct((b * l, BRANCH_W), F32),
        compiler_params=_cparams("parallel", "arbitrary"),
        name="sb_prompt",
    )(sbq, sbkv16)


PAGES_PER_STEP = 8


def _sb_decode_kernel(n_pg, pt_ref, q_ref, kvn_ref, *rest):
    page_refs = rest[:n_pg]
    o_ref, carry_s, acc_s = rest[n_pg:]
    s = pl.program_id(1)
    tq = q_ref.shape[1]
    rows = N_HEADS * tq
    qbd = _stack_heads_bd(q_ref[0])
    u2 = _suffix_matrix(TK)

    @pl.when(s == 0)
    def _():
        mask = _lane_iota((rows, TK)) < (_row_iota((rows, TK)) % tq)
        kvn = kvn_ref[0]
        carry, acc = _sb_tile(qbd, kvn[0:BRANCH_W, :], kvn[BRANCH_W:, :], mask, u2,
                              jnp.zeros((rows, 1), F32), jnp.zeros((rows, BRANCH_W), F32), True)
        carry_s[...] = carry
        acc_s[...] = acc

    @pl.when(jnp.max(carry_s[...]) >= SB_DEAD_LOG)
    def _():
        carry, acc = carry_s[...], acc_s[...]
        group = 2 if n_pg % 2 == 0 else 1
        u2g = _suffix_matrix(group * TK) if group > 1 else u2
        for k in range(0, n_pg, group):
            kv = jnp.concatenate([page_refs[k + g][0, 0] for g in reversed(range(group))], axis=1).astype(BF16)
            carry, acc = _sb_tile(qbd, kv[0:BRANCH_W, :], kv[BRANCH_W:, :], None, u2g, carry, acc, True)
        carry_s[...] = carry
        acc_s[...] = acc

    @pl.when(s == pl.num_programs(1) - 1)
    def _():
        o_ref[0] = _unstack_heads_bd(acc_s[...], tq)


def _sb_decode_call(sbq3, kvn_t16, cache_t, layer, page_table):
    b, tq, _ = sbq3.shape
    n_pages = page_table.shape[1]
    n_pg = math.gcd(PAGES_PER_STEP, n_pages)
    n_steps = n_pages // n_pg

    def page_spec(k):
        return pl.BlockSpec((1, 1, 2 * BRANCH_W, TK),
                            lambda i, s, pt: (layer, pt[i, n_pages - 1 - (s * n_pg + k)], 0, 0))

    grid_spec = pltpu.PrefetchScalarGridSpec(
        num_scalar_prefetch=1,
        grid=(b, n_steps),
        in_specs=[pl.BlockSpec((1, tq, BRANCH_W), lambda i, s, pt: (i, 0, 0)),
                  pl.BlockSpec((1, 2 * BRANCH_W, TK), lambda i, s, pt: (i, 0, 0))]
                 + [page_spec(k) for k in range(n_pg)],
        out_specs=pl.BlockSpec((1, tq, BRANCH_W), lambda i, s, pt: (i, 0, 0)),
        scratch_shapes=[pltpu.VMEM((N_HEADS * tq, 1), F32), pltpu.VMEM((N_HEADS * tq, BRANCH_W), F32)],
    )
    return pl.pallas_call(
        functools.partial(_sb_decode_kernel, n_pg),
        grid_spec=grid_spec,
        out_shape=jax.ShapeDtypeStruct((b, tq, BRANCH_W), F32),
        compiler_params=_cparams("parallel", "arbitrary"),
        name="sb_decode",
    )(page_table, sbq3, kvn_t16, *([cache_t] * n_pg))


def _round_up(x, m):
    return -(-x // m) * m


def _stack_heads_128(q):
    qf = q.astype(F32)
    tq = qf.shape[0]
    low = _lane_iota((tq, LANES)) < HEAD_DIM
    parts = []
    for h in range(N_HEADS):
        half = qf[:, LANES * (h // 2):LANES * (h // 2 + 1)]
        if h % 2 == 1:
            half = pltpu.roll(half, HEAD_DIM, 1)
        parts.append(jnp.where(low, half, 0.0))
    return jnp.concatenate(parts, axis=0)


def _compress(rows_ref, wc_ref, n):
    p0 = jnp.zeros((n, LANES), F32)
    p1 = jnp.zeros((n, LANES), F32)
    for s in range(CMP_STRIDE):
        x = rows_ref[pl.ds(s, n, stride=CMP_STRIDE), :].astype(BF16)
        p0 = p0 + jnp.dot(x, wc_ref[s], preferred_element_type=F32)
        p1 = p1 + jnp.dot(x, wc_ref[CMP_STRIDE + s], preferred_element_type=F32)
    return p0 + pltpu.roll(p1, n - 1, 0)


def _compress_chunks(xs_ref, wcc_ref):
    n = xs_ref.shape[1]
    p = jnp.zeros((n, 2 * LANES), F32)
    for i in range(CMP_STRIDE // 2):
        p = p + jnp.dot(xs_ref[i].astype(BF16), wcc_ref[i], preferred_element_type=F32)
    return p[:, 0:LANES] + pltpu.roll(p[:, LANES:2 * LANES], n - 1, 0)


def _cmp_pairs(wc):
    first, second = wc[:CMP_STRIDE], wc[CMP_STRIDE:]
    both = jnp.concatenate([first, second], axis=2)
    return both.reshape(CMP_STRIDE // 2, 2 * LANES, 2 * LANES)


def _cmp_select(qp16, kvc, qpos_rows, qpos_q, tq, nsp, n_top):
    n = kvc.shape[0]
    kvc16 = kvc.astype(BF16)
    s = _dot_nt(qp16, kvc16)
    cmask = (_lane_iota(s.shape) * CMP_STRIDE + (CMP_BLOCK - 1)) <= qpos_rows
    s = jnp.where(cmask, s, NEG_BIG)
    m = jnp.max(s, axis=-1, keepdims=True)
    m = jnp.where(m > 0.5 * NEG_BIG, m, 0.0)
    e = jnp.where(cmask, jnp.exp(s - m), 0.0)
    p16 = (e / jnp.maximum(jnp.sum(e, axis=-1, keepdims=True), 1e-30)).astype(BF16)
    o_cmp = jnp.dot(p16, kvc16, preferred_element_type=F32)
    ci = _row_iota((n, nsp)) * CMP_STRIDE
    sj = _lane_iota((n, nsp)) * SLC_BLOCK
    c2s = jnp.where((ci < sj + SLC_BLOCK) & (sj < ci + CMP_BLOCK), 1.0, 0.0).astype(BF16)
    imp4 = jnp.dot(p16, c2s, preferred_element_type=F32)
    imp = imp4[0:tq]
    for h in range(1, N_HEADS):
        imp = imp + imp4[h * tq:(h + 1) * tq]

    blk = _lane_iota((tq, nsp))
    blkf = blk.astype(F32)
    cur = qpos_q // SLC_BLOCK
    forced = (blk == 0) | (blk == cur) | (blk == cur - 1)
    val = jnp.where(forced, jnp.inf, imp)
    avail = jnp.where(blk * SLC_BLOCK <= qpos_q, 1.0, 0.0)
    sel = jnp.zeros((tq, nsp), F32)
    for _ in range(n_top):
        vm = jnp.where(avail > 0.5, val, -jnp.inf)
        mx = jnp.max(vm, axis=-1, keepdims=True)
        cand = (avail > 0.5) & (vm == mx)
        first = jnp.min(jnp.where(cand, blkf, float(nsp)), axis=-1, keepdims=True)
        pick = blkf == first
        sel = jnp.where(pick, 1.0, sel)
        avail = jnp.where(pick, 0.0, avail)
    return o_cmp, sel.astype(BF16)


def _cmp_select_t(qp_t16, kvc, kvc_t, c2s_t16, j, tq, n_top):
    n = kvc.shape[0]
    nsp = c2s_t16.shape[0]
    rows4 = N_HEADS * tq
    qpos_cols = j * tq + _lane_iota((n, rows4)) % tq
    s = jnp.dot(kvc.astype(BF16), qp_t16, preferred_element_type=F32)
    cmask = (_row_iota((n, rows4)) * CMP_STRIDE + (CMP_BLOCK - 1)) <= qpos_cols
    s = jnp.where(cmask, s, NEG_BIG)
    m = jnp.max(s, axis=0, keepdims=True)
    m = jnp.where(m > 0.5 * NEG_BIG, m, 0.0)
    e = jnp.where(cmask, jnp.exp(s - m), 0.0)
    p16 = (e / jnp.maximum(jnp.sum(e, axis=0, keepdims=True), 1e-30)).astype(BF16)
    o_cmp_t = jnp.dot(kvc_t.astype(BF16), p16, preferred_element_type=F32)
    imp4 = jnp.dot(c2s_t16, p16, preferred_element_type=F32)
    imp = imp4[:, 0:tq]
    for h in range(1, N_HEADS):
        imp = imp + imp4[:, h * tq:(h + 1) * tq]

    blk = _row_iota((nsp, tq))
    blkf = blk.astype(F32)
    qpos = j * tq + _lane_iota((nsp, tq))
    cur = qpos // SLC_BLOCK
    forced = (blk == 0) | (blk == cur) | (blk == cur - 1)
    val = jnp.where(forced, jnp.inf, imp)
    avail = jnp.where(blk * SLC_BLOCK <= qpos, 1.0, 0.0)
    sel = jnp.zeros((nsp, tq), F32)
    for _ in range(n_top):
        vm = jnp.where(avail > 0.5, val, -jnp.inf)
        mx = jnp.max(vm, axis=0, keepdims=True)
        cand = (avail > 0.5) & (vm == mx)
        first = jnp.min(jnp.where(cand, blkf, float(nsp)), axis=0, keepdims=True)
        pick = blkf == first
        sel = jnp.where(pick, 1.0, sel)
        avail = jnp.where(pick, 0.0, avail)
    return o_cmp_t, sel


def _attn_tile(q16, kv16, mask, m, l, acc, transposed=False):
    s = jnp.dot(q16, kv16, preferred_element_type=F32) if transposed else _dot_nt(q16, kv16)
    s = jnp.where(mask, s, NEG_BIG)
    m_new = jnp.maximum(m, jnp.max(s, axis=-1, keepdims=True))
    p = jnp.where(mask, jnp.exp(s - m_new), 0.0)
    alpha = jnp.exp(m - m_new)
    l = alpha * l + jnp.sum(p, axis=-1, keepdims=True)
    p16 = p.astype(BF16)
    acc = alpha * acc + (_dot_nt(p16, kv16) if transposed else jnp.dot(p16, kv16, preferred_element_type=F32))
    return m_new, l, acc


def _attn_update(s, kv16, m, l, acc):
    m_new = jnp.maximum(m, jnp.max(s, axis=-1, keepdims=True))
    p = jnp.exp(s - m_new)
    alpha = jnp.exp(m - m_new)
    l = alpha * l + jnp.sum(p, axis=-1, keepdims=True)
    acc = alpha * acc + jnp.dot(p.astype(BF16), kv16, preferred_element_type=F32)
    return m_new, l, acc


def _attn_init(rows, width):
    return jnp.full((rows, 1), NEG_BIG, F32), jnp.zeros((rows, 1), F32), jnp.zeros((rows, width), F32)


def _slc_tile(kt, carry, qr16, kv16, selb, qpos_rows, nsp, tk, transposed):
    er = _row_iota((nsp, tk))
    ec = _lane_iota((nsp, tk))
    expand = jnp.where(er == (tk // SLC_BLOCK) * kt + ec // SLC_BLOCK, 1.0, 0.0).astype(BF16)
    seltok = jnp.dot(selb, expand, preferred_element_type=F32)
    seltok4 = jnp.concatenate([seltok] * N_HEADS, axis=0)
    tok = kt * tk + _lane_iota(seltok4.shape)
    mask = (seltok4 > 0.5) & (tok <= qpos_rows)
    return _attn_tile(qr16, kv16, mask, *carry, transposed=transposed)


def _win_tile(kv16, kpos, carry, qw16, qpos_rows):
    dist = qpos_rows - kpos
    mask = (dist >= 0) & (dist < WINDOW)
    return _attn_tile(qw16, kv16, mask, *carry)


def _nsa_combine(g, o_c, o_s, o_w, tq):
    comb = []
    for h in range(N_HEADS):
        sl = slice(h * tq, (h + 1) * tq)
        comb.append(g[:, h:h + 1] * o_c[sl] + g[:, N_HEADS + h:N_HEADS + h + 1] * o_s[sl]
                    + g[:, 2 * N_HEADS + h:2 * N_HEADS + h + 1] * o_w[sl])
    low = _lane_iota((tq, LANES)) < HEAD_DIM
    out01 = jnp.where(low, pltpu.roll(comb[0], HEAD_DIM, 1), comb[1])
    out23 = jnp.where(low, pltpu.roll(comb[2], HEAD_DIM, 1), comb[3])
    return jnp.concatenate([out01, out23], axis=1)


def _nsa_prompt_kernel(n_top, nq_ref, nqr_ref, ng_ref, rowsc_ref, rows_ref, rowst_ref, win_ref, wint_ref, onehot_ref,
                       c2st_ref, wc_ref, o_ref, kvc_s, kvct_s):
    j = pl.program_id(1)
    tq = nq_ref.shape[0]

    @pl.when(j == 0)
    def _():
        kvc = _compress(rowsc_ref, wc_ref, kvc_s.shape[0])
        kvc_s[...] = kvc
        kvct_s[...] = kvc.T

    rows4 = N_HEADS * tq
    qp_t16 = _stack_heads_128(nq_ref[...]).T.astype(BF16)
    qw_t = _stack_heads_128(nqr_ref[...]).T
    qw_t16 = qw_t.astype(BF16)

    o_c_t, sel_t = _cmp_select_t(qp_t16, kvc_s[...], kvct_s[...], c2st_ref[...], j, tq, n_top)
    last = (j * tq) // TKP

    sel_bias_t = (sel_t - 1.0) * (-NEG_BIG)
    q_aug_t = jnp.concatenate([qw_t, jnp.concatenate([sel_bias_t] * N_HEADS, axis=1)], axis=0).astype(BF16)

    def slc_scores(kt):
        r0 = pl.multiple_of(kt * TKP, TKP)
        k_aug = jnp.concatenate([rows_ref[pl.ds(r0, TKP), :], onehot_ref[pl.ds(r0, TKP), :]], axis=1)
        return jnp.dot(k_aug, q_aug_t, preferred_element_type=F32)

    def slc_update(s, kt, m, l, acc):
        m_new = jnp.maximum(m, jnp.max(s, axis=0, keepdims=True))
        p = jnp.exp(s - m_new)
        alpha = jnp.exp(m - m_new)
        l = alpha * l + jnp.sum(p, axis=0, keepdims=True)
        vt = rowst_ref[:, pl.ds(pl.multiple_of(kt * TKP, TKP), TKP)]
        acc = alpha * acc + jnp.dot(vt, p.astype(BF16), preferred_element_type=F32)
        return m_new, l, acc

    def slc_body(kt, st):
        m, l, acc, s = st
        s_next = slc_scores(jnp.minimum(kt + 1, last))
        m, l, acc = slc_update(s, kt, m, l, acc)
        return m, l, acc, s_next

    init = (jnp.full((1, rows4), NEG_BIG, F32), jnp.zeros((1, rows4), F32), jnp.zeros((LANES, rows4), F32))
    m_s, l_s, acc_s, s_diag = lax.fori_loop(0, last, slc_body, (*init, slc_scores(0)))
    tok = last * TKP + _row_iota((TKP, rows4))
    qpos_cols = j * tq + _lane_iota((TKP, rows4)) % tq
    s_diag = jnp.where(tok <= qpos_cols, s_diag, NEG_BIG)
    _, l_s, acc_s = slc_update(s_diag, last, m_s, l_s, acc_s)
    o_s = (acc_s / l_s).T

    def win_tile(kt, mask, st):
        m, l, acc = st
        r0 = pl.multiple_of(kt * TKP, TKP)
        s = jnp.dot(win_ref[pl.ds(r0, TKP), :], qw_t16, preferred_element_type=F32)
        s = jnp.where(mask, s, NEG_BIG)
        m_new = jnp.maximum(m, jnp.max(s, axis=0, keepdims=True))
        p = jnp.exp(s - m_new)
        alpha = jnp.exp(m - m_new)
        l = alpha * l + jnp.sum(p, axis=0, keepdims=True)
        acc = alpha * acc + jnp.dot(wint_ref[:, pl.ds(r0, TKP)], p.astype(BF16), preferred_element_type=F32)
        return m_new, l, acc

    krow = _row_iota((TKP, rows4))
    st_w = win_tile(last, last * TKP + krow <= qpos_cols, init)
    kb = jnp.maximum(last - 1, 0)
    st_w = win_tile(kb, kb * TKP + krow < last * TKP, st_w)
    ka = jnp.maximum(last - 2, 0)
    kpos_a = ka * TKP + krow
    _, l_w, acc_w = win_tile(ka, (qpos_cols - kpos_a < WINDOW) & (kpos_a < (last - 1) * TKP), st_w)
    o_ref[...] = _nsa_combine(ng_ref[...], o_c_t.T, o_s, (acc_w / l_w).T, tq)


def _nsa_prompt_call(nq, nqr, ng, rowsc, rowss16, rowst16, win16, wint16, wc, b, l):
    tq = TK
    n = l // tq
    n_top = min(N_SEL, l // SLC_BLOCK)
    nsp = _round_up(l // SLC_BLOCK, LANES)
    onehot = (jnp.arange(l)[:, None] // SLC_BLOCK == jnp.arange(nsp)[None, :]).astype(BF16)
    cs = jnp.arange(l // CMP_STRIDE)[None, :] * CMP_STRIDE
    ss = jnp.arange(nsp)[:, None] * SLC_BLOCK
    c2s_t = ((cs < ss + SLC_BLOCK) & (ss < cs + CMP_BLOCK)).astype(BF16)
    qspec = lambda w: pl.BlockSpec((tq, w), lambda i, j: (i * n + j, 0))
    return pl.pallas_call(
        functools.partial(_nsa_prompt_kernel, n_top),
        grid=(b, n),
        in_specs=[qspec(BRANCH_W), qspec(BRANCH_W), qspec(LANES),
                  pl.BlockSpec((l, LANES), lambda i, j: (i, 0)),
                  pl.BlockSpec((l, LANES), lambda i, j: (i, 0)),
                  pl.BlockSpec((LANES, l), lambda i, j: (0, i)),
                  pl.BlockSpec((l, LANES), lambda i, j: (i, 0)),
                  pl.BlockSpec((LANES, l), lambda i, j: (0, i)),
                  pl.BlockSpec(onehot.shape, lambda i, j: (0, 0)),
                  pl.BlockSpec(c2s_t.shape, lambda i, j: (0, 0)),
                  pl.BlockSpec(wc.shape, lambda i, j: (0, 0, 0))],
        out_specs=qspec(BRANCH_W),
        out_shape=jax.ShapeDtypeStruct((b * l, BRANCH_W), F32),
        scratch_shapes=[pltpu.VMEM((l // CMP_STRIDE, LANES), F32), pltpu.VMEM((LANES, l // CMP_STRIDE), F32)],
        compiler_params=_cparams("parallel", "arbitrary"),
        name="nsa_prompt",
    )(nq, nqr, ng, rowsc, rowss16, rowst16, win16, wint16, onehot, c2s_t, wc)


def _nsa_decode_kernel(n_pg, past_len, n_cmp, n_top, pt_ref, nq_ref, nqr_ref, ng_ref, rown_ref, rownt_ref, cwin_ref,
                       winn_ref, onehot_ref, wcc_ref, *rest):
    page_refs = rest[:n_pg]
    o_ref, wst_ref, xs_s, rowst_s, win_s = rest[n_pg:]
    s = pl.program_id(1)
    tq = nq_ref.shape[1]
    n_steps = rowst_s.shape[0]
    slab = n_pg * TK
    chunks_per_page = TK // CMP_STRIDE
    o_idx = _row_iota((TK, TK))
    perm = jnp.where(_lane_iota((TK, TK)) == CMP_STRIDE * (o_idx % chunks_per_page) + o_idx // chunks_per_page,
                     1.0, 0.0).astype(BF16)
    for k, pg in enumerate(page_refs):
        c0 = pl.multiple_of((s * n_pg + k) * chunks_per_page, chunks_per_page)
        by_offset = _dot_nt(perm, pg[0, 0, 0:LANES, :])
        for off in range(CMP_STRIDE):
            xs_s[off // 2, pl.ds(c0, chunks_per_page), (off % 2) * LANES:(off % 2 + 1) * LANES] = (
                by_offset[off * chunks_per_page:(off + 1) * chunks_per_page])
        rowst_s[s, :, k * TK:(k + 1) * TK] = pg[0, 0, LANES:2 * LANES, :]

    @pl.when(s == pl.num_programs(1) - 1)
    def _():
        w_rows = win_s.shape[0]
        wb = cwin_ref.shape[2]
        nsp = onehot_ref.shape[0]
        c_new = past_len // CMP_STRIDE
        first_row = _row_iota((SUBLANES, LANES)) == 0
        for off in range(CMP_STRIDE):
            new = rown_ref[0, off:off + 1, 0:LANES] if off < tq else jnp.zeros((1, LANES), F32)
            xs_s[off // 2, c_new:c_new + SUBLANES, (off % 2) * LANES:(off % 2 + 1) * LANES] = (
                jnp.where(first_row, new, 0.0))
        win_s[0:wb, :] = cwin_ref[0, 0]
        win_s[wb:wb + tq, :] = winn_ref[0]
        win_s[wb + tq:w_rows, :] = jnp.zeros((w_rows - wb - tq, LANES), F32)
        wst_ref[0] = win_s[tq:wb + tq, :]

        rows4 = N_HEADS * tq
        qpos_rows = past_len + lax.broadcasted_iota(jnp.int32, (rows4, 1), 0) % tq
        qpos_q = past_len + lax.broadcasted_iota(jnp.int32, (tq, 1), 0)
        qp16 = _stack_heads_128(nq_ref[0]).astype(BF16)
        qw16 = _stack_heads_128(nqr_ref[0]).astype(BF16)

        kvc = _compress_chunks(xs_s, wcc_ref)
        o_c, selb = _cmp_select(qp16, kvc, qpos_rows, qpos_q, tq, nsp, n_top)

        sel_bias = (selb.astype(F32) - 1.0) * (-NEG_BIG)
        q_aug = jnp.concatenate([_stack_heads_128(nqr_ref[0]), jnp.concatenate([sel_bias] * N_HEADS, axis=0)],
                                axis=1).astype(BF16)

        def slc_step(s_t, kv_t, st):
            m, l, acc = st
            m_new = jnp.maximum(m, jnp.max(s_t, axis=-1, keepdims=True))
            p = jnp.exp(s_t - m_new)
            alpha = jnp.exp(m - m_new)
            l = alpha * l + jnp.sum(p, axis=-1, keepdims=True)
            return m_new, l, alpha * acc + _dot_nt(p.astype(BF16), kv_t)

        def slab_body(kt, st):
            kv_t = rowst_s[kt].astype(BF16)
            k0 = pl.multiple_of(kt * slab, slab)
            k_aug_t = jnp.concatenate([kv_t, onehot_ref[:, pl.ds(k0, slab)]], axis=0)
            return slc_step(jnp.dot(q_aug, k_aug_t, preferred_element_type=F32), kv_t, st)

        st = lax.fori_loop(0, n_steps, slab_body, _attn_init(rows4, LANES))
        kv_t = rownt_ref[0].astype(BF16)
        s_new = jnp.dot(q_aug, jnp.concatenate([kv_t, onehot_ref[:, past_len:past_len + TK]], axis=0),
                        preferred_element_type=F32)
        s_new = jnp.where(past_len + _lane_iota((rows4, TK)) <= qpos_rows, s_new, NEG_BIG)
        _, l_s, acc_s = slc_step(s_new, kv_t, st)
        o_s = acc_s / l_s

        carry = _attn_init(rows4, LANES)
        for kt in range(w_rows // TK):
            kpos = past_len - wb + kt * TK + _lane_iota((rows4, TK))
            carry = _win_tile(win_s[kt * TK:(kt + 1) * TK, :].astype(BF16), kpos, carry, qw16, qpos_rows)
        _, l_w, acc_w = carry
        o_ref[0] = _nsa_combine(ng_ref[0], o_c, o_s, acc_w / l_w, tq)


def _nsa_decode_call(nq3, nqr3, ng3, rown3, rownt3, winn3, cache_nsa_t, cache_win4, layer, page_table, wc):
    b, tq, _ = nq3.shape
    n_pages = page_table.shape[1]
    past_len = n_pages * TK
    wb = cache_win4.shape[2]
    n_pg = math.gcd(2 * PAGES_PER_STEP, n_pages)
    n_steps = n_pages // n_pg
    t_pad = _round_up(past_len + tq, SLC_BLOCK)
    n_cmp = _round_up(t_pad // CMP_STRIDE, SUBLANES)
    t_rows = _round_up(n_cmp * CMP_STRIDE, TK)
    assert t_rows == past_len + TK and tq <= SUBLANES and n_cmp - past_len // CMP_STRIDE == SUBLANES
    w_rows = _round_up(wb + tq, TK)
    n_top = min(N_SEL, t_pad // SLC_BLOCK)
    nsp = _round_up(t_pad // SLC_BLOCK, LANES)
    onehot_t = (jnp.arange(nsp)[:, None] == jnp.arange(t_rows)[None, :] // SLC_BLOCK).astype(BF16)
    wcc = _cmp_pairs(wc)

    def page_spec(k):
        return pl.BlockSpec((1, 1, BRANCH_W, TK), lambda i, s, pt: (layer, pt[i, s * n_pg + k], 0, 0))

    row3 = lambda r, w: pl.BlockSpec((1, r, w), lambda i, s, pt: (i, 0, 0))
    grid_spec = pltpu.PrefetchScalarGridSpec(
        num_scalar_prefetch=1,
        grid=(b, n_steps),
        in_specs=[row3(tq, BRANCH_W), row3(tq, BRANCH_W), row3(tq, LANES), row3(tq, BRANCH_W), row3(LANES, TK),
                  pl.BlockSpec((1, 1, wb, LANES), lambda i, s, pt: (layer, i, 0, 0)),
                  row3(tq, LANES),
                  pl.BlockSpec(onehot_t.shape, lambda i, s, pt: (0, 0)),
                  pl.BlockSpec(wcc.shape, lambda i, s, pt: (0, 0, 0))]
                 + [page_spec(k) for k in range(n_pg)],
        out_specs=[row3(tq, BRANCH_W), row3(wb, LANES)],
        scratch_shapes=[pltpu.VMEM((CMP_STRIDE // 2, n_cmp, 2 * LANES), F32),
                        pltpu.VMEM((n_steps, LANES, n_pg * TK), F32),
                        pltpu.VMEM((w_rows, LANES), F32)],
    )
    return pl.pallas_call(
        functools.partial(_nsa_decode_kernel, n_pg, past_len, n_cmp, n_top),
        grid_spec=grid_spec,
        out_shape=[jax.ShapeDtypeStruct((b, tq, BRANCH_W), F32), jax.ShapeDtypeStruct((b, wb, LANES), F32)],
        compiler_params=_cparams("parallel", "arbitrary"),
        name="nsa_decode",
    )(page_table, nq3, nqr3, ng3, rown3, rownt3, cache_win4, winn3, onehot_t, wcc, *([cache_nsa_t] * n_pg))


def _cmp_weights(w_cmp_k, w_cmp_v):
    z = jnp.zeros_like(w_cmp_k)
    top = jnp.concatenate([w_cmp_k, z], axis=2)
    bot = jnp.concatenate([z, w_cmp_v], axis=2)
    return jnp.concatenate([top, bot], axis=1).astype(BF16)


def _rope_tables(pos):
    posf = pos.astype(F32)[:, None]
    jj = np.arange(LANES) % HEAD_DIM

    def table(half, theta, lanes_on):
        inv = theta ** (-jnp.arange(half, dtype=F32) / half)
        ang = posf * inv[None, :]
        cos_l = jnp.cos(ang)[:, jj % half]
        sin_l = jnp.sin(ang)[:, jj % half] * jnp.where(jj < half, -1.0, 1.0).astype(F32)[None, :]
        on = jnp.asarray((jj < 2 * half) & lanes_on)[None, :]
        return jnp.where(on, cos_l, 1.0), jnp.where(on, sin_l, 0.0)

    every = np.ones(LANES, bool)
    first = np.arange(LANES) < HEAD_DIM
    parts = (*table(HEAD_DIM // 2, RET_THETA, every), *table(ROPE_DIM // 2, ROPE_THETA, every),
             *table(ROPE_DIM // 2, ROPE_THETA, first))
    return jnp.concatenate(parts, axis=1)


def _layer_weights(i, p):
    w_t = p["w_in_t"][:, i, :]
    d = w_t.shape[1]
    c_mg = w_t.shape[0] - N_BRANCH * d
    c_nsa = 8 * BRANCH_W
    w_nsa = w_t[c_nsa:c_mg]
    w_nsa = jnp.pad(w_nsa, ((0, 3 * BRANCH_W - w_nsa.shape[0]), (0, 0)))
    w = {
        "w_ret": w_t[0:4 * BRANCH_W].astype(BF16), "w_u": w_t[4 * BRANCH_W:5 * BRANCH_W].astype(BF16),
        "w_sb": w_t[5 * BRANCH_W:8 * BRANCH_W].astype(BF16), "w_nsa": w_nsa.astype(BF16),
        "w_mg": w_t[c_mg:].astype(BF16), "w_branch": p["w_branch"][i].astype(BF16),
        "w_out": p["w_out"][i].astype(BF16), "ln1_g": p["ln1_g"][i][None, :], "ln1_b": p["ln1_b"][i][None, :],
        "w_up": p["w_ffn_up"][i].astype(BF16), "w_dn": p["w_ffn_down"][i].astype(BF16),
        "ln2_g": p["ln2_g"][i][None, :], "ln2_b": p["ln2_b"][i][None, :],
        "wc": _cmp_weights(p["w_cmp_k"][i], p["w_cmp_v"][i]),
    }
    s5 = _s5_params(p["ssm_a_re"][i], p["ssm_a_im"][i], p["ssm_b_re"][i], p["ssm_b_im"][i], p["ssm_c_re"][i],
                    p["ssm_c_im"][i], p["ssm_d"][i], p["ssm_log_dt"][i], p["w_glu"][i], p["b_glu"][i])
    return w, s5


def _ssm_rows(s):
    b = s.shape[0]
    return jnp.moveaxis(s, -1, 1).reshape(b, 2, -1)


def _ssm_state(rows, g):
    b = rows.shape[0]
    return jnp.moveaxis(rows.reshape(b, 2, g, -1), 1, -1)


def _layer(x, mod, tabs, w, s5, alpha, past):
    b, l, d = x.shape
    t = b * l
    x2 = x.reshape(t, d)
    g = s5["a_re"].shape[-1] // SSM_STATE
    if past is None:
        tm = min(l, 512)
        tpb = l // tm
        mod_in, tab_in = mod, tabs
    else:
        tm, tpb = t, None
        mod_in = jnp.repeat(mod.reshape(b, 8 * d), l, axis=0)
        tab_in = jnp.tile(tabs, (b, 1))
    ret, u, sbq, sbkv, sbkv16, nq, nqr, rows, win, ng, rowsc, rowss16, win16, *rowst16 = _in_call(
        x2, mod_in, tab_in, w, tpb, tm)

    if past is None:
        r0 = jnp.zeros((b, BRANCH_W, BRANCH_W), F32)
        x0 = jnp.zeros((b, 2, g * SSM_STATE), F32)
    else:
        r0 = _bd_from_heads(past["ret"])
        x0 = _ssm_rows(past["ssm"])
    out_a, r_bd = _ret_call(ret, b, l, r0)
    out_b, s_rows = _s5_call(u, b, l, x0, s5)

    if past is None:
        out_c = _sb_prompt_call(sbq, sbkv16, b, l)
        out_d = _nsa_prompt_call(nq, nqr, ng, rowsc, rowss16, rowst16[0], win16, rowst16[1], w["wc"], b, l)
        win_state = win.reshape(b, l, 2, HEAD_DIM)[:, l - min(WINDOW, l):]
    else:
        kvn_t = jnp.swapaxes(jnp.pad(sbkv16.reshape(b, l, 2 * BRANCH_W), ((0, 0), (0, TK - l), (0, 0))), 1, 2)
        out_c = _sb_decode_call(sbq.reshape(b, l, BRANCH_W), kvn_t, past["sb"], past["layer"], past["page_table"])
        rows3 = rows.reshape(b, l, BRANCH_W)
        rown_t = jnp.swapaxes(jnp.pad(rows3[:, :, LANES:], ((0, 0), (0, TK - l), (0, 0))), 1, 2)
        out_d, wst = _nsa_decode_call(nq.reshape(b, l, BRANCH_W), nqr.reshape(b, l, BRANCH_W),
                                      ng.reshape(b, l, LANES), rows3, rown_t,
                                      win.reshape(b, l, LANES), past["nsa"], past["win"], past["layer"],
                                      past["page_table"], w["wc"])
        out_c = out_c.reshape(t, BRANCH_W)
        out_d = out_d.reshape(t, BRANCH_W)
        win_state = wst.reshape(b, -1, 2, HEAD_DIM)

    tm2 = min(tm, 512)
    tpb2 = None if tpb is None else l // tm2
    x1 = _merge_call(x2, mod_in, [out_a, out_b, out_c, out_d], w, tpb2, tm2, alpha)
    xo = _ffn_call(x1, mod_in, w, tpb2, tm2, alpha)
    state = (sbkv.reshape(b, l, 2, N_HEADS, HEAD_DIM), rows.reshape(b, l, 4, HEAD_DIM), win_state,
             _heads_from_bd(r_bd), _ssm_state(s_rows, g))
    return xo.reshape(b, l, d), state


def kernel(x_prompt, x_sample, c_prompt, c_sample, cache_sb, cache_nsa, cache_win, state_ret, state_ssm, page_table, w_ada, b_ada, w_in, ssm_a_re, ssm_a_im, ssm_b_re, ssm_b_im, ssm_c_re, ssm_c_im, ssm_d, ssm_log_dt, w_glu, b_glu, w_cmp_k, w_cmp_v, w_branch, w_out, ln1_g, ln1_b, w_ffn_up, w_ffn_down, ln2_g, ln2_b):
    p = dict(w_in_t=jnp.transpose(w_in, (2, 0, 1)), ssm_a_re=ssm_a_re, ssm_a_im=ssm_a_im, ssm_b_re=ssm_b_re, ssm_b_im=ssm_b_im,
             ssm_c_re=ssm_c_re, ssm_c_im=ssm_c_im, ssm_d=ssm_d, ssm_log_dt=ssm_log_dt, w_glu=w_glu, b_glu=b_glu,
             w_cmp_k=w_cmp_k, w_cmp_v=w_cmp_v, w_branch=w_branch, w_out=w_out, ln1_g=ln1_g, ln1_b=ln1_b,
             w_ffn_up=w_ffn_up, w_ffn_down=w_ffn_down, ln2_g=ln2_g, ln2_b=ln2_b)
    depth, d = w_ada.shape[0], w_ada.shape[1]
    bp, lp, _ = x_prompt.shape
    bs, ls, _ = x_sample.shape
    n_pool, page = cache_sb.shape[1], cache_sb.shape[2]
    past_len = page_table.shape[1] * page
    alpha = (2.0 * depth) ** 0.25

    c_all = jnp.concatenate([c_prompt, c_sample], axis=0)
    m_rows = _round_up(bp + bs, SUBLANES)
    c_all = jnp.pad(c_all, ((0, m_rows - bp - bs), (0, 0)))
    mods = _ada_call(c_all, w_ada, b_ada)

    tabs_p = _rope_tables(jnp.arange(lp, dtype=jnp.int32))
    tabs_s = _rope_tables(past_len + jnp.arange(ls, dtype=jnp.int32))
    cache_sb4 = jnp.transpose(cache_sb, (0, 1, 3, 4, 5, 2)).reshape(depth, n_pool, 2 * BRANCH_W, page)
    cache_nsa4 = jnp.transpose(cache_nsa, (0, 1, 3, 4, 2)).reshape(depth, n_pool, BRANCH_W, page)
    cache_win4 = cache_win.reshape(depth, bs, cache_win.shape[2], LANES)

    xp, xs = x_prompt, x_sample
    st_p = [[] for _ in range(5)]
    st_s = [[] for _ in range(5)]
    for i in range(depth):
        w, s5 = _layer_weights(i, p)
        mod_i = jnp.pad(mods[i].reshape(m_rows, 6, d), ((0, 0), (0, 2), (0, 0)))
        xp, new_p = _layer(xp, mod_i[:bp], tabs_p, w, s5, alpha, None)
        past = dict(sb=cache_sb4, nsa=cache_nsa4, win=cache_win4, ret=state_ret[i], ssm=state_ssm[i],
                    layer=i, page_table=page_table)
        xs, new_s = _layer(xs, mod_i[bp:bp + bs], tabs_s, w, s5, alpha, past)
        for k in range(5):
            st_p[k].append(new_p[k])
            st_s[k].append(new_s[k])
    sb_p, nsa_p, win_p, ret_p, ssm_p = [jnp.stack(s, axis=0) for s in st_p]
    sb_s, nsa_s, win_s, ret_s, ssm_s = [jnp.stack(s, axis=0) for s in st_s]
    return (xp, xs, sb_p, sb_s, nsa_p, nsa_s, win_p, win_s, ret_p, ret_s, ssm_p, ssm_s)
```

```python
import functools
import math

import jax
import jax.numpy as jnp
import numpy as np
from jax import lax
from jax.experimental import pallas as pl
from jax.experimental.pallas import tpu as pltpu

F32 = jnp.float32
BF16 = jnp.bfloat16

LANES = 128
SUBLANES = 8
VMEM_LIMIT_BYTES = 56 * 1024 * 1024

HEAD_DIM = 64
N_HEADS = 4
BRANCH_W = N_HEADS * HEAD_DIM
N_BRANCH = 4
RET_THETA = 10000.0
RET_CHUNK = 128
SSM_GROUP_CH = 16
SSM_STATE = 64
CMP_BLOCK = 32
CMP_STRIDE = 16
SLC_BLOCK = 64
N_SEL = 16
WINDOW = 512
ROPE_THETA = 500000.0
ROPE_DIM = HEAD_DIM // 4
LN_EPS = 1e-5
QK_SCALE = HEAD_DIM ** -0.5
NEG_BIG = -1e30


def _cparams(*sem, flags=None):
    return pltpu.CompilerParams(dimension_semantics=tuple(sem), vmem_limit_bytes=VMEM_LIMIT_BYTES, flags=flags)


def _lane_iota(shape):
    return lax.broadcasted_iota(jnp.int32, shape, len(shape) - 1)


def _row_iota(shape):
    return lax.broadcasted_iota(jnp.int32, shape, len(shape) - 2)


def _dot(a, b):
    return jnp.dot(a.astype(BF16), b.astype(BF16), preferred_element_type=F32)


def _dot_nt(a, b):
    return lax.dot_general(a.astype(BF16), b.astype(BF16), (((1,), (1,)), ((), ())), preferred_element_type=F32)


def _dot_tn(a, b):
    return lax.dot_general(a.astype(BF16), b.astype(BF16), (((0,), (0,)), ((), ())), preferred_element_type=F32)


def _ln_rows(x):
    mu = jnp.mean(x, axis=-1, keepdims=True)
    xc = x - mu
    var = jnp.mean(xc * xc, axis=-1, keepdims=True)
    return xc * lax.rsqrt(var + LN_EPS)


def _sigmoid(x):
    return 1.0 / (1.0 + jnp.exp(-x))


def _silu(x):
    return x * _sigmoid(x)


def _head_mask(shape, h):
    lane = _lane_iota(shape)
    return (lane >= h * HEAD_DIM) & (lane < (h + 1) * HEAD_DIM)


def _ada_kernel(c_ref, w_ref, b_ref, o_ref):
    o_ref[0] = _dot(_silu(c_ref[...]), w_ref[0]) + b_ref[0]


def _ada_call(c_all, w_ada, b_ada):
    depth, d, n = w_ada.shape
    m = c_all.shape[0]
    tn = 1536
    return pl.pallas_call(
        _ada_kernel,
        grid=(depth, n // tn),
        in_specs=[pl.BlockSpec((m, d), lambda i, j: (0, 0)),
                  pl.BlockSpec((1, d, tn), lambda i, j: (i, 0, j)),
                  pl.BlockSpec((1, 1, tn), lambda i, j: (i, 0, j))],
        out_specs=pl.BlockSpec((1, m, tn), lambda i, j: (i, 0, j)),
        out_shape=jax.ShapeDtypeStruct((depth, m, n), F32),
        compiler_params=_cparams("parallel", "parallel"),
        name="ada_mod",
    )(c_all, w_ada, b_ada.reshape(depth, 1, n))


def _rope128(x, cos, sin_signed, half):
    first = (_lane_iota(x.shape) % HEAD_DIM) < half
    partner = jnp.where(first, pltpu.roll(x, LANES - half, 1), pltpu.roll(x, half, 1))
    return x * cos + partner * sin_signed


def _mod_rows(m_ref, row, per_token):
    if per_token:
        d = m_ref.shape[-1] // 8
        return m_ref[:, row * d:(row + 1) * d]
    return m_ref[0, row:row + 1, :]


def _in_kernel(per_token, x_ref, m_ref, wr_ref, wu_ref, wsb_ref, wn_ref, tab_ref,
               ret_ref, u_ref, sbq_ref, sbkv_ref, sbkv16_ref, nq_ref, nqr_ref, rows_ref, win_ref, ng_ref,
               rowsc_ref, rowss16_ref, win16_ref, *maybe_rowst16_ref):
    x = x_ref[...]
    h = (_ln_rows(x) * (1.0 + _mod_rows(m_ref, 1, per_token)) + _mod_rows(m_ref, 0, per_token)).astype(BF16)
    tab = tab_ref[...]
    rc, rs = tab[:, 0:128], tab[:, 128:256]
    qc, qs = tab[:, 256:384], tab[:, 384:512]
    kc, ks = tab[:, 512:640], tab[:, 640:768]

    zr = _dot_nt(h, wr_ref[...])
    half_r = HEAD_DIM // 2
    for c in range(2):
        ret_ref[:, c * 128:(c + 1) * 128] = _rope128(zr[:, c * 128:(c + 1) * 128], rc, rs, half_r)
    for c in range(2, 4):
        ret_ref[:, c * 128:(c + 1) * 128] = _rope128(zr[:, c * 128:(c + 1) * 128], rc, rs, half_r) * QK_SCALE
    ret_ref[:, 512:1024] = zr[:, 512:1024]

    u_ref[...] = _dot_nt(h, wu_ref[...])

    zs = _dot_nt(h, wsb_ref[...])
    sbq_ref[...] = (zs[:, 0:256] * QK_SCALE).astype(BF16)
    sbkv_ref[...] = zs[:, 256:768]
    sbkv16_ref[...] = zs[:, 256:768].astype(BF16)

    zn = _dot_nt(h, wn_ref[...])
    half_n = ROPE_DIM // 2
    nq_ref[...] = (zn[:, 0:256] * QK_SCALE).astype(BF16)
    for c in range(2):
        nqr_ref[:, c * 128:(c + 1) * 128] = (
            _rope128(zn[:, c * 128:(c + 1) * 128], qc, qs, half_n) * QK_SCALE).astype(BF16)
    rows_slc = _rope128(zn[:, 384:512], kc, ks, half_n)
    rows_ref[:, 0:128] = zn[:, 256:384]
    rows_ref[:, 128:256] = rows_slc
    rowsc_ref[...] = zn[:, 256:384]
    rowss16_ref[...] = rows_slc.astype(BF16)
    if maybe_rowst16_ref:
        maybe_rowst16_ref[0][...] = rows_slc.T.astype(BF16)
    win = _rope128(zn[:, 512:640], kc, ks, half_n)
    win_ref[...] = win
    win16_ref[...] = win.astype(BF16)
    if maybe_rowst16_ref:
        maybe_rowst16_ref[1][...] = win.T.astype(BF16)
    ng_ref[...] = _sigmoid(zn[:, 640:768])


def _in_call(x2, mod, tabs, w, tiles_per_batch, tm):
    t, d = x2.shape
    per_token = tiles_per_batch is None
    nt = t // tm
    if per_token:
        mod_spec = pl.BlockSpec((tm, 8 * d), lambda i: (i, 0))
        tab_spec = pl.BlockSpec((tm, 768), lambda i: (i, 0))
    else:
        mod_spec = pl.BlockSpec((1, 8, d), lambda i: (i // tiles_per_batch, 0, 0))
        tab_spec = pl.BlockSpec((tm, 768), lambda i: (i % tiles_per_batch, 0))

    def wspec(a):
        return pl.BlockSpec(a.shape, lambda i: (0, 0))

    def ospec(n):
        return pl.BlockSpec((tm, n), lambda i: (i, 0))

    outs = [(1024, F32), (256, F32), (256, BF16), (512, F32), (512, BF16), (256, BF16), (256, BF16),
            (256, F32), (128, F32), (128, F32), (128, F32), (128, BF16), (128, BF16)]
    return pl.pallas_call(
        functools.partial(_in_kernel, per_token),
        grid=(nt,),
        in_specs=[pl.BlockSpec((tm, d), lambda i: (i, 0)), mod_spec,
                  wspec(w["w_ret"]), wspec(w["w_u"]), wspec(w["w_sb"]), wspec(w["w_nsa"]), tab_spec],
        out_specs=[ospec(n) for n, _ in outs] + ([] if per_token else [pl.BlockSpec((LANES, tm), lambda i: (0, i))] * 2),
        out_shape=[jax.ShapeDtypeStruct((t, n), dt) for n, dt in outs]
        + ([] if per_token else [jax.ShapeDtypeStruct((LANES, t), BF16)] * 2),
        compiler_params=_cparams("parallel"),
        name="in_proj",
    )(x2, mod, w["w_ret"], w["w_u"], w["w_sb"], w["w_nsa"], tabs)


def _merge_kernel(per_token, alpha, x_ref, m_ref, a_ref, b_ref, c_ref, d_ref, wmg_ref, wbr_ref, wout_ref,
                  g_ref, bias_ref, o_ref):
    x = x_ref[...]
    dm = x.shape[-1]
    h = (_ln_rows(x) * (1.0 + _mod_rows(m_ref, 1, per_token)) + _mod_rows(m_ref, 0, per_token)).astype(BF16)
    acc = jnp.zeros(x.shape, F32)
    for n, br_ref in enumerate((a_ref, b_ref, c_ref, d_ref)):
        mg = _dot_nt(h, wmg_ref[n * dm:(n + 1) * dm, :])
        br = jnp.dot(br_ref[...].astype(BF16), wbr_ref[n], preferred_element_type=F32)
        acc = acc + _sigmoid(mg) * br
    y = jnp.dot(acc.astype(BF16), wout_ref[...], preferred_element_type=F32)
    r = alpha * x + (1.0 + _mod_rows(m_ref, 2, per_token)) * y
    o_ref[...] = _ln_rows(r) * g_ref[...] + bias_ref[...]


def _merge_call(x2, mod, branches, w, tiles_per_batch, tm, alpha):
    t, d = x2.shape
    per_token = tiles_per_batch is None
    if per_token:
        mod_spec = pl.BlockSpec((tm, 8 * d), lambda i: (i, 0))
    else:
        mod_spec = pl.BlockSpec((1, 8, d), lambda i: (i // tiles_per_batch, 0, 0))
    row = lambda n: pl.BlockSpec((tm, n), lambda i: (i, 0))
    return pl.pallas_call(
        functools.partial(_merge_kernel, per_token, alpha),
        grid=(t // tm,),
        in_specs=[row(d), mod_spec, row(BRANCH_W), row(BRANCH_W), row(BRANCH_W), row(BRANCH_W),
                  pl.BlockSpec(w["w_mg"].shape, lambda i: (0, 0)),
                  pl.BlockSpec(w["w_branch"].shape, lambda i: (0, 0, 0)),
                  pl.BlockSpec(w["w_out"].shape, lambda i: (0, 0)),
                  pl.BlockSpec((1, d), lambda i: (0, 0)), pl.BlockSpec((1, d), lambda i: (0, 0))],
        out_specs=row(d),
        out_shape=jax.ShapeDtypeStruct((t, d), F32),
        compiler_params=_cparams("parallel"),
        name="merge_out",
    )(x2, mod, *branches, w["w_mg"], w["w_branch"], w["w_out"], w["ln1_g"], w["ln1_b"])


def _ffn_kernel(per_token, alpha, n_chunk, x_ref, m_ref, wup_ref, wdn_ref, g_ref, bias_ref, o_ref):
    x = x_ref[...]
    dff = wdn_ref.shape[0]
    ck = dff // n_chunk
    h = (_ln_rows(x) * (1.0 + _mod_rows(m_ref, 4, per_token)) + _mod_rows(m_ref, 3, per_token)).astype(BF16)
    f = jnp.zeros(x.shape, F32)
    for c in range(n_chunk):
        a = jnp.dot(h, wup_ref[:, c * ck:(c + 1) * ck], preferred_element_type=F32)
        u = jnp.dot(h, wup_ref[:, dff + c * ck:dff + (c + 1) * ck], preferred_element_type=F32)
        f = f + jnp.dot((_silu(a) * u).astype(BF16), wdn_ref[c * ck:(c + 1) * ck, :], preferred_element_type=F32)
    r = alpha * x + (1.0 + _mod_rows(m_ref, 5, per_token)) * f
    o_ref[...] = _ln_rows(r) * g_ref[...] + bias_ref[...]


def _ffn_call(x2, mod, w, tiles_per_batch, tm, alpha):
    t, d = x2.shape
    per_token = tiles_per_batch is None
    if per_token:
        mod_spec = pl.BlockSpec((tm, 8 * d), lambda i: (i, 0))
    else:
        mod_spec = pl.BlockSpec((1, 8, d), lambda i: (i // tiles_per_batch, 0, 0))
    dff = w["w_dn"].shape[0]
    n_chunk = 2 if dff % 256 == 0 else 1
    return pl.pallas_call(
        functools.partial(_ffn_kernel, per_token, alpha, n_chunk),
        grid=(t // tm,),
        in_specs=[pl.BlockSpec((tm, d), lambda i: (i, 0)), mod_spec,
                  pl.BlockSpec(w["w_up"].shape, lambda i: (0, 0)),
                  pl.BlockSpec(w["w_dn"].shape, lambda i: (0, 0)),
                  pl.BlockSpec((1, d), lambda i: (0, 0)), pl.BlockSpec((1, d), lambda i: (0, 0))],
        out_specs=pl.BlockSpec((tm, d), lambda i: (i, 0)),
        out_shape=jax.ShapeDtypeStruct((t, d), F32),
        compiler_params=_cparams("parallel"),
        name="ffn",
    )(x2, mod, w["w_up"], w["w_dn"], w["ln2_g"], w["ln2_b"])


def _ret_kernel(chunk, q_ref, k_ref, v_ref, g_ref, r0_ref, intra_ref, qdec_ref, kdec_ref, cdec_ref,
                o_ref, rout_ref, r_s):
    j = pl.program_id(1)

    @pl.when(j == 0)
    def _():
        r_s[...] = r0_ref[0]

    for c in range(q_ref.shape[0] // chunk):
        rows = slice(c * chunk, (c + 1) * chunk)
        q = q_ref[rows, :]
        k = k_ref[rows, :]
        v16 = v_ref[rows, :].astype(BF16)
        k16 = k.astype(BF16)
        o = _dot(q * qdec_ref[...], r_s[...])
        for h in range(N_HEADS):
            hm = _head_mask(q.shape, h)
            s = _dot_nt(jnp.where(hm, q, 0.0), k16) * intra_ref[h]
            o = o + jnp.where(hm, _dot(s, v16), 0.0)
        upd = _dot_tn(k * kdec_ref[...], v16)
        rr = _row_iota(upd.shape) // HEAD_DIM
        cc = _lane_iota(upd.shape) // HEAD_DIM
        r_s[...] = r_s[...] * cdec_ref[...] + jnp.where(rr == cc, upd, 0.0)

        mu = jnp.zeros(o.shape, F32)
        for h in range(N_HEADS):
            hm = _head_mask(o.shape, h)
            mu = mu + jnp.where(hm, jnp.sum(jnp.where(hm, o, 0.0), axis=-1, keepdims=True), 0.0)
        oc = o - mu * (1.0 / HEAD_DIM)
        var = jnp.zeros(o.shape, F32)
        oc2 = oc * oc
        for h in range(N_HEADS):
            hm = _head_mask(o.shape, h)
            var = var + jnp.where(hm, jnp.sum(jnp.where(hm, oc2, 0.0), axis=-1, keepdims=True), 0.0)
        o_ref[rows, :] = oc * lax.rsqrt(var * (1.0 / HEAD_DIM) + LN_EPS) * _silu(g_ref[rows, :])

    @pl.when(j == pl.num_programs(1) - 1)
    def _():
        rout_ref[0] = r_s[...]


def _ret_tables(chunk):
    h = jnp.arange(N_HEADS, dtype=F32)
    log_g = jnp.log1p(-jnp.exp2(-5.0 - h))
    idx = jnp.arange(chunk, dtype=F32)
    diff = idx[:, None] - idx[None, :]
    intra = jnp.where(diff >= 0, jnp.exp(jnp.maximum(diff, 0.0)[None] * log_g[:, None, None]), 0.0)
    q_dec = jnp.exp((idx + 1.0)[None, :] * log_g[:, None])
    k_dec = jnp.exp((chunk - 1.0 - idx)[None, :] * log_g[:, None])
    c_dec = jnp.exp(chunk * log_g)
    lanes = lambda t: jnp.repeat(t.T, HEAD_DIM, axis=1)
    return intra, lanes(q_dec), lanes(k_dec), jnp.repeat(c_dec, HEAD_DIM)[None, :]


def _ret_call(ret, b, l, r0_bd):
    chunk = RET_CHUNK if l % RET_CHUNK == 0 else l
    per_step = 4 if (l // chunk) % 4 == 0 else 1
    n = l // (chunk * per_step)
    intra, q_dec, k_dec, c_dec = _ret_tables(chunk)
    col = lambda c: pl.BlockSpec((chunk * per_step, BRANCH_W), lambda i, j: (i * n + j, c))
    full = lambda a: pl.BlockSpec(a.shape, lambda i, j: (0,) * a.ndim)
    return pl.pallas_call(
        functools.partial(_ret_kernel, chunk),
        grid=(b, n),
        in_specs=[col(0), col(1), col(2), col(3),
                  pl.BlockSpec((1, BRANCH_W, BRANCH_W), lambda i, j: (i, 0, 0)),
                  full(intra), full(q_dec), full(k_dec), full(c_dec)],
        out_specs=[pl.BlockSpec((chunk * per_step, BRANCH_W), lambda i, j: (i * n + j, 0)),
                   pl.BlockSpec((1, BRANCH_W, BRANCH_W), lambda i, j: (i, 0, 0))],
        out_shape=[jax.ShapeDtypeStruct((b * l, BRANCH_W), F32),
                   jax.ShapeDtypeStruct((b, BRANCH_W, BRANCH_W), F32)],
        scratch_shapes=[pltpu.VMEM((BRANCH_W, BRANCH_W), F32)],
        compiler_params=_cparams("parallel", "arbitrary"),
        name="retention",
    )(ret, ret, ret, ret, r0_bd, intra, q_dec, k_dec, c_dec)


def _bd_from_heads(r):
    b = r.shape[0]
    eye = jnp.eye(N_HEADS, dtype=r.dtype)
    return jnp.einsum("bhij,hg->bhigj", r, eye).reshape(b, BRANCH_W, BRANCH_W)


def _heads_from_bd(r_bd):
    b = r_bd.shape[0]
    r5 = r_bd.reshape(b, N_HEADS, HEAD_DIM, N_HEADS, HEAD_DIM)
    return jnp.stack([r5[:, h, :, h, :] for h in range(N_HEADS)], axis=1)


def _cmul(ar, ai, br, bi):
    return ar * br - ai * bi, ar * bi + ai * br


def _gelu_tanh(x):
    return 0.5 * x * (1.0 + jnp.tanh(math.sqrt(2.0 / math.pi) * (x + 0.044715 * (x * x * x))))


def _s5_kernel(chain, u_ref, x0_ref, are_ref, aim_ref, ldt_ref, bre_ref, bim_ref, cre_ref, cim_ref, dsk_ref,
               wglu_ref, bglu_ref, o_ref, st_ref, xr_s, xi_s, cr_s, ci_s):
    rows = u_ref.shape[0]
    n_grp = rows // SUBLANES
    j = pl.program_id(1)

    ar, ai = are_ref[...], aim_ref[...]
    dt = jnp.exp(ldt_ref[...])
    mag = jnp.exp(ar * dt)
    abr, abi = mag * jnp.cos(ai * dt), mag * jnp.sin(ai * dt)
    nr, ni = abr - 1.0, abi
    den = ar * ar + ai * ai
    fr = (nr * ar + ni * ai) / den
    fi = (ni * ar - nr * ai) / den
    bbr = fr * bre_ref[...] - fi * bim_ref[...]
    bbi = fr * bim_ref[...] + fi * bre_ref[...]

    u = u_ref[...]
    u16 = u.astype(BF16)
    xr_s[...] = jnp.dot(u16, bbr.astype(BF16), preferred_element_type=F32)
    xi_s[...] = jnp.dot(u16, bbi.astype(BF16), preferred_element_type=F32)

    a2r, a2i = _cmul(abr, abi, abr, abi)
    a4r, a4i = _cmul(a2r, a2i, a2r, a2i)
    row = _row_iota((SUBLANES, abr.shape[-1]))
    pr, pi_ = jnp.broadcast_to(abr, row.shape), jnp.broadcast_to(abi, row.shape)
    qr, qi = abr, abi
    for i in range(1, SUBLANES):
        qr, qi = _cmul(qr, qi, abr, abi)
        pr = jnp.where(row == i, qr, pr)
        pi_ = jnp.where(row == i, qi, pi_)
    steps = ((1, abr, abi), (2, a2r, a2i), (4, a4r, a4i))

    if chain:
        @pl.when(j == 0)
        def _():
            cr_s[...] = x0_ref[0, 0:1, :]
            ci_s[...] = x0_ref[0, 1:2, :]

    def group(g, carry):
        r0 = pl.multiple_of(g * SUBLANES, SUBLANES)
        xr = xr_s[pl.ds(r0, SUBLANES), :]
        xi = xi_s[pl.ds(r0, SUBLANES), :]
        for d, er, ei in steps:
            sr = jnp.where(row >= d, pltpu.roll(xr, d, 0), 0.0)
            si = jnp.where(row >= d, pltpu.roll(xi, d, 0), 0.0)
            tr, ti = _cmul(er, ei, sr, si)
            xr, xi = xr + tr, xi + ti
        if chain:
            c_r, c_i = cr_s[...], ci_s[...]
        else:
            c_r, c_i = x0_ref[g, 0:1, :], x0_ref[g, 1:2, :]
        tr, ti = _cmul(pr, pi_, c_r, c_i)
        xr, xi = xr + tr, xi + ti
        xr_s[pl.ds(r0, SUBLANES), :] = xr
        xi_s[pl.ds(r0, SUBLANES), :] = xi
        if chain:
            cr_s[...] = xr[SUBLANES - 1:SUBLANES, :]
            ci_s[...] = xi[SUBLANES - 1:SUBLANES, :]
        else:
            st_ref[g, 0:1, :] = xr[SUBLANES - 1:SUBLANES, :]
            st_ref[g, 1:2, :] = xi[SUBLANES - 1:SUBLANES, :]
        return carry

    lax.fori_loop(0, n_grp, group, 0)

    if chain:
        @pl.when(j == pl.num_programs(1) - 1)
        def _():
            st_ref[0, 0:1, :] = cr_s[...]
            st_ref[0, 1:2, :] = ci_s[...]

    y = (jnp.dot(xr_s[...].astype(BF16), cre_ref[...], preferred_element_type=F32)
         - jnp.dot(xi_s[...].astype(BF16), cim_ref[...], preferred_element_type=F32) + dsk_ref[...] * u)
    zb = _gelu_tanh(y)
    o_ref[...] = zb * _sigmoid(jnp.dot(zb.astype(BF16), wglu_ref[...], preferred_element_type=F32) + bglu_ref[...])


def _s5_call(u2, b, l, x0, w):
    n_state = x0.shape[-1]
    chain = l % 128 == 0
    if chain:
        tl = min(l, 512)
        grid = (b, l // tl)
        rows = tl
        u_spec = pl.BlockSpec((tl, BRANCH_W), lambda i, j: (i * (l // tl) + j, 0))
        x0_spec = pl.BlockSpec((1, 2, n_state), lambda i, j: (i, 0, 0))
    else:
        assert l == SUBLANES
        grid = (1, 1)
        rows = b * l
        u_spec = pl.BlockSpec((rows, BRANCH_W), lambda i, j: (0, 0))
        x0_spec = pl.BlockSpec((b, 2, n_state), lambda i, j: (0, 0, 0))
    full = lambda a: pl.BlockSpec(a.shape, lambda i, j: (0,) * a.ndim)
    params = [w["a_re"], w["a_im"], w["log_dt"], w["b_re_bd"], w["b_im_bd"], w["c_re_bd"], w["c_im_bd"],
              w["d_skip"], w["w_glu"], w["b_glu"]]
    return pl.pallas_call(
        functools.partial(_s5_kernel, chain),
        grid=grid,
        in_specs=[u_spec, x0_spec] + [full(a) for a in params],
        out_specs=[u_spec, x0_spec],
        out_shape=[jax.ShapeDtypeStruct((b * l, BRANCH_W), F32), jax.ShapeDtypeStruct(x0.shape, F32)],
        scratch_shapes=[pltpu.VMEM((rows, n_state), F32), pltpu.VMEM((rows, n_state), F32),
                        pltpu.VMEM((1, n_state), F32), pltpu.VMEM((1, n_state), F32)],
        compiler_params=_cparams("parallel", "arbitrary"),
        name="s5",
    )(u2, x0, *params)


def _s5_params(a_re, a_im, b_re, b_im, c_re, c_im, d_skip, log_dt, w_glu, b_glu):
    g, p = a_re.shape
    cg = b_re.shape[-1]
    eye = jnp.eye(g, dtype=F32)
    b_bd = lambda t: jnp.einsum("gpc,gh->gchp", t, eye).reshape(g * cg, g * p)
    c_bd = lambda t: jnp.einsum("gcp,gh->gphc", t, eye).reshape(g * p, g * cg)
    return {
        "a_re": a_re.reshape(1, g * p), "a_im": a_im.reshape(1, g * p),
        "log_dt": jnp.repeat(log_dt, p).reshape(1, g * p),
        "b_re_bd": b_bd(b_re), "b_im_bd": b_bd(b_im),
        "c_re_bd": c_bd(c_re).astype(BF16), "c_im_bd": c_bd(c_im).astype(BF16),
        "d_skip": d_skip.reshape(1, g * cg), "w_glu": w_glu.astype(BF16), "b_glu": b_glu.reshape(1, -1),
    }


TK = 128


def _stack_heads_bd(q):
    qf = q.astype(F32)
    return jnp.concatenate([jnp.where(_head_mask(qf.shape, h), qf, 0.0) for h in range(N_HEADS)], axis=0).astype(BF16)


def _unstack_heads_bd(acc, tq):
    out = jnp.zeros((tq, acc.shape[-1]), F32)
    for h in range(N_HEADS):
        blk = acc[h * tq:(h + 1) * tq]
        out = out + jnp.where(_head_mask(blk.shape, h), blk, 0.0)
    return out


def _suffix_matrix(tk):
    r = _row_iota((2 * tk, tk)) % tk
    c = _lane_iota((2 * tk, tk))
    return jnp.where(r > c, 1.0, 0.0).astype(BF16)


def _sb_scores(qbd, k16, transposed):
    return jnp.dot(qbd, k16, preferred_element_type=F32) if transposed else _dot_nt(qbd, k16)


def _sb_tile(qbd, k16, v16, mask, u2, carry, acc, transposed):
    return _sb_tile_z(_sb_scores(qbd, k16, transposed), v16, mask, u2, carry, acc, transposed)


def _sb_logw(z, mask, u2):
    log_beta = jnp.minimum(z, 0.0) - jnp.log(1.0 + jnp.exp(-jnp.abs(z)))
    l1m = log_beta - z
    if mask is not None:
        l1m = jnp.where(mask, l1m, 0.0)
    hi = lax.bitcast_convert_type(lax.bitcast_convert_type(l1m, jnp.uint32) & jnp.uint32(0xFFFF0000), F32)
    lo = l1m - hi
    suffix = jnp.dot(jnp.concatenate([hi.astype(BF16), lo.astype(BF16)], axis=1), u2, preferred_element_type=F32)
    return log_beta + suffix, suffix[:, 0:1] + l1m[:, 0:1]


def _sb_accumulate(logw, total, v16, mask, carry, acc, transposed):
    w = jnp.exp(logw + carry)
    if mask is not None:
        w = jnp.where(mask, w, 0.0)
    w16 = w.astype(BF16)
    acc = acc + (_dot_nt(w16, v16) if transposed else jnp.dot(w16, v16, preferred_element_type=F32))
    return carry + total, acc


def _sb_tile_z(z, v16, mask, u2, carry, acc, transposed):
    logw, total = _sb_logw(z, mask, u2)
    return _sb_accumulate(logw, total, v16, mask, carry, acc, transposed)


TKP = 256
SB_DEAD_LOG = -104.0


def _sb_prompt_kernel(q_ref, kv_ref, o_ref):
    j = pl.program_id(1)
    tq = q_ref.shape[0]
    qbd = _stack_heads_bd(q_ref[...])
    u2 = _suffix_matrix(TKP)
    rows = N_HEADS * tq
    n_full = (j * tq) // TKP
    qpos = j * tq + _row_iota((rows, TKP)) % tq
    diag_mask = (n_full * TKP + _lane_iota((rows, TKP))) < qpos

    def keys(kt):
        return kv_ref[pl.ds(pl.multiple_of(kt * TKP, TKP), TKP), 0:BRANCH_W]

    def values(kt):
        return kv_ref[pl.ds(pl.multiple_of(kt * TKP, TKP), TKP), BRANCH_W:2 * BRANCH_W]

    carry, acc = _sb_tile(qbd, keys(n_full), values(n_full), diag_mask, u2,
                          jnp.zeros((rows, 1), F32), jnp.zeros((rows, BRANCH_W), F32), False)

    def cond(st):
        it, carry, _ = st
        return (it < n_full) & (jnp.max(carry) >= SB_DEAD_LOG)

    def body(st):
        it, carry, acc = st
        kt = n_full - 1 - it
        carry, acc = _sb_tile(qbd, keys(kt), values(kt), None, u2, carry, acc, False)
        return it + 1, carry, acc

    _, carry, acc = lax.while_loop(cond, body, (jnp.int32(0), carry, acc))
    o_ref[...] = _unstack_heads_bd(acc, tq)


def _sb_prompt_call(sbq, sbkv16, b, l):
    tq = TK
    n = l // tq
    assert l % TKP == 0
    return pl.pallas_call(
        _sb_prompt_kernel,
        grid=(b, n),
        in_specs=[pl.BlockSpec((tq, BRANCH_W), lambda i, j: (i * n + j, 0)),
                  pl.BlockSpec((l, 2 * BRANCH_W), lambda i, j: (i, 0))],
        out_specs=pl.BlockSpec((tq, BRANCH_W), lambda i, j: (i * n + j, 0)),
        out_shape=jax.ShapeDtypeStruct((b * l, BRANCH_W), F32),
        compiler_params=_cparams("parallel", "arbitrary"),
        name="sb_prompt",
    )(sbq, sbkv16)


PAGES_PER_STEP = 8


def _sb_decode_kernel(n_pg, pt_ref, q_ref, kvn_ref, *rest):
    page_refs = rest[:n_pg]
    o_ref, carry_s, acc_s = rest[n_pg:]
    s = pl.program_id(1)
    tq = q_ref.shape[1]
    rows = N_HEADS * tq
    qbd = _stack_heads_bd(q_ref[0])
    u2 = _suffix_matrix(TK)

    @pl.when(s == 0)
    def _():
        mask = _lane_iota((rows, TK)) < (_row_iota((rows, TK)) % tq)
        kvn = kvn_ref[0]
        carry, acc = _sb_tile(qbd, kvn[0:BRANCH_W, :], kvn[BRANCH_W:, :], mask, u2,
                              jnp.zeros((rows, 1), F32), jnp.zeros((rows, BRANCH_W), F32), True)
        carry_s[...] = carry
        acc_s[...] = acc

    @pl.when(jnp.max(carry_s[...]) >= SB_DEAD_LOG)
    def _():
        carry, acc = carry_s[...], acc_s[...]
        group = 2 if n_pg % 2 == 0 else 1
        u2g = _suffix_matrix(group * TK) if group > 1 else u2
        for k in range(0, n_pg, group):
            kv = jnp.concatenate([page_refs[k + g][0, 0] for g in reversed(range(group))], axis=1).astype(BF16)
            carry, acc = _sb_tile(qbd, kv[0:BRANCH_W, :], kv[BRANCH_W:, :], None, u2g, carry, acc, True)
        carry_s[...] = carry
        acc_s[...] = acc

    @pl.when(s == pl.num_programs(1) - 1)
    def _():
        o_ref[0] = _unstack_heads_bd(acc_s[...], tq)


def _sb_decode_call(sbq3, kvn_t16, cache_t, layer, page_table):
    b, tq, _ = sbq3.shape
    n_pages = page_table.shape[1]
    n_pg = math.gcd(PAGES_PER_STEP, n_pages)
    n_steps = n_pages // n_pg

    def page_spec(k):
        return pl.BlockSpec((1, 1, 2 * BRANCH_W, TK),
                            lambda i, s, pt: (layer, pt[i, n_pages - 1 - (s * n_pg + k)], 0, 0))

    grid_spec = pltpu.PrefetchScalarGridSpec(
        num_scalar_prefetch=1,
        grid=(b, n_steps),
        in_specs=[pl.BlockSpec((1, tq, BRANCH_W), lambda i, s, pt: (i, 0, 0)),
                  pl.BlockSpec((1, 2 * BRANCH_W, TK), lambda i, s, pt: (i, 0, 0))]
                 + [page_spec(k) for k in range(n_pg)],
        out_specs=pl.BlockSpec((1, tq, BRANCH_W), lambda i, s, pt: (i, 0, 0)),
        scratch_shapes=[pltpu.VMEM((N_HEADS * tq, 1), F32), pltpu.VMEM((N_HEADS * tq, BRANCH_W), F32)],
    )
    return pl.pallas_call(
        functools.partial(_sb_decode_kernel, n_pg),
        grid_spec=grid_spec,
        out_shape=jax.ShapeDtypeStruct((b, tq, BRANCH_W), F32),
        compiler_params=_cparams("parallel", "arbitrary"),
        name="sb_decode",
    )(page_table, sbq3, kvn_t16, *([cache_t] * n_pg))


def _round_up(x, m):
    return -(-x // m) * m


def _stack_heads_128(q):
    qf = q.astype(F32)
    tq = qf.shape[0]
    low = _lane_iota((tq, LANES)) < HEAD_DIM
    parts = []
    for h in range(N_HEADS):
        half = qf[:, LANES * (h // 2):LANES * (h // 2 + 1)]
        if h % 2 == 1:
            half = pltpu.roll(half, HEAD_DIM, 1)
        parts.append(jnp.where(low, half, 0.0))
    return jnp.concatenate(parts, axis=0)


def _compress(rows_ref, wc_ref, n):
    p0 = jnp.zeros((n, LANES), F32)
    p1 = jnp.zeros((n, LANES), F32)
    for s in range(CMP_STRIDE):
        x = rows_ref[pl.ds(s, n, stride=CMP_STRIDE), :].astype(BF16)
        p0 = p0 + jnp.dot(x, wc_ref[s], preferred_element_type=F32)
        p1 = p1 + jnp.dot(x, wc_ref[CMP_STRIDE + s], preferred_element_type=F32)
    return p0 + pltpu.roll(p1, n - 1, 0)


def _compress_chunks(xs_ref, wcc_ref):
    n = xs_ref.shape[1]
    p = jnp.zeros((n, 2 * LANES), F32)
    for i in range(CMP_STRIDE // 2):
        p = p + jnp.dot(xs_ref[i].astype(BF16), wcc_ref[i], preferred_element_type=F32)
    return p[:, 0:LANES] + pltpu.roll(p[:, LANES:2 * LANES], n - 1, 0)


def _cmp_pairs(wc):
    first, second = wc[:CMP_STRIDE], wc[CMP_STRIDE:]
    both = jnp.concatenate([first, second], axis=2)
    return both.reshape(CMP_STRIDE // 2, 2 * LANES, 2 * LANES)


def _cmp_select(qp16, kvc, qpos_rows, qpos_q, tq, nsp, n_top):
    n = kvc.shape[0]
    kvc16 = kvc.astype(BF16)
    s = _dot_nt(qp16, kvc16)
    cmask = (_lane_iota(s.shape) * CMP_STRIDE + (CMP_BLOCK - 1)) <= qpos_rows
    s = jnp.where(cmask, s, NEG_BIG)
    m = jnp.max(s, axis=-1, keepdims=True)
    m = jnp.where(m > 0.5 * NEG_BIG, m, 0.0)
    e = jnp.where(cmask, jnp.exp(s - m), 0.0)
    p16 = (e / jnp.maximum(jnp.sum(e, axis=-1, keepdims=True), 1e-30)).astype(BF16)
    o_cmp = jnp.dot(p16, kvc16, preferred_element_type=F32)
    ci = _row_iota((n, nsp)) * CMP_STRIDE
    sj = _lane_iota((n, nsp)) * SLC_BLOCK
    c2s = jnp.where((ci < sj + SLC_BLOCK) & (sj < ci + CMP_BLOCK), 1.0, 0.0).astype(BF16)
    imp4 = jnp.dot(p16, c2s, preferred_element_type=F32)
    imp = imp4[0:tq]
    for h in range(1, N_HEADS):
        imp = imp + imp4[h * tq:(h + 1) * tq]

    blk = _lane_iota((tq, nsp))
    blkf = blk.astype(F32)
    cur = qpos_q // SLC_BLOCK
    forced = (blk == 0) | (blk == cur) | (blk == cur - 1)
    val = jnp.where(forced, jnp.inf, imp)
    avail = jnp.where(blk * SLC_BLOCK <= qpos_q, 1.0, 0.0)
    sel = jnp.zeros((tq, nsp), F32)
    for _ in range(n_top):
        vm = jnp.where(avail > 0.5, val, -jnp.inf)
        mx = jnp.max(vm, axis=-1, keepdims=True)
        cand = (avail > 0.5) & (vm == mx)
        first = jnp.min(jnp.where(cand, blkf, float(nsp)), axis=-1, keepdims=True)
        pick = blkf == first
        sel = jnp.where(pick, 1.0, sel)
        avail = jnp.where(pick, 0.0, avail)
    return o_cmp, sel.astype(BF16)


def _cmp_select_t(qp_t16, kvc, kvc_t, c2s_t16, j, tq, n_top):
    n = kvc.shape[0]
    nsp = c2s_t16.shape[0]
    rows4 = N_HEADS * tq
    qpos_cols = j * tq + _lane_iota((n, rows4)) % tq
    s = jnp.dot(kvc.astype(BF16), qp_t16, preferred_element_type=F32)
    cmask = (_row_iota((n, rows4)) * CMP_STRIDE + (CMP_BLOCK - 1)) <= qpos_cols
    s = jnp.where(cmask, s, NEG_BIG)
    m = jnp.max(s, axis=0, keepdims=True)
    m = jnp.where(m > 0.5 * NEG_BIG, m, 0.0)
    e = jnp.where(cmask, jnp.exp(s - m), 0.0)
    p16 = (e / jnp.maximum(jnp.sum(e, axis=0, keepdims=True), 1e-30)).astype(BF16)
    o_cmp_t = jnp.dot(kvc_t.astype(BF16), p16, preferred_element_type=F32)
    imp4 = jnp.dot(c2s_t16, p16, preferred_element_type=F32)
    imp = imp4[:, 0:tq]
    for h in range(1, N_HEADS):
        imp = imp + imp4[:, h * tq:(h + 1) * tq]

    blk = _row_iota((nsp, tq))
    blkf = blk.astype(F32)
    qpos = j * tq + _lane_iota((nsp, tq))
    cur = qpos // SLC_BLOCK
    forced = (blk == 0) | (blk == cur) | (blk == cur - 1)
    val = jnp.where(forced, jnp.inf, imp)
    avail = jnp.where(blk * SLC_BLOCK <= qpos, 1.0, 0.0)
    sel = jnp.zeros((nsp, tq), F32)
    for _ in range(n_top):
        vm = jnp.where(avail > 0.5, val, -jnp.inf)
        mx = jnp.max(vm, axis=0, keepdims=True)
        cand = (avail > 0.5) & (vm == mx)
        first = jnp.min(jnp.where(cand, blkf, float(nsp)), axis=0, keepdims=True)
        pick = blkf == first
        sel = jnp.where(pick, 1.0, sel)
        avail = jnp.where(pick, 0.0, avail)
    return o_cmp_t, sel


def _attn_tile(q16, kv16, mask, m, l, acc, transposed=False):
    s = jnp.dot(q16, kv16, preferred_element_type=F32) if transposed else _dot_nt(q16, kv16)
    s = jnp.where(mask, s, NEG_BIG)
    m_new = jnp.maximum(m, jnp.max(s, axis=-1, keepdims=True))
    p = jnp.where(mask, jnp.exp(s - m_new), 0.0)
    alpha = jnp.exp(m - m_new)
    l = alpha * l + jnp.sum(p, axis=-1, keepdims=True)
    p16 = p.astype(BF16)
    acc = alpha * acc + (_dot_nt(p16, kv16) if transposed else jnp.dot(p16, kv16, preferred_element_type=F32))
    return m_new, l, acc


def _attn_update(s, kv16, m, l, acc):
    m_new = jnp.maximum(m, jnp.max(s, axis=-1, keepdims=True))
    p = jnp.exp(s - m_new)
    alpha = jnp.exp(m - m_new)
    l = alpha * l + jnp.sum(p, axis=-1, keepdims=True)
    acc = alpha * acc + jnp.dot(p.astype(BF16), kv16, preferred_element_type=F32)
    return m_new, l, acc


def _attn_init(rows, width):
    return jnp.full((rows, 1), NEG_BIG, F32), jnp.zeros((rows, 1), F32), jnp.zeros((rows, width), F32)


def _slc_tile(kt, carry, qr16, kv16, selb, qpos_rows, nsp, tk, transposed):
    er = _row_iota((nsp, tk))
    ec = _lane_iota((nsp, tk))
    expand = jnp.where(er == (tk // SLC_BLOCK) * kt + ec // SLC_BLOCK, 1.0, 0.0).astype(BF16)
    seltok = jnp.dot(selb, expand, preferred_element_type=F32)
    seltok4 = jnp.concatenate([seltok] * N_HEADS, axis=0)
    tok = kt * tk + _lane_iota(seltok4.shape)
    mask = (seltok4 > 0.5) & (tok <= qpos_rows)
    return _attn_tile(qr16, kv16, mask, *carry, transposed=transposed)


def _win_tile(kv16, kpos, carry, qw16, qpos_rows):
    dist = qpos_rows - kpos
    mask = (dist >= 0) & (dist < WINDOW)
    return _attn_tile(qw16, kv16, mask, *carry)


def _nsa_combine(g, o_c, o_s, o_w, tq):
    comb = []
    for h in range(N_HEADS):
        sl = slice(h * tq, (h + 1) * tq)
        comb.append(g[:, h:h + 1] * o_c[sl] + g[:, N_HEADS + h:N_HEADS + h + 1] * o_s[sl]
                    + g[:, 2 * N_HEADS + h:2 * N_HEADS + h + 1] * o_w[sl])
    low = _lane_iota((tq, LANES)) < HEAD_DIM
    out01 = jnp.where(low, pltpu.roll(comb[0], HEAD_DIM, 1), comb[1])
    out23 = jnp.where(low, pltpu.roll(comb[2], HEAD_DIM, 1), comb[3])
    return jnp.concatenate([out01, out23], axis=1)


def _nsa_prompt_kernel(n_top, nq_ref, nqr_ref, ng_ref, rowsc_ref, rows_ref, rowst_ref, win_ref, wint_ref, onehot_ref,
                       c2st_ref, wc_ref, o_ref, kvc_s, kvct_s):
    j = pl.program_id(1)
    tq = nq_ref.shape[0]

    @pl.when(j == 0)
    def _():
        kvc = _compress(rowsc_ref, wc_ref, kvc_s.shape[0])
        kvc_s[...] = kvc
        kvct_s[...] = kvc.T

    rows4 = N_HEADS * tq
    qp_t16 = _stack_heads_128(nq_ref[...]).T.astype(BF16)
    qw_t = _stack_heads_128(nqr_ref[...]).T
    qw_t16 = qw_t.astype(BF16)

    o_c_t, sel_t = _cmp_select_t(qp_t16, kvc_s[...], kvct_s[...], c2st_ref[...], j, tq, n_top)
    last = (j * tq) // TKP

    sel_bias_t = (sel_t - 1.0) * (-NEG_BIG)
    q_aug_t = jnp.concatenate([qw_t, jnp.concatenate([sel_bias_t] * N_HEADS, axis=1)], axis=0).astype(BF16)

    def slc_scores(kt):
        r0 = pl.multiple_of(kt * TKP, TKP)
        k_aug = jnp.concatenate([rows_ref[pl.ds(r0, TKP), :], onehot_ref[pl.ds(r0, TKP), :]], axis=1)
        return jnp.dot(k_aug, q_aug_t, preferred_element_type=F32)

    def slc_update(s, kt, m, l, acc):
        m_new = jnp.maximum(m, jnp.max(s, axis=0, keepdims=True))
        p = jnp.exp(s - m_new)
        alpha = jnp.exp(m - m_new)
        l = alpha * l + jnp.sum(p, axis=0, keepdims=True)
        vt = rowst_ref[:, pl.ds(pl.multiple_of(kt * TKP, TKP), TKP)]
        acc = alpha * acc + jnp.dot(vt, p.astype(BF16), preferred_element_type=F32)
        return m_new, l, acc

    def slc_body(kt, st):
        m, l, acc, s = st
        s_next = slc_scores(jnp.minimum(kt + 1, last))
        m, l, acc = slc_update(s, kt, m, l, acc)
        return m, l, acc, s_next

    def slc_pair(i, st):
        m, l, acc, s = st
        kt = 2 * i
        s_b = slc_scores(jnp.minimum(kt + 1, last))
        m, l, acc = slc_update(s, kt, m, l, acc)
        s_next = slc_scores(jnp.minimum(kt + 2, last))
        m, l, acc = slc_update(s_b, kt + 1, m, l, acc)
        return m, l, acc, s_next

    init = (jnp.full((1, rows4), NEG_BIG, F32), jnp.zeros((1, rows4), F32), jnp.zeros((LANES, rows4), F32))
    n_pair = last // 2
    st_s = lax.fori_loop(0, n_pair, slc_pair, (*init, slc_scores(0)))
    m_s, l_s, acc_s, s_diag = lax.fori_loop(2 * n_pair, last, slc_body, st_s)
    tok = last * TKP + _row_iota((TKP, rows4))
    qpos_cols = j * tq + _lane_iota((TKP, rows4)) % tq
    s_diag = jnp.where(tok <= qpos_cols, s_diag, NEG_BIG)
    _, l_s, acc_s = slc_update(s_diag, last, m_s, l_s, acc_s)
    o_s = (acc_s / l_s).T

    def win_tile(kt, mask, st):
        m, l, acc = st
        r0 = pl.multiple_of(kt * TKP, TKP)
        s = jnp.dot(win_ref[pl.ds(r0, TKP), :], qw_t16, preferred_element_type=F32)
        s = jnp.where(mask, s, NEG_BIG)
        m_new = jnp.maximum(m, jnp.max(s, axis=0, keepdims=True))
        p = jnp.exp(s - m_new)
        alpha = jnp.exp(m - m_new)
        l = alpha * l + jnp.sum(p, axis=0, keepdims=True)
        acc = alpha * acc + jnp.dot(wint_ref[:, pl.ds(r0, TKP)], p.astype(BF16), preferred_element_type=F32)
        return m_new, l, acc

    krow = _row_iota((TKP, rows4))
    st_w = win_tile(last, last * TKP + krow <= qpos_cols, init)
    kb = jnp.maximum(last - 1, 0)
    st_w = win_tile(kb, kb * TKP + krow < last * TKP, st_w)
    ka = jnp.maximum(last - 2, 0)
    kpos_a = ka * TKP + krow
    _, l_w, acc_w = win_tile(ka, (qpos_cols - kpos_a < WINDOW) & (kpos_a < (last - 1) * TKP), st_w)
    o_ref[...] = _nsa_combine(ng_ref[...], o_c_t.T, o_s, (acc_w / l_w).T, tq)


def _nsa_prompt_call(nq, nqr, ng, rowsc, rowss16, rowst16, win16, wint16, wc, b, l):
    tq = TK
    n = l // tq
    n_top = min(N_SEL, l // SLC_BLOCK)
    nsp = _round_up(l // SLC_BLOCK, LANES)
    onehot = (jnp.arange(l)[:, None] // SLC_BLOCK == jnp.arange(nsp)[None, :]).astype(BF16)
    cs = jnp.arange(l // CMP_STRIDE)[None, :] * CMP_STRIDE
    ss = jnp.arange(nsp)[:, None] * SLC_BLOCK
    c2s_t = ((cs < ss + SLC_BLOCK) & (ss < cs + CMP_BLOCK)).astype(BF16)
    qspec = lambda w: pl.BlockSpec((tq, w), lambda i, j: (i * n + j, 0))
    return pl.pallas_call(
        functools.partial(_nsa_prompt_kernel, n_top),
        grid=(b, n),
        in_specs=[qspec(BRANCH_W), qspec(BRANCH_W), qspec(LANES),
                  pl.BlockSpec((l, LANES), lambda i, j: (i, 0)),
                  pl.BlockSpec((l, LANES), lambda i, j: (i, 0)),
                  pl.BlockSpec((LANES, l), lambda i, j: (0, i)),
                  pl.BlockSpec((l, LANES), lambda i, j: (i, 0)),
                  pl.BlockSpec((LANES, l), lambda i, j: (0, i)),
                  pl.BlockSpec(onehot.shape, lambda i, j: (0, 0)),
                  pl.BlockSpec(c2s_t.shape, lambda i, j: (0, 0)),
                  pl.BlockSpec(wc.shape, lambda i, j: (0, 0, 0))],
        out_specs=qspec(BRANCH_W),
        out_shape=jax.ShapeDtypeStruct((b * l, BRANCH_W), F32),
        scratch_shapes=[pltpu.VMEM((l // CMP_STRIDE, LANES), F32), pltpu.VMEM((LANES, l // CMP_STRIDE), F32)],
        compiler_params=_cparams("parallel", "arbitrary"),
        name="nsa_prompt",
    )(nq, nqr, ng, rowsc, rowss16, rowst16, win16, wint16, onehot, c2s_t, wc)


def _nsa_decode_kernel(n_pg, past_len, n_cmp, n_top, pt_ref, nq_ref, nqr_ref, ng_ref, rown_ref, rownt_ref, cwin_ref,
                       winn_ref, onehot_ref, wcc_ref, *rest):
    page_refs = rest[:n_pg]
    o_ref, wst_ref, xs_s, rowst_s, win_s = rest[n_pg:]
    s = pl.program_id(1)
    tq = nq_ref.shape[1]
    n_steps = rowst_s.shape[0]
    slab = n_pg * TK
    chunks_per_page = TK // CMP_STRIDE
    o_idx = _row_iota((TK, TK))
    perm = jnp.where(_lane_iota((TK, TK)) == CMP_STRIDE * (o_idx % chunks_per_page) + o_idx // chunks_per_page,
                     1.0, 0.0).astype(BF16)
    for k, pg in enumerate(page_refs):
        c0 = pl.multiple_of((s * n_pg + k) * chunks_per_page, chunks_per_page)
        by_offset = _dot_nt(perm, pg[0, 0, 0:LANES, :])
        for off in range(CMP_STRIDE):
            xs_s[off // 2, pl.ds(c0, chunks_per_page), (off % 2) * LANES:(off % 2 + 1) * LANES] = (
                by_offset[off * chunks_per_page:(off + 1) * chunks_per_page])
        rowst_s[s, :, k * TK:(k + 1) * TK] = pg[0, 0, LANES:2 * LANES, :]

    @pl.when(s == pl.num_programs(1) - 1)
    def _():
        w_rows = win_s.shape[0]
        wb = cwin_ref.shape[2]
        nsp = onehot_ref.shape[0]
        c_new = past_len // CMP_STRIDE
        first_row = _row_iota((SUBLANES, LANES)) == 0
        for off in range(CMP_STRIDE):
            new = rown_ref[0, off:off + 1, 0:LANES] if off < tq else jnp.zeros((1, LANES), F32)
            xs_s[off // 2, c_new:c_new + SUBLANES, (off % 2) * LANES:(off % 2 + 1) * LANES] = (
                jnp.where(first_row, new, 0.0))
        win_s[0:wb, :] = cwin_ref[0, 0]
        win_s[wb:wb + tq, :] = winn_ref[0]
        win_s[wb + tq:w_rows, :] = jnp.zeros((w_rows - wb - tq, LANES), F32)
        wst_ref[0] = win_s[tq:wb + tq, :]

        rows4 = N_HEADS * tq
        qpos_rows = past_len + lax.broadcasted_iota(jnp.int32, (rows4, 1), 0) % tq
        qpos_q = past_len + lax.broadcasted_iota(jnp.int32, (tq, 1), 0)
        qp16 = _stack_heads_128(nq_ref[0]).astype(BF16)
        qw16 = _stack_heads_128(nqr_ref[0]).astype(BF16)

        kvc = _compress_chunks(xs_s, wcc_ref)
        o_c, selb = _cmp_select(qp16, kvc, qpos_rows, qpos_q, tq, nsp, n_top)

        sel_bias = (selb.astype(F32) - 1.0) * (-NEG_BIG)
        q_aug = jnp.concatenate([_stack_heads_128(nqr_ref[0]), jnp.concatenate([sel_bias] * N_HEADS, axis=0)],
                                axis=1).astype(BF16)

        def slc_step(s_t, kv_t, st):
            m, l, acc = st
            m_new = jnp.maximum(m, jnp.max(s_t, axis=-1, keepdims=True))
            p = jnp.exp(s_t - m_new)
            alpha = jnp.exp(m - m_new)
            l = alpha * l + jnp.sum(p, axis=-1, keepdims=True)
            return m_new, l, alpha * acc + _dot_nt(p.astype(BF16), kv_t)

        def slab_body(kt, st):
            kv_t = rowst_s[kt].astype(BF16)
            k0 = pl.multiple_of(kt * slab, slab)
            k_aug_t = jnp.concatenate([kv_t, onehot_ref[:, pl.ds(k0, slab)]], axis=0)
            return slc_step(jnp.dot(q_aug, k_aug_t, preferred_element_type=F32), kv_t, st)

        st = lax.fori_loop(0, n_steps, slab_body, _attn_init(rows4, LANES))
        kv_t = rownt_ref[0].astype(BF16)
        s_new = jnp.dot(q_aug, jnp.concatenate([kv_t, onehot_ref[:, past_len:past_len + TK]], axis=0),
                        preferred_element_type=F32)
        s_new = jnp.where(past_len + _lane_iota((rows4, TK)) <= qpos_rows, s_new, NEG_BIG)
        _, l_s, acc_s = slc_step(s_new, kv_t, st)
        o_s = acc_s / l_s

        carry = _attn_init(rows4, LANES)
        for kt in range(w_rows // TK):
            kpos = past_len - wb + kt * TK + _lane_iota((rows4, TK))
            carry = _win_tile(win_s[kt * TK:(kt + 1) * TK, :].astype(BF16), kpos, carry, qw16, qpos_rows)
        _, l_w, acc_w = carry
        o_ref[0] = _nsa_combine(ng_ref[0], o_c, o_s, acc_w / l_w, tq)


def _nsa_decode_call(nq3, nqr3, ng3, rown3, rownt3, winn3, cache_nsa_t, cache_win4, layer, page_table, wc):
    b, tq, _ = nq3.shape
    n_pages = page_table.shape[1]
    past_len = n_pages * TK
    wb = cache_win4.shape[2]
    n_pg = math.gcd(2 * PAGES_PER_STEP, n_pages)
    n_steps = n_pages // n_pg
    t_pad = _round_up(past_len + tq, SLC_BLOCK)
    n_cmp = _round_up(t_pad // CMP_STRIDE, SUBLANES)
    t_rows = _round_up(n_cmp * CMP_STRIDE, TK)
    assert t_rows == past_len + TK and tq <= SUBLANES and n_cmp - past_len // CMP_STRIDE == SUBLANES
    w_rows = _round_up(wb + tq, TK)
    n_top = min(N_SEL, t_pad // SLC_BLOCK)
    nsp = _round_up(t_pad // SLC_BLOCK, LANES)
    onehot_t = (jnp.arange(nsp)[:, None] == jnp.arange(t_rows)[None, :] // SLC_BLOCK).astype(BF16)
    wcc = _cmp_pairs(wc)

    def page_spec(k):
        return pl.BlockSpec((1, 1, BRANCH_W, TK), lambda i, s, pt: (layer, pt[i, s * n_pg + k], 0, 0))

    row3 = lambda r, w: pl.BlockSpec((1, r, w), lambda i, s, pt: (i, 0, 0))
    grid_spec = pltpu.PrefetchScalarGridSpec(
        num_scalar_prefetch=1,
        grid=(b, n_steps),
        in_specs=[row3(tq, BRANCH_W), row3(tq, BRANCH_W), row3(tq, LANES), row3(tq, BRANCH_W), row3(LANES, TK),
                  pl.BlockSpec((1, 1, wb, LANES), lambda i, s, pt: (layer, i, 0, 0)),
                  row3(tq, LANES),
                  pl.BlockSpec(onehot_t.shape, lambda i, s, pt: (0, 0)),
                  pl.BlockSpec(wcc.shape, lambda i, s, pt: (0, 0, 0))]
                 + [page_spec(k) for k in range(n_pg)],
        out_specs=[row3(tq, BRANCH_W), row3(wb, LANES)],
        scratch_shapes=[pltpu.VMEM((CMP_STRIDE // 2, n_cmp, 2 * LANES), F32),
                        pltpu.VMEM((n_steps, LANES, n_pg * TK), F32),
                        pltpu.VMEM((w_rows, LANES), F32)],
    )
    return pl.pallas_call(
        functools.partial(_nsa_decode_kernel, n_pg, past_len, n_cmp, n_top),
        grid_spec=grid_spec,
        out_shape=[jax.ShapeDtypeStruct((b, tq, BRANCH_W), F32), jax.ShapeDtypeStruct((b, wb, LANES), F32)],
        compiler_params=_cparams("parallel", "arbitrary"),
        name="nsa_decode",
    )(page_table, nq3, nqr3, ng3, rown3, rownt3, cache_win4, winn3, onehot_t, wcc, *([cache_nsa_t] * n_pg))


def _cmp_weights(w_cmp_k, w_cmp_v):
    z = jnp.zeros_like(w_cmp_k)
    top = jnp.concatenate([w_cmp_k, z], axis=2)
    bot = jnp.concatenate([z, w_cmp_v], axis=2)
    return jnp.concatenate([top, bot], axis=1).astype(BF16)


def _rope_tables(pos):
    posf = pos.astype(F32)[:, None]
    jj = np.arange(LANES) % HEAD_DIM

    def table(half, theta, lanes_on):
        inv = theta ** (-jnp.arange(half, dtype=F32) / half)
        ang = posf * inv[None, :]
        cos_l = jnp.cos(ang)[:, jj % half]
        sin_l = jnp.sin(ang)[:, jj % half] * jnp.where(jj < half, -1.0, 1.0).astype(F32)[None, :]
        on = jnp.asarray((jj < 2 * half) & lanes_on)[None, :]
        return jnp.where(on, cos_l, 1.0), jnp.where(on, sin_l, 0.0)

    every = np.ones(LANES, bool)
    first = np.arange(LANES) < HEAD_DIM
    parts = (*table(HEAD_DIM // 2, RET_THETA, every), *table(ROPE_DIM // 2, ROPE_THETA, every),
             *table(ROPE_DIM // 2, ROPE_THETA, first))
    return jnp.concatenate(parts, axis=1)


def _layer_weights(i, p):
    w_t = p["w_in_t"][:, i, :]
    d = w_t.shape[1]
    c_mg = w_t.shape[0] - N_BRANCH * d
    c_nsa = 8 * BRANCH_W
    w_nsa = w_t[c_nsa:c_mg]
    w_nsa = jnp.pad(w_nsa, ((0, 3 * BRANCH_W - w_nsa.shape[0]), (0, 0)))
    w = {
        "w_ret": w_t[0:4 * BRANCH_W].astype(BF16), "w_u": w_t[4 * BRANCH_W:5 * BRANCH_W].astype(BF16),
        "w_sb": w_t[5 * BRANCH_W:8 * BRANCH_W].astype(BF16), "w_nsa": w_nsa.astype(BF16),
        "w_mg": w_t[c_mg:].astype(BF16), "w_branch": p["w_branch"][i].astype(BF16),
        "w_out": p["w_out"][i].astype(BF16), "ln1_g": p["ln1_g"][i][None, :], "ln1_b": p["ln1_b"][i][None, :],
        "w_up": p["w_ffn_up"][i].astype(BF16), "w_dn": p["w_ffn_down"][i].astype(BF16),
        "ln2_g": p["ln2_g"][i][None, :], "ln2_b": p["ln2_b"][i][None, :],
        "wc": _cmp_weights(p["w_cmp_k"][i], p["w_cmp_v"][i]),
    }
    s5 = _s5_params(p["ssm_a_re"][i], p["ssm_a_im"][i], p["ssm_b_re"][i], p["ssm_b_im"][i], p["ssm_c_re"][i],
                    p["ssm_c_im"][i], p["ssm_d"][i], p["ssm_log_dt"][i], p["w_glu"][i], p["b_glu"][i])
    return w, s5


def _ssm_rows(s):
    b = s.shape[0]
    return jnp.moveaxis(s, -1, 1).reshape(b, 2, -1)


def _ssm_state(rows, g):
    b = rows.shape[0]
    return jnp.moveaxis(rows.reshape(b, 2, g, -1), 1, -1)


def _layer(x, mod, tabs, w, s5, alpha, past):
    b, l, d = x.shape
    t = b * l
    x2 = x.reshape(t, d)
    g = s5["a_re"].shape[-1] // SSM_STATE
    if past is None:
        tm = min(l, 512)
        tpb = l // tm
        mod_in, tab_in = mod, tabs
    else:
        tm, tpb = t, None
        mod_in = jnp.repeat(mod.reshape(b, 8 * d), l, axis=0)
        tab_in = jnp.tile(tabs, (b, 1))
    ret, u, sbq, sbkv, sbkv16, nq, nqr, rows, win, ng, rowsc, rowss16, win16, *rowst16 = _in_call(
        x2, mod_in, tab_in, w, tpb, tm)

    if past is None:
        r0 = jnp.zeros((b, BRANCH_W, BRANCH_W), F32)
        x0 = jnp.zeros((b, 2, g * SSM_STATE), F32)
    else:
        r0 = _bd_from_heads(past["ret"])
        x0 = _ssm_rows(past["ssm"])
    out_a, r_bd = _ret_call(ret, b, l, r0)
    out_b, s_rows = _s5_call(u, b, l, x0, s5)

    if past is None:
        out_c = _sb_prompt_call(sbq, sbkv16, b, l)
        out_d = _nsa_prompt_call(nq, nqr, ng, rowsc, rowss16, rowst16[0], win16, rowst16[1], w["wc"], b, l)
        win_state = win.reshape(b, l, 2, HEAD_DIM)[:, l - min(WINDOW, l):]
    else:
        kvn_t = jnp.swapaxes(jnp.pad(sbkv16.reshape(b, l, 2 * BRANCH_W), ((0, 0), (0, TK - l), (0, 0))), 1, 2)
        out_c = _sb_decode_call(sbq.reshape(b, l, BRANCH_W), kvn_t, past["sb"], past["layer"], past["page_table"])
        rows3 = rows.reshape(b, l, BRANCH_W)
        rown_t = jnp.swapaxes(jnp.pad(rows3[:, :, LANES:], ((0, 0), (0, TK - l), (0, 0))), 1, 2)
        out_d, wst = _nsa_decode_call(nq.reshape(b, l, BRANCH_W), nqr.reshape(b, l, BRANCH_W),
                                      ng.reshape(b, l, LANES), rows3, rown_t,
                                      win.reshape(b, l, LANES), past["nsa"], past["win"], past["layer"],
                                      past["page_table"], w["wc"])
        out_c = out_c.reshape(t, BRANCH_W)
        out_d = out_d.reshape(t, BRANCH_W)
        win_state = wst.reshape(b, -1, 2, HEAD_DIM)

    tm2 = min(tm, 512)
    tpb2 = None if tpb is None else l // tm2
    x1 = _merge_call(x2, mod_in, [out_a, out_b, out_c, out_d], w, tpb2, tm2, alpha)
    xo = _ffn_call(x1, mod_in, w, tpb2, tm2, alpha)
    state = (sbkv.reshape(b, l, 2, N_HEADS, HEAD_DIM), rows.reshape(b, l, 4, HEAD_DIM), win_state,
             _heads_from_bd(r_bd), _ssm_state(s_rows, g))
    return xo.reshape(b, l, d), state


def kernel(x_prompt, x_sample, c_prompt, c_sample, cache_sb, cache_nsa, cache_win, state_ret, state_ssm, page_table, w_ada, b_ada, w_in, ssm_a_re, ssm_a_im, ssm_b_re, ssm_b_im, ssm_c_re, ssm_c_im, ssm_d, ssm_log_dt, w_glu, b_glu, w_cmp_k, w_cmp_v, w_branch, w_out, ln1_g, ln1_b, w_ffn_up, w_ffn_down, ln2_g, ln2_b):
    p = dict(w_in_t=jnp.transpose(w_in, (2, 0, 1)), ssm_a_re=ssm_a_re, ssm_a_im=ssm_a_im, ssm_b_re=ssm_b_re, ssm_b_im=ssm_b_im,
             ssm_c_re=ssm_c_re, ssm_c_im=ssm_c_im, ssm_d=ssm_d, ssm_log_dt=ssm_log_dt, w_glu=w_glu, b_glu=b_glu,
             w_cmp_k=w_cmp_k, w_cmp_v=w_cmp_v, w_branch=w_branch, w_out=w_out, ln1_g=ln1_g, ln1_b=ln1_b,
             w_ffn_up=w_ffn_up, w_ffn_down=w_ffn_down, ln2_g=ln2_g, ln2_b=ln2_b)
    depth, d = w_ada.shape[0], w_ada.shape[1]
    bp, lp, _ = x_prompt.shape
    bs, ls, _ = x_sample.shape
    n_pool, page = cache_sb.shape[1], cache_sb.shape[2]
    past_len = page_table.shape[1] * page
    alpha = (2.0 * depth) ** 0.25

    c_all = jnp.concatenate([c_prompt, c_sample], axis=0)
    m_rows = _round_up(bp + bs, SUBLANES)
    c_all = jnp.pad(c_all, ((0, m_rows - bp - bs), (0, 0)))
    mods = _ada_call(c_all, w_ada, b_ada)

    tabs_p = _rope_tables(jnp.arange(lp, dtype=jnp.int32))
    tabs_s = _rope_tables(past_len + jnp.arange(ls, dtype=jnp.int32))
    cache_sb4 = jnp.transpose(cache_sb, (0, 1, 3, 4, 5, 2)).reshape(depth, n_pool, 2 * BRANCH_W, page)
    cache_nsa4 = jnp.transpose(cache_nsa, (0, 1, 3, 4, 2)).reshape(depth, n_pool, BRANCH_W, page)
    cache_win4 = cache_win.reshape(depth, bs, cache_win.shape[2], LANES)

    xp, xs = x_prompt, x_sample
    st_p = [[] for _ in range(5)]
    st_s = [[] for _ in range(5)]
    for i in range(depth):
        w, s5 = _layer_weights(i, p)
        mod_i = jnp.pad(mods[i].reshape(m_rows, 6, d), ((0, 0), (0, 2), (0, 0)))
        xp, new_p = _layer(xp, mod_i[:bp], tabs_p, w, s5, alpha, None)
        past = dict(sb=cache_sb4, nsa=cache_nsa4, win=cache_win4, ret=state_ret[i], ssm=state_ssm[i],
                    layer=i, page_table=page_table)
        xs, new_s = _layer(xs, mod_i[bp:bp + bs], tabs_s, w, s5, alpha, past)
        for k in range(5):
            st_p[k].append(new_p[k])
            st_s[k].append(new_s[k])
    sb_p, nsa_p, win_p, ret_p, ssm_p = [jnp.stack(s, axis=0) for s in st_p]
    sb_s, nsa_s, win_s, ret_s, ssm_s = [jnp.stack(s, axis=0) for s in st_s]
    return (xp, xs, sb_p, sb_s, nsa_p, nsa_s, win_p, win_s, ret_p, ret_s, ssm_p, ssm_s)
```
